```python
import math
import jax
import jax.numpy as jnp
from jax import lax
import numpy as np

D_MODEL = 4096
BATCH = 2
SEQ = 8192
DEPTH = 2

MIX_WIDTH = D_MODEL
C_A = MIX_WIDTH // 4
N_A = 64
H_A = C_A // N_A
W_LORA = 64
A_LORA = 64
V_LORA = 32
G_LORA = 160
C_B = MIX_WIDTH // 4
H_B = 8
D_B = C_B // H_B
CHUNK = 128
C_C = MIX_WIDTH // 4
H_C = 8
D_C = C_C // H_C
C_D = MIX_WIDTH - C_A - C_B - C_C
H_D = 8
DV_D = C_D // H_D
DH_D = DV_D // 2
QBLK = 128
D_FF = 11008
N_EXPERTS = 8
TOP_K = 2
D_FF_EXPERT = 5632
P_DIM = 256
ROPE_THETA = 10000.0
RMS_EPS = 1e-6
LN_EPS = 1e-5
GN_EPS = 64e-5
L2_EPS = 1e-12

N_RWKV = 3 * C_A + W_LORA + A_LORA + G_LORA
N_GMLP = 2 * C_B
N_SB = 3 * C_C
N_DIFF = 3 * C_D
N_IN0 = N_RWKV + N_GMLP + N_SB + N_DIFF
N_IN_REST = N_IN0 + V_LORA
N_DENSE_LAYERS = (DEPTH + 1) // 2
N_MOE_LAYERS = DEPTH // 2

kernel_name = 'hybrid_rwkv7_gmlp_stickbreak_diffattn_moe_block'


def split_cols(z, sizes):
    cuts = [int(c) for c in np.cumsum(sizes)[:-1]]
    return jnp.split(z, cuts, axis=-1)


def rmsnorm(x, g, eps=RMS_EPS):
    xf = x.astype(jnp.float32)
    y = xf * lax.rsqrt(jnp.mean(xf * xf, axis=-1, keepdims=True) + eps)
    return (y * g.astype(jnp.float32)).astype(x.dtype)


def layernorm(x, g, b, eps=LN_EPS):
    xf = x.astype(jnp.float32)
    mu = jnp.mean(xf, axis=-1, keepdims=True)
    var = jnp.mean(jnp.square(xf - mu), axis=-1, keepdims=True)
    y = (xf - mu) * lax.rsqrt(var + eps)
    return (y * g.astype(jnp.float32) + b.astype(jnp.float32)).astype(x.dtype)


def token_shift(z, mu):
    z_prev = jnp.pad(z[:, :-1], ((0, 0), (1, 0), (0, 0)))
    return z + (z_prev - z) * mu


def rope_tables(positions, dim):
    inv_freq = 1.0 / (ROPE_THETA ** (jnp.arange(0, dim, 2, dtype=jnp.float32) / dim))
    ang = positions.astype(jnp.float32)[..., None] * inv_freq
    return jnp.cos(ang), jnp.sin(ang)


def apply_rope(x, cos, sin):
    c = cos[:, :, None, None, :]
    s = sin[:, :, None, None, :]
    x1, x2 = jnp.split(x.astype(jnp.float32), 2, axis=-1)
    return jnp.concatenate([x1 * c - x2 * s, x2 * c + x1 * s], axis=-1).astype(x.dtype)


def rwkv7_scan(r, w, k, v, a_, b_):
    B, T, H, N = r.shape

    def step(S, inp):
        r_t, w_t, k_t, v_t, a_t, b_t = inp
        sa = jnp.einsum('bhij,bhj->bhi', S, a_t)
        S = S * w_t[:, :, None, :] + sa[..., None] * b_t[:, :, None, :] + v_t[..., None] * k_t[:, :, None, :]
        return S, jnp.einsum('bhij,bhj->bhi', S, r_t)

    xs = tuple(jnp.moveaxis(t, 1, 0) for t in (r, w, k, v, a_, b_))
    S0 = jnp.zeros((B, H, N, N), jnp.float32)
    _, y = lax.scan(step, S0, xs)
    return jnp.moveaxis(y, 0, 1)


def rwkv7_time_mix(r, w_dn, k, v, a_dn, g_dn, w0, w_up, a0, a_up, g_up, k_k, k_a, r_k, lnx_g, lnx_b):
    B, T, C = r.shape
    f32 = jnp.float32

    def heads(t):
        return t.astype(f32).reshape(B, T, H_A, N_A)

    w = -jax.nn.softplus(-(w0 + jnp.tanh(w_dn) @ w_up)) - 0.5
    decay = jnp.exp(-jnp.exp(w.astype(f32)))
    a = jax.nn.sigmoid(a0 + a_dn @ a_up)
    g = jax.nn.sigmoid(g_dn) @ g_up
    kk = heads(k * k_k)
    kk = kk / jnp.maximum(jnp.linalg.norm(kk, axis=-1, keepdims=True), L2_EPS)
    k = k * (1.0 + (a - 1.0) * k_a)
    rh, kh, vh = heads(r), heads(k), heads(v)
    y = rwkv7_scan(rh, heads(decay), kh, vh, -kk, kk * heads(a))
    mu = jnp.mean(y, axis=-1, keepdims=True)
    var = jnp.mean(jnp.square(y - mu), axis=-1, keepdims=True)
    y = ((y - mu) * lax.rsqrt(var + GN_EPS)).reshape(B, T, C) * lnx_g + lnx_b
    bonus = jnp.sum(rh * kh * r_k.astype(f32), axis=-1, keepdims=True) * vh
    y = (y + bonus.reshape(B, T, C)) * g
    return y.astype(r.dtype)


def gmlp_spatial_gating(u, v, ln_g, ln_b, w_s, b_s):
    B, T, C = u.shape
    u = jax.nn.gelu(u, approximate=False)
    v = layernorm(jax.nn.gelu(v, approximate=False), ln_g, ln_b)
    vb = v.reshape(B, T // CHUNK, CHUNK, H_B, D_B)
    causal = jnp.tril(jnp.ones((CHUNK, CHUNK), dtype=bool))
    w = jnp.where(causal, w_s, jnp.zeros_like(w_s))
    s = jnp.einsum('hts,bcshd->bcthd', w, vb) + jnp.swapaxes(b_s, 0, 1)[:, :, None]
    return u * s.reshape(B, T, C)


def stick_breaking_attention(q, k, v):
    B, T, H, D = q.shape
    nb = T // QBLK
    scale = D ** -0.5
    qb = q.reshape(B, nb, QBLK, H, D).transpose(1, 0, 3, 2, 4)
    kt = k.transpose(0, 2, 1, 3)
    vt = v.transpose(0, 2, 1, 3)
    s_idx = jnp.arange(T)

    def block(args):
        bi, q_blk = args
        t_idx = bi * QBLK + jnp.arange(QBLK)
        z = jnp.einsum('bhqd,bhsd->bhqs', q_blk, kt).astype(jnp.float32) * scale
        mask = s_idx[None, :] < t_idx[:, None]
        log_beta = jax.nn.log_sigmoid(z)
        log_1mb = jnp.where(mask, jax.nn.log_sigmoid(-z), 0.0)
        after = lax.cumsum(log_1mb, axis=3, reverse=True) - log_1mb
        att = jnp.where(mask, jnp.exp(log_beta + after), 0.0)
        return jnp.einsum('bhqs,bhsd->bhqd', att.astype(vt.dtype), vt)

    out = lax.map(block, (jnp.arange(nb), qb))
    return out.transpose(1, 0, 3, 2, 4).reshape(B, T, H, D)


def differential_attention(q, k, v, lam):
    B, T, H, _, Dh = q.shape
    nb = T // QBLK
    scale = Dh ** -0.5
    qb = q.reshape(B, nb, QBLK, H, 2, Dh).transpose(1, 0, 3, 4, 2, 5)
    kt = k.transpose(0, 2, 3, 1, 4)
    vt = v.transpose(0, 2, 1, 3)
    s_idx = jnp.arange(T)

    def block(args):
        bi, q_blk = args
        t_idx = bi * QBLK + jnp.arange(QBLK)
        s = jnp.einsum('bhmqd,bhmsd->bhmqs', q_blk, kt).astype(jnp.float32) * scale
        mask = s_idx[None, :] <= t_idx[:, None]
        prob = jax.nn.softmax(jnp.where(mask, s, -jnp.inf), axis=-1)
        att = prob[:, :, 0] - lam * prob[:, :, 1]
        return jnp.einsum('bhqs,bhsd->bhqd', att.astype(vt.dtype), vt)

    out = lax.map(block, (jnp.arange(nb), qb))
    return out.transpose(1, 0, 3, 2, 4).reshape(B, T, H, v.shape[-1])


def swiglu(x, w_gate, w_up, w_down):
    return (jax.nn.silu(x @ w_gate) * (x @ w_up)) @ w_down


def moe_swiglu(x, w_router, w_gate, w_up, w_down):
    B, T, D = x.shape
    xt = x.reshape(B * T, D)
    logits = (xt @ w_router).astype(jnp.float32)
    top_logit, top_idx = lax.top_k(logits, TOP_K)
    gates = jax.nn.softmax(top_logit, axis=-1)
    combine = jnp.sum(jax.nn.one_hot(top_idx, N_EXPERTS, dtype=jnp.float32) * gates[..., None], axis=1)
    out = jnp.zeros_like(xt)
    for e in range(N_EXPERTS):
        out = out + combine[:, e:e + 1].astype(xt.dtype) * swiglu(xt, w_gate[e], w_up[e], w_down[e])
    return out.reshape(B, T, D)


def setup_inputs(seed: int = 0) -> dict:
    key = jax.random.key(seed)
    ks = iter(jax.random.split(key, 64))

    def nrm(shape, scale):
        return jax.random.normal(next(ks), shape, jnp.float32) * scale

    def gain(shape):
        return 1.0 + nrm(shape, 0.02)

    def unif(shape):
        return jax.random.uniform(next(ks), shape, dtype=jnp.float32, minval=0.05, maxval=0.95)

    L, LR = DEPTH, DEPTH - 1
    ND, NM = N_DENSE_LAYERS, N_MOE_LAYERS
    return dict(
        x=nrm((BATCH, SEQ, D_MODEL), 1.0),
        p=nrm((DEPTH, BATCH, SEQ, P_DIM), 1.0),
        positions=jnp.broadcast_to(jnp.arange(SEQ, dtype=jnp.int32), (BATCH, SEQ)),
        attn_norm=gain((L, D_MODEL)),
        w_in0=nrm((D_MODEL, N_IN0), D_MODEL ** -0.5),
        w_in_rest=nrm((LR, D_MODEL, N_IN_REST), D_MODEL ** -0.5),
        w_out=nrm((L, MIX_WIDTH, D_MODEL), MIX_WIDTH ** -0.5),
        rwkv_mu=unif((L, N_RWKV)),
        rwkv_mu_v=unif((LR, V_LORA)),
        rwkv_w0=nrm((L, C_A), 1.0),
        rwkv_w_up=nrm((L, W_LORA, C_A), 0.1),
        rwkv_a0=nrm((L, C_A), 0.5),
        rwkv_a_up=nrm((L, A_LORA, C_A), 0.1),
        rwkv_v0=nrm((LR, C_A), 0.5),
        rwkv_v_up=nrm((LR, V_LORA, C_A), 0.1),
        rwkv_g_up=nrm((L, G_LORA, C_A), G_LORA ** -0.5),
        rwkv_k_k=0.85 + nrm((L, C_A), 0.05),
        rwkv_k_a=1.0 + nrm((L, C_A), 0.05),
        rwkv_r_k=nrm((L, H_A, N_A), 0.1),
        rwkv_lnx_g=gain((L, C_A)),
        rwkv_lnx_b=nrm((L, C_A), 0.01),
        gm_ln_g=gain((L, C_B)),
        gm_ln_b=nrm((L, C_B), 0.01),
        gm_w_s=nrm((L, H_B, CHUNK, CHUNK), CHUNK ** -0.5),
        gm_b_s=1.0 + nrm((L, H_B, CHUNK), 0.01),
        gm_out_g=gain((L, C_B)),
        sb_out_g=gain((L, C_C)),
        diff_lambda=nrm((L, 4, DH_D), 0.1),
        diff_subln=gain((L, DV_D)),
        ffn_norm=gain((L, D_MODEL)),
        dense_w_gate=nrm((ND, D_MODEL, D_FF), D_MODEL ** -0.5),
        dense_w_up=nrm((ND, D_MODEL, D_FF), D_MODEL ** -0.5),
        dense_w_down=nrm((ND, D_FF, D_MODEL), D_FF ** -0.5),
        moe_router=nrm((NM, D_MODEL, N_EXPERTS), D_MODEL ** -0.5),
        moe_w_gate=nrm((NM, N_EXPERTS, D_MODEL, D_FF_EXPERT), D_MODEL ** -0.5),
        moe_w_up=nrm((NM, N_EXPERTS, D_MODEL, D_FF_EXPERT), D_MODEL ** -0.5),
        moe_w_down=nrm((NM, N_EXPERTS, D_FF_EXPERT, D_MODEL), D_FF_EXPERT ** -0.5),
        ple_norm=gain((L, D_MODEL)),
        ple_w_gate=nrm((L, D_MODEL, D_MODEL), D_MODEL ** -0.5),
        ple_w_proj=nrm((L, P_DIM, D_MODEL), P_DIM ** -0.5),
        final_norm=gain((D_MODEL,)),
    )


def reference(x, p, positions, attn_norm, w_in0, w_in_rest, w_out, rwkv_mu, rwkv_mu_v, rwkv_w0, rwkv_w_up, rwkv_a0, rwkv_a_up, rwkv_v0, rwkv_v_up, rwkv_g_up, rwkv_k_k, rwkv_k_a, rwkv_r_k, rwkv_lnx_g, rwkv_lnx_b, gm_ln_g, gm_ln_b, gm_w_s, gm_b_s, gm_out_g, sb_out_g, diff_lambda, diff_subln, ffn_norm, dense_w_gate, dense_w_up, dense_w_down, moe_router, moe_w_gate, moe_w_up, moe_w_down, ple_norm, ple_w_gate, ple_w_proj, final_norm):
    B, T, _ = x.shape
    cos, sin = rope_tables(positions, DH_D)
    h = x
    v_first = None
    for i in range(DEPTH):
        n = rmsnorm(h, attn_norm[i])
        w_in = w_in0 if i == 0 else w_in_rest[i - 1]
        proj = n @ w_in
        z_a, z_b, z_c, z_d = split_cols(proj[..., :N_IN0], (N_RWKV, N_GMLP, N_SB, N_DIFF))

        z_a = token_shift(z_a, rwkv_mu[i])
        r_a, w_dn, k_a, v_a, a_dn, g_dn = split_cols(z_a, (C_A, W_LORA, C_A, C_A, A_LORA, G_LORA))
        if i == 0:
            v_first = v_a
        else:
            v_dn = token_shift(proj[..., N_IN0:], rwkv_mu_v[i - 1])
            v_a = v_a + (v_first - v_a) * jax.nn.sigmoid(rwkv_v0[i - 1] + v_dn @ rwkv_v_up[i - 1])
        y_a = rwkv7_time_mix(r_a, w_dn, k_a, v_a, a_dn, g_dn, rwkv_w0[i], rwkv_w_up[i], rwkv_a0[i], rwkv_a_up[i], rwkv_g_up[i], rwkv_k_k[i], rwkv_k_a[i], rwkv_r_k[i], rwkv_lnx_g[i], rwkv_lnx_b[i])

        u_b, v_b = split_cols(z_b, (C_B, C_B))
        y_b = gmlp_spatial_gating(u_b, v_b, gm_ln_g[i], gm_ln_b[i], gm_w_s[i], gm_b_s[i])
        y_b = rmsnorm(y_b.reshape(B, T, H_B, D_B), gm_out_g[i].reshape(H_B, D_B)).reshape(B, T, C_B)

        q_c, k_c, v_c = [t.reshape(B, T, H_C, D_C) for t in split_cols(z_c, (C_C, C_C, C_C))]
        y_c = stick_breaking_attention(q_c, k_c, v_c)
        y_c = rmsnorm(y_c, sb_out_g[i].reshape(H_C, D_C)).reshape(B, T, C_C)

        q_d, k_d, v_d = split_cols(z_d, (C_D, C_D, C_D))
        q_d = apply_rope(q_d.reshape(B, T, H_D, 2, DH_D), cos, sin)
        k_d = apply_rope(k_d.reshape(B, T, H_D, 2, DH_D), cos, sin)
        lam_init = 0.8 - 0.6 * math.exp(-0.3 * i)
        lv = diff_lambda[i].astype(jnp.float32)
        lam = jnp.exp(jnp.sum(lv[0] * lv[1])) - jnp.exp(jnp.sum(lv[2] * lv[3])) + lam_init
        y_d = differential_attention(q_d, k_d, v_d.reshape(B, T, H_D, DV_D), lam)
        y_d = (rmsnorm(y_d, diff_subln[i], eps=LN_EPS) * (1.0 - lam_init)).reshape(B, T, C_D)

        mix = jnp.concatenate([y_a, y_b, y_c, y_d], axis=-1)
        h = h + mix @ w_out[i]

        n = rmsnorm(h, ffn_norm[i])
        j = i // 2
        if i % 2 == 0:
            h = h + swiglu(n, dense_w_gate[j], dense_w_up[j], dense_w_down[j])
        else:
            h = h + moe_swiglu(n, moe_router[j], moe_w_gate[j], moe_w_up[j], moe_w_down[j])

        n = rmsnorm(h, ple_norm[i])
        h = h + jax.nn.sigmoid(n @ ple_w_gate[i]) * (p[i] @ ple_w_proj[i])
    return rmsnorm(h, final_norm)
```

```python
import functools
import math

import jax
import jax.numpy as jnp
import numpy as np
from jax import lax
from jax.experimental import pallas as pl
from jax.experimental.pallas import tpu as pltpu

F32 = jnp.float32
BF16 = jnp.bfloat16

D_MODEL = 4096
DEPTH = 2
C_A = 1024
N_A = 64
W_LORA, A_LORA, V_LORA, G_LORA = 64, 64, 32, 160
C_B = 1024
H_B = 8
CHUNK = 128
C_C = 1024
H_C = 8
D_C = 128
C_D = 1024
H_D = 8
DV_D = 128
DH_D = 64
D_FF = 11008
N_EXPERTS = 8
D_FF_EXPERT = 5632
P_DIM = 256
ROPE_THETA = 10000.0
RMS_EPS = 1e-6
LN_EPS = 1e-5
GN_EPS = 64e-5
L2_EPS = 1e-12
N_RWKV = 3 * C_A + W_LORA + A_LORA + G_LORA
N_GMLP = 2 * C_B
N_SB = 3 * C_C
N_DIFF = 3 * C_D
N_IN0 = N_RWKV + N_GMLP + N_SB + N_DIFF

LANES = 128
VMEM_LIMIT = 56 * 1024 * 1024

LORA_PAD = 512
N_A_COLS = 3 * C_A + LORA_PAD
D_FF_PAD = 11264
RWKV_CHUNK = 64
SB_CUTOFF = -110.0

NT = (((1,), (1,)), ((), ()))
TN = (((0,), (0,)), ((), ()))


def _cparams(*sem):
    return pltpu.CompilerParams(dimension_semantics=sem, vmem_limit_bytes=VMEM_LIMIT)


def _dot(a, b, dims=None):
    if dims is None:
        return jnp.dot(a, b, preferred_element_type=F32)
    return lax.dot_general(a, b, dims, preferred_element_type=F32)


def _split2(x):
    hi = x.astype(BF16)
    lo = (x - hi.astype(F32)).astype(BF16)
    return hi, lo


def _split3(x):
    hi = x.astype(BF16)
    r = x - hi.astype(F32)
    mid = r.astype(BF16)
    lo = (r - mid.astype(F32)).astype(BF16)
    return hi, mid, lo


def _pick_tile(n, prefs):
    for t in prefs:
        if n % t == 0:
            return t
    raise ValueError(f"no tile for {n}")


def _rmsnorm_kernel(x_ref, g_ref, o_ref, *, eps):
    x = x_ref[...].astype(F32)
    ms = jnp.mean(x * x, axis=-1, keepdims=True)
    o_ref[...] = (x * lax.rsqrt(ms + eps) * g_ref[...]).astype(o_ref.dtype)


def rmsnorm(x, g, out_dtype, tm=512):
    m, d = x.shape
    return pl.pallas_call(
        functools.partial(_rmsnorm_kernel, eps=RMS_EPS),
        grid=(m // tm,),
        in_specs=[pl.BlockSpec((tm, d), lambda i: (i, 0)), pl.BlockSpec((1, d), lambda i: (0, 0))],
        out_specs=pl.BlockSpec((tm, d), lambda i: (i, 0)),
        out_shape=jax.ShapeDtypeStruct((m, d), out_dtype),
        compiler_params=_cparams("parallel"),
        name="rmsnorm",
    )(x, g.reshape(1, d).astype(F32))


def _mm_kernel(a_ref, b_ref, o_ref):
    o_ref[...] = _dot(a_ref[...], b_ref[...]).astype(o_ref.dtype)


def matmul(a, b, out_dtype, tm=1024):
    m, k = a.shape
    n = b.shape[1]
    tn = _pick_tile(n, (1024, 512, 256, 128))
    return pl.pallas_call(
        _mm_kernel,
        grid=(m // tm, n // tn),
        in_specs=[pl.BlockSpec((tm, k), lambda i, j: (i, 0)), pl.BlockSpec((k, tn), lambda i, j: (0, j))],
        out_specs=pl.BlockSpec((tm, tn), lambda i, j: (i, j)),
        out_shape=jax.ShapeDtypeStruct((m, n), out_dtype),
        compiler_params=_cparams("parallel", "arbitrary"),
        name="matmul",
    )(a, b)


def _mix_out_kernel(ya_ref, yb_ref, yc_ref, yd_ref, w_ref, res_ref, o_ref):
    kq = ya_ref.shape[1]
    acc = res_ref[...]
    for idx, y_ref in enumerate((ya_ref, yb_ref, yc_ref, yd_ref)):
        acc = acc + _dot(y_ref[...], w_ref[idx * kq:(idx + 1) * kq, :])
    o_ref[...] = acc


def mix_out_proj(ys, w, res, tm=1024, tn=512):
    m, kq = ys[0].shape
    k, n = w.shape
    y_spec = pl.BlockSpec((tm, kq), lambda i, j: (i, 0))
    return pl.pallas_call(
        _mix_out_kernel,
        grid=(m // tm, n // tn),
        in_specs=[y_spec, y_spec, y_spec, y_spec,
                  pl.BlockSpec((k, tn), lambda i, j: (0, j)),
                  pl.BlockSpec((tm, tn), lambda i, j: (i, j))],
        out_specs=pl.BlockSpec((tm, tn), lambda i, j: (i, j)),
        out_shape=jax.ShapeDtypeStruct((m, n), F32),
        compiler_params=_cparams("parallel", "arbitrary"),
        name="mix_out_proj",
    )(*ys, w, res)


def _swiglu_up_kernel(a_ref, wg_ref, wu_ref, o_ref):
    a = a_ref[...]
    g = _dot(a, wg_ref[...])
    u = _dot(a, wu_ref[...])
    o_ref[...] = (g * jax.nn.sigmoid(g) * u).astype(o_ref.dtype)


def swiglu_up(a, wg, wu, tm=1024, tn=512):
    m, k = a.shape
    n = wg.shape[1]
    w_spec = pl.BlockSpec((k, tn), lambda i, j: (0, j))
    return pl.pallas_call(
        _swiglu_up_kernel,
        grid=(m // tm, n // tn),
        in_specs=[pl.BlockSpec((tm, k), lambda i, j: (i, 0)), w_spec, w_spec],
        out_specs=pl.BlockSpec((tm, tn), lambda i, j: (i, j)),
        out_shape=jax.ShapeDtypeStruct((m, n), BF16),
        compiler_params=_cparams("parallel", "arbitrary"),
        name="swiglu_up",
    )(a, wg, wu)


def _down_res_kernel(a_ref, b_ref, res_ref, o_ref, acc_ref):
    kk = pl.program_id(2)

    @pl.when(kk == 0)
    def _():
        acc_ref[...] = res_ref[...]

    acc_ref[...] += _dot(a_ref[...], b_ref[...])

    @pl.when(kk == pl.num_programs(2) - 1)
    def _():
        o_ref[...] = acc_ref[...]


def down_proj_residual(a, b, res, tm=1024, tn=1024, tk=2816):
    m, k = a.shape
    n = b.shape[1]
    return pl.pallas_call(
        _down_res_kernel,
        grid=(m // tm, n // tn, k // tk),
        in_specs=[pl.BlockSpec((tm, tk), lambda i, j, kk: (i, kk)),
                  pl.BlockSpec((tk, tn), lambda i, j, kk: (kk, j)),
                  pl.BlockSpec((tm, tn), lambda i, j, kk: (i, j))],
        out_specs=pl.BlockSpec((tm, tn), lambda i, j, kk: (i, j)),
        out_shape=jax.ShapeDtypeStruct((m, n), F32),
        scratch_shapes=[pltpu.VMEM((tm, tn), F32)],
        compiler_params=_cparams("parallel", "parallel", "arbitrary"),
        name="down_proj_residual",
    )(a, b, res)


def _ple_kernel(n_ref, wg_ref, p_ref, wp_ref, res_ref, o_ref):
    gate = jax.nn.sigmoid(_dot(n_ref[...], wg_ref[...]))
    emb = _dot(p_ref[...], wp_ref[...])
    o_ref[...] = res_ref[...] + gate * emb


def ple_residual(n, wg, p, wp, res, tm=1024, tn=512):
    m, k = n.shape
    nn = wg.shape[1]
    pd = p.shape[1]
    return pl.pallas_call(
        _ple_kernel,
        grid=(m // tm, nn // tn),
        in_specs=[pl.BlockSpec((tm, k), lambda i, j: (i, 0)),
                  pl.BlockSpec((k, tn), lambda i, j: (0, j)),
                  pl.BlockSpec((tm, pd), lambda i, j: (i, 0)),
                  pl.BlockSpec((pd, tn), lambda i, j: (0, j)),
                  pl.BlockSpec((tm, tn), lambda i, j: (i, j))],
        out_specs=pl.BlockSpec((tm, tn), lambda i, j: (i, j)),
        out_shape=jax.ShapeDtypeStruct((m, nn), F32),
        compiler_params=_cparams("parallel", "arbitrary"),
        name="ple_residual",
    )(n, wg, p, wp, res)


def _head_sum(x, ones_bd):
    outs = []
    for p in range(x.shape[1] // LANES):
        hi, lo = _split2(x[:, p * LANES:(p + 1) * LANES])
        outs.append(_dot(hi, ones_bd) + _dot(lo, ones_bd))
    return jnp.concatenate(outs, axis=1)


def _rwkv_kernel(z_ref, vf_ref, mu_ref, vec_ref, uw_ref, ua_ref, ug_ref, uv_ref,
                 y_ref, vout_ref, prev_ref, h_ref, *, has_v_res):
    tc = RWKV_CHUNK
    c = pl.program_id(1)

    @pl.when(c == 0)
    def _():
        prev_ref[...] = jnp.zeros_like(prev_ref)
        h_ref[...] = jnp.zeros_like(h_ref)

    z = z_ref[...]
    row = lax.broadcasted_iota(jnp.int32, z.shape, 0)
    zp = jnp.where(row == 0, prev_ref[...], pltpu.roll(z, 1, 0))
    prev_ref[...] = z[tc - 1:tc, :]
    zs = z + (zp - z) * mu_ref[...]
    r = zs[:, 0:C_A]
    k = zs[:, C_A:2 * C_A]
    v = zs[:, 2 * C_A:3 * C_A]
    lr = zs[:, 3 * C_A:3 * C_A + LORA_PAD]

    w0 = vec_ref[0:1, :]
    a0 = vec_ref[1:2, :]
    v0 = vec_ref[2:3, :]
    k_k = vec_ref[3:4, :]
    k_a = vec_ref[4:5, :]
    r_k = vec_ref[5:6, :]
    lnx_g = vec_ref[6:7, :]
    lnx_b = vec_ref[7:8, :]

    lr_b = lr.astype(BF16)
    w_lin = _dot(jnp.tanh(lr).astype(BF16), uw_ref[...])
    a_lin = _dot(lr_b, ua_ref[...])
    g = _dot(jax.nn.sigmoid(lr).astype(BF16), ug_ref[...])
    w = -jax.nn.softplus(-(w0 + w_lin)) - 0.5
    ld = -jnp.exp(w)
    a_lr = jax.nn.sigmoid(a0 + a_lin)
    if has_v_res:
        v_lin = _dot(lr_b, uv_ref[...])
        v = v + (vf_ref[...] - v) * jax.nn.sigmoid(v0 + v_lin)
    vout_ref[...] = v

    li = lax.broadcasted_iota(jnp.int32, (LANES, LANES), 0)
    lj = lax.broadcasted_iota(jnp.int32, (LANES, LANES), 1)
    same_head = (li // N_A) == (lj // N_A)
    ones_bd = jnp.where(same_head, 1.0, 0.0).astype(BF16)

    kk = k * k_k
    kk = kk / jnp.maximum(jnp.sqrt(_head_sum(kk * kk, ones_bd)), L2_EPS)
    k = k * (1.0 + (a_lr - 1.0) * k_a)
    a_s = -kk
    b_s = kk * a_lr

    ti = lax.broadcasted_iota(jnp.int32, (tc, tc), 0)
    tj = lax.broadcasted_iota(jnp.int32, (tc, tc), 1)
    tri = jnp.where(ti >= tj, 1.0, 0.0).astype(BF16)
    ld_h, ld_m, ld_l = _split3(ld)
    cum = _dot(tri, ld_h) + _dot(tri, ld_m) + _dot(tri, ld_l)
    cum_end = cum[tc - 1:tc, :]
    p_inc = jnp.exp(cum)
    p_exc = jnp.exp(cum - ld)
    p_inv = jnp.exp(-cum)
    p_end = jnp.exp(cum_end - cum)
    decay_end = jnp.exp(cum_end)

    a_t = a_s * p_exc
    b_t = b_s * p_inv
    k_t = k * p_inv
    r_t = r * p_inc
    b_h = b_s * p_end
    k_h = k * p_end

    first_head = lax.broadcasted_iota(jnp.int32, (tc, LANES), 1) < N_A
    strict = same_head & ((li % N_A) > (lj % N_A))
    incl = same_head & ((li % N_A) >= (lj % N_A))
    eye = li == lj

    def stack(x):
        return jnp.concatenate([jnp.where(first_head, x, 0.0), jnp.where(first_head, 0.0, x)], axis=0)

    def dup(x):
        return jnp.concatenate([x, x], axis=0)

    y_parts = []
    for p in range(C_A // LANES):
        sl = slice(p * LANES, (p + 1) * LANES)
        a_st = stack(a_t[:, sl])
        r_st = stack(r_t[:, sl])
        v_st = stack(v[:, sl]).astype(BF16)
        bh_st = stack(b_h[:, sl]).astype(BF16)
        kh_st = stack(k_h[:, sl]).astype(BF16)
        lhs = jnp.concatenate([a_st, r_st], axis=0).astype(BF16)
        rhs = jnp.concatenate([dup(b_t[:, sl]), dup(k_t[:, sl])], axis=0).astype(BF16)
        x = _dot(lhs, rhs, NT)
        n2 = 2 * tc
        l_ab = jnp.where(strict, x[0:n2, 0:n2], 0.0)
        a_ak = jnp.where(strict, x[0:n2, n2:2 * n2], 0.0).astype(BF16)
        a_rb = jnp.where(incl, x[n2:2 * n2, 0:n2], 0.0).astype(BF16)
        a_rk = jnp.where(incl, x[n2:2 * n2, n2:2 * n2], 0.0).astype(BF16)
        zz = jnp.concatenate([a_st, _dot(a_ak, v_st)], axis=1)
        lk = l_ab
        n_iter = int(math.log2(tc))
        for it in range(n_iter):
            lk_b = lk.astype(BF16)
            zz = zz + _dot(lk_b, zz.astype(BF16))
            if it < n_iter - 1:
                lk = _dot(lk_b, lk_b)
        zz_b = zz.astype(BF16)
        qy = _dot(a_rb, zz_b)
        q = r_st + qy[:, 0:LANES]
        y3 = qy[:, LANES:] + _dot(a_rk, v_st)
        mg = _dot(bh_st, zz_b, TN)
        m_mat = jnp.where(eye, decay_end[:, sl], 0.0) + mg[:, 0:LANES]
        g_mat = mg[:, LANES:] + _dot(kh_st, v_st, TN)
        h_hi, h_lo = _split2(h_ref[p])
        qm = jnp.concatenate([q, m_mat], axis=0).astype(BF16)
        out = _dot(qm, h_hi) + _dot(qm, h_lo)
        y_st = out[0:n2, :] + y3
        y_parts.append(y_st[0:tc, :] + y_st[tc:n2, :])
        h_ref[p] = out[n2:, :] + g_mat
    y = jnp.concatenate(y_parts, axis=1)

    inv_n = 1.0 / N_A
    mean = _head_sum(y, ones_bd) * inv_n
    d = y - mean
    var = _head_sum(d * d, ones_bd) * inv_n
    yn = d * lax.rsqrt(var + GN_EPS) * lnx_g + lnx_b
    bonus = _head_sum(r * k * r_k, ones_bd) * v
    y_ref[...] = ((yn + bonus) * g).astype(y_ref.dtype)


def rwkv_mix(proj_a, v_first, mu, vec, uw, ua, ug, uv, batch, seq, has_v_res):
    tc = RWKV_CHUNK
    nc = seq // tc
    row_spec = lambda w: pl.BlockSpec((tc, w), lambda b, c: (b * nc + c, 0))
    full = lambda s: pl.BlockSpec(s, lambda b, c: tuple(0 for _ in s))
    m = batch * seq
    return pl.pallas_call(
        functools.partial(_rwkv_kernel, has_v_res=has_v_res),
        grid=(batch, nc),
        in_specs=[row_spec(N_A_COLS), row_spec(C_A), full((1, N_A_COLS)), full((8, C_A)),
                  full((LORA_PAD, C_A)), full((LORA_PAD, C_A)), full((LORA_PAD, C_A)), full((LORA_PAD, C_A))],
        out_specs=[row_spec(C_A), row_spec(C_A)],
        out_shape=[jax.ShapeDtypeStruct((m, C_A), BF16), jax.ShapeDtypeStruct((m, C_A), F32)],
        scratch_shapes=[pltpu.VMEM((1, N_A_COLS), F32), pltpu.VMEM((C_A // LANES, LANES, LANES), F32)],
        compiler_params=_cparams("parallel", "arbitrary"),
        name="rwkv_mix",
    )(proj_a, v_first, mu, vec, uw, ua, ug, uv)


def _gelu(x):
    return 0.5 * x * (1.0 + lax.erf(x * math.sqrt(0.5)))


def _gmlp_kernel(z_ref, lng_ref, lnb_ref, w_ref, bs_ref, og_ref, o_ref):
    u = _gelu(z_ref[:, 0:C_B])
    v = _gelu(z_ref[:, C_B:2 * C_B])
    mu = jnp.mean(v, axis=-1, keepdims=True)
    d = v - mu
    var = jnp.mean(d * d, axis=-1, keepdims=True)
    vn = d * lax.rsqrt(var + LN_EPS) * lng_ref[...] + lnb_ref[...]
    ti = lax.broadcasted_iota(jnp.int32, (CHUNK, CHUNK), 0)
    tj = lax.broadcasted_iota(jnp.int32, (CHUNK, CHUNK), 1)
    causal = ti >= tj
    dh = C_B // H_B
    for h in range(H_B):
        sl = slice(h * dh, (h + 1) * dh)
        w = jnp.where(causal, w_ref[h], 0.0).astype(BF16)
        s = _dot(w, vn[:, sl].astype(BF16)) + bs_ref[:, h:h + 1]
        y = u[:, sl] * s
        ms = jnp.mean(y * y, axis=-1, keepdims=True)
        o_ref[:, sl] = (y * lax.rsqrt(ms + RMS_EPS) * og_ref[:, sl]).astype(o_ref.dtype)


def gmlp_mix(proj_bd, ln_g, ln_b, w_s, b_s_t, out_g):
    m = proj_bd.shape[0]
    full = lambda s: pl.BlockSpec(s, lambda i: tuple(0 for _ in s))
    return pl.pallas_call(
        _gmlp_kernel,
        grid=(m // CHUNK,),
        in_specs=[pl.BlockSpec((CHUNK, 2 * C_B), lambda i: (i, 0)),
                  full((1, C_B)), full((1, C_B)), full((H_B, CHUNK, CHUNK)), full((CHUNK, H_B)), full((1, C_B))],
        out_specs=pl.BlockSpec((CHUNK, C_B), lambda i: (i, 0)),
        out_shape=jax.ShapeDtypeStruct((m, C_B), BF16),
        compiler_params=_cparams("parallel"),
        name="gmlp_mix",
    )(proj_bd, ln_g, ln_b, w_s, b_s_t, out_g)


def _sb_kernel(q_ref, k_ref, v_ref, g_ref, o_ref, *, tq, scale):
    qi = pl.program_id(2)
    q = q_ref[...]
    row = lax.broadcasted_iota(jnp.int32, (tq, tq), 0)
    col = lax.broadcasted_iota(jnp.int32, (tq, tq), 1)
    below = col < row
    upper = jnp.where(row > col, 1.0, 0.0).astype(BF16)

    def block(j, carry, acc, diagonal):
        start = pl.multiple_of(j * tq, tq)
        kb = k_ref[pl.ds(start, tq), :]
        vb = v_ref[pl.ds(start, tq), :]
        z = _dot(q, kb, NT) * scale
        log_beta = jnp.minimum(z, 0.0) - jnp.log(1.0 + jnp.exp(-jnp.abs(z)))
        log_1mb = log_beta - z
        if diagonal:
            log_1mb = jnp.where(below, log_1mb, 0.0)
        hi, lo = _split2(log_1mb)
        after = _dot(hi, upper) + _dot(lo, upper)
        att = jnp.exp(log_beta + after + carry)
        if diagonal:
            att = jnp.where(below, att, 0.0)
        acc = acc + _dot(att.astype(BF16), vb)
        carry = carry + after[:, 0:1] + log_1mb[:, 0:1]
        return carry, acc

    carry0 = jnp.zeros((tq, 1), F32)
    acc0 = jnp.zeros((tq, D_C), F32)
    carry, acc = block(qi, carry0, acc0, True)

    def body(i, state):
        return block(qi - 1 - i, state[0], state[1], False)

    carry, acc = lax.fori_loop(0, qi, body, (carry, acc))
    ms = jnp.mean(acc * acc, axis=-1, keepdims=True)
    o_ref[...] = (acc * lax.rsqrt(ms + RMS_EPS) * g_ref[...]).astype(o_ref.dtype)


def sb_attention(proj_cd, out_g, batch, seq, tq=256):
    nq = seq // tq
    return pl.pallas_call(
        functools.partial(_sb_kernel, tq=tq, scale=D_C ** -0.5),
        grid=(batch, H_C, nq),
        in_specs=[pl.BlockSpec((None, tq, D_C), lambda b, h, i: (b, i, h)),
                  pl.BlockSpec((None, seq, D_C), lambda b, h, i: (b, 0, H_C + h)),
                  pl.BlockSpec((None, seq, D_C), lambda b, h, i: (b, 0, 2 * H_C + h)),
                  pl.BlockSpec((1, D_C), lambda b, h, i: (0, h))],
        out_specs=pl.BlockSpec((None, tq, D_C), lambda b, h, i: (b, i, h)),
        out_shape=jax.ShapeDtypeStruct((batch, seq, C_C), BF16),
        compiler_params=_cparams("parallel", "parallel", "arbitrary"),
        name="sb_attention",
    )(proj_cd, proj_cd, proj_cd, out_g)


def _rope_table_kernel(pos_ref, invf_ref, cos_ref, sin_ref):
    ang = pos_ref[...].astype(F32) * invf_ref[...]
    cos_ref[...] = jnp.cos(ang)
    sin_ref[...] = jnp.sin(ang)


def rope_tables(pos_col, invf, tm=512):
    m = pos_col.shape[0]
    spec = pl.BlockSpec((tm, LANES), lambda i: (i, 0))
    return pl.pallas_call(
        _rope_table_kernel,
        grid=(m // tm,),
        in_specs=[pl.BlockSpec((tm, 1), lambda i: (i, 0)), pl.BlockSpec((1, LANES), lambda i: (0, 0))],
        out_specs=[spec, spec],
        out_shape=[jax.ShapeDtypeStruct((m, LANES), F32)] * 2,
        compiler_params=_cparams("parallel"),
        name="rope_tables",
    )(pos_col, invf)


def _rope_kernel(x_ref, cos_ref, sin_ref, o_ref):
    x = x_ref[...]
    width = x.shape[1]
    reps = width // LANES
    c = jnp.concatenate([cos_ref[...]] * reps, axis=1)
    s = jnp.concatenate([sin_ref[...]] * reps, axis=1)
    lane = lax.broadcasted_iota(jnp.int32, x.shape, 1)
    first_half = (lane % DH_D) < (DH_D // 2)
    rot = jnp.where(first_half, -pltpu.roll(x, width - DH_D // 2, 1), pltpu.roll(x, DH_D // 2, 1))
    o_ref[...] = (x * c + rot * s).astype(o_ref.dtype)


def rope_qk(proj_bd, cos_t, sin_t, tm=256):
    m = proj_bd.shape[0]
    width = 2 * C_D
    return pl.pallas_call(
        _rope_kernel,
        grid=(m // tm,),
        in_specs=[pl.BlockSpec((tm, width), lambda i: (i, 1)),
                  pl.BlockSpec((tm, LANES), lambda i: (i, 0)),
                  pl.BlockSpec((tm, LANES), lambda i: (i, 0))],
        out_specs=pl.BlockSpec((tm, width), lambda i: (i, 0)),
        out_shape=jax.ShapeDtypeStruct((m, width), BF16),
        compiler_params=_cparams("parallel"),
        name="rope_qk",
    )(proj_bd, cos_t, sin_t)


def _diff_kernel(lam_ref, q_ref, k_ref, v_ref, g_ref, o_ref, *, tq, lam_init):
    qi = pl.program_id(2)
    q = q_ref[...]
    lane = lax.broadcasted_iota(jnp.int32, q.shape, 1)
    zero = jnp.zeros_like(q)
    qs = jnp.concatenate([jnp.where(lane < DH_D, q, zero), jnp.where(lane < DH_D, zero, q)], axis=0)
    row = lax.broadcasted_iota(jnp.int32, (2 * tq, tq), 0) % tq
    col = lax.broadcasted_iota(jnp.int32, (2 * tq, tq), 1)
    visible = col <= row
    scale = DH_D ** -0.5

    def block(j, m, l, acc, diagonal):
        start = pl.multiple_of(j * tq, tq)
        kb = k_ref[pl.ds(start, tq), :]
        vb = v_ref[pl.ds(start, tq), :]
        s = _dot(qs, kb, NT) * scale
        if diagonal:
            s = jnp.where(visible, s, -jnp.inf)
        m_new = jnp.maximum(m, jnp.max(s, axis=-1, keepdims=True))
        alpha = jnp.exp(m - m_new)
        p = jnp.exp(s - m_new)
        l = alpha * l + jnp.sum(p, axis=-1, keepdims=True)
        acc = alpha * acc + _dot(p.astype(BF16), vb)
        return m_new, l, acc

    m0 = jnp.full((2 * tq, 1), -jnp.inf, F32)
    l0 = jnp.zeros((2 * tq, 1), F32)
    acc0 = jnp.zeros((2 * tq, DV_D), F32)
    m, l, acc = block(qi, m0, l0, acc0, True)

    def body(i, state):
        return block(i, state[0], state[1], state[2], False)

    m, l, acc = lax.fori_loop(0, qi, body, (m, l, acc))
    lv = lam_ref[...]
    lam = (jnp.exp(jnp.sum(lv[0:1, :] * lv[1:2, :], axis=-1, keepdims=True))
           - jnp.exp(jnp.sum(lv[2:3, :] * lv[3:4, :], axis=-1, keepdims=True)) + lam_init)
    o = acc / l
    y = o[0:tq, :] - lam * o[tq:2 * tq, :]
    ms = jnp.mean(y * y, axis=-1, keepdims=True)
    o_ref[...] = (y * lax.rsqrt(ms + LN_EPS) * g_ref[...] * (1.0 - lam_init)).astype(o_ref.dtype)


def diff_attention(lam_p, qk_rot, proj_cd, subln, batch, seq, lam_init, tq=256):
    nq = seq // tq
    return pl.pallas_call(
        functools.partial(_diff_kernel, tq=tq, lam_init=lam_init),
        grid=(batch, H_D, nq),
        in_specs=[pl.BlockSpec((4, DH_D), lambda b, h, i: (0, 0)),
                  pl.BlockSpec((None, tq, DV_D), lambda b, h, i: (b, i, h)),
                  pl.BlockSpec((None, seq, DV_D), lambda b, h, i: (b, 0, H_D + h)),
                  pl.BlockSpec((None, seq, DV_D), lambda b, h, i: (b, 0, 3 * H_C + h)),
                  pl.BlockSpec((1, DV_D), lambda b, h, i: (0, 0))],
        out_specs=pl.BlockSpec((None, tq, DV_D), lambda b, h, i: (b, i, h)),
        out_shape=jax.ShapeDtypeStruct((batch, seq, C_D), BF16),
        compiler_params=_cparams("parallel", "parallel", "arbitrary"),
        name="diff_attention",
    )(lam_p, qk_rot, qk_rot, proj_cd, subln)


def _router_kernel(h_ref, g_ref, wr_ref, n_ref, comb_ref, idx_ref):
    x = h_ref[...]
    ms = jnp.mean(x * x, axis=-1, keepdims=True)
    n = x * lax.rsqrt(ms + RMS_EPS) * g_ref[...]
    n_ref[...] = n.astype(n_ref.dtype)
    n_hi, n_lo = _split2(n)
    w_hi, w_lo = _split2(wr_ref[...])
    logits = _dot(n_hi, w_hi) + _dot(n_lo, w_hi) + _dot(n_hi, w_lo)
    lane = lax.broadcasted_iota(jnp.int32, logits.shape, 1)
    neg = -jnp.inf
    logits = jnp.where(lane < N_EXPERTS, logits, neg)
    m1 = jnp.max(logits, axis=-1, keepdims=True)
    i1 = jnp.min(jnp.where(logits == m1, lane, LANES), axis=-1, keepdims=True)
    rest = jnp.where(lane == i1, neg, logits)
    m2 = jnp.max(rest, axis=-1, keepdims=True)
    i2 = jnp.min(jnp.where(rest == m2, lane, LANES), axis=-1, keepdims=True)
    e = jnp.exp(m2 - m1)
    g1 = 1.0 / (1.0 + e)
    g2 = e / (1.0 + e)
    comb_ref[...] = jnp.where(lane == i1, g1, 0.0) + jnp.where(lane == i2, g2, 0.0)
    idx_ref[...] = jnp.where(lane == 0, i1, jnp.where(lane == 1, i2, 0))


def moe_router(h, g, w_router_pad, tm=256):
    m, d = h.shape
    spec = pl.BlockSpec((tm, LANES), lambda i: (i, 0))
    return pl.pallas_call(
        _router_kernel,
        grid=(m // tm,),
        in_specs=[pl.BlockSpec((tm, d), lambda i: (i, 0)), pl.BlockSpec((1, d), lambda i: (0, 0)),
                  pl.BlockSpec((d, LANES), lambda i: (0, 0))],
        out_specs=[pl.BlockSpec((tm, d), lambda i: (i, 0)), spec, spec],
        out_shape=[jax.ShapeDtypeStruct((m, d), BF16), jax.ShapeDtypeStruct((m, LANES), F32),
                   jax.ShapeDtypeStruct((m, LANES), jnp.int32)],
        compiler_params=_cparams("parallel"),
        name="moe_router",
    )(h, g, w_router_pad)


def _moe_up_kernel(te_ref, a_ref, wg_ref, wu_ref, o_ref):
    del te_ref
    a = a_ref[...]
    g = _dot(a, wg_ref[...])
    u = _dot(a, wu_ref[...])
    o_ref[...] = (g * jax.nn.sigmoid(g) * u).astype(o_ref.dtype)


def moe_up(tile_expert, xs, wg, wu, tm, tn=512):
    s, k = xs.shape
    n = wg.shape[2]
    w_spec = pl.BlockSpec((None, k, tn), lambda i, j, te: (te[i], 0, j))
    return pl.pallas_call(
        _moe_up_kernel,
        grid_spec=pltpu.PrefetchScalarGridSpec(
            num_scalar_prefetch=1,
            grid=(s // tm, n // tn),
            in_specs=[pl.BlockSpec((tm, k), lambda i, j, te: (i, 0)), w_spec, w_spec],
            out_specs=pl.BlockSpec((tm, tn), lambda i, j, te: (i, j)),
        ),
        out_shape=jax.ShapeDtypeStruct((s, n), BF16),
        compiler_params=_cparams("parallel", "arbitrary"),
        name="moe_up",
    )(tile_expert, xs, wg, wu)


def _moe_down_kernel(te_ref, a_ref, w_ref, gate_ref, o_ref):
    del te_ref
    o_ref[...] = _dot(a_ref[...], w_ref[...]) * gate_ref[...]


def moe_down(tile_expert, hs, wd, slot_gate, tm, tn=1024):
    s, k = hs.shape
    n = wd.shape[2]
    return pl.pallas_call(
        _moe_down_kernel,
        grid_spec=pltpu.PrefetchScalarGridSpec(
            num_scalar_prefetch=1,
            grid=(s // tm, n // tn),
            in_specs=[pl.BlockSpec((tm, k), lambda i, j, te: (i, 0)),
                      pl.BlockSpec((None, k, tn), lambda i, j, te: (te[i], 0, j)),
                      pl.BlockSpec((tm, 1), lambda i, j, te: (i, 0))],
            out_specs=pl.BlockSpec((tm, tn), lambda i, j, te: (i, j)),
        ),
        out_shape=jax.ShapeDtypeStruct((s, n), F32),
        compiler_params=_cparams("parallel", "arbitrary"),
        name="moe_down",
    )(tile_expert, hs, wd, slot_gate)


def moe_layer(h, ffn_g, w_router, wg, wu, wd, tm=512):
    m, d = h.shape
    w_router_pad = jnp.pad(w_router.astype(F32), ((0, 0), (0, LANES - N_EXPERTS)))
    n_b, comb, idx = moe_router(h, ffn_g.reshape(1, d).astype(F32), w_router_pad)
    top_idx = idx[:, 0:2]
    gates = jnp.take_along_axis(comb[:, 0:N_EXPERTS], top_idx, axis=1)

    flat_e = top_idx.reshape(-1)
    n_pairs = 2 * m
    onehot = (flat_e[:, None] == jnp.arange(N_EXPERTS, dtype=jnp.int32)[None, :]).astype(jnp.int32)
    counts = jnp.sum(onehot, axis=0)
    rank = jnp.sum((jnp.cumsum(onehot, axis=0) - onehot) * onehot, axis=1)
    padded = ((counts + tm - 1) // tm) * tm
    pad_off = jnp.cumsum(padded) - padded
    raw_off = jnp.cumsum(counts) - counts
    dest = pad_off[flat_e] + rank
    n_tiles = n_pairs // tm + N_EXPERTS
    n_slots = n_tiles * tm
    order = jnp.argsort(flat_e, stable=True).astype(jnp.int32)
    tile_start = jnp.arange(n_tiles, dtype=jnp.int32) * tm
    pad_end = pad_off + padded
    tile_expert = jnp.minimum(jnp.sum((tile_start[:, None] >= pad_end[None, :]).astype(jnp.int32), axis=1),
                              N_EXPERTS - 1).astype(jnp.int32)
    slot = jnp.arange(n_slots, dtype=jnp.int32)
    slot_e = jnp.repeat(tile_expert, tm)
    local = slot - pad_off[slot_e]
    valid = (local < counts[slot_e]) & (slot < pad_end[N_EXPERTS - 1])
    src_pair = order[jnp.clip(raw_off[slot_e] + local, 0, n_pairs - 1)]
    slot_token = jnp.where(valid, src_pair // 2, 0)
    slot_gate = jnp.where(valid, gates.reshape(-1)[src_pair], 0.0).astype(F32).reshape(n_slots, 1)

    xs = jnp.take(n_b, slot_token, axis=0)
    hs = moe_up(tile_expert, xs, wg, wu, tm)
    ys = moe_down(tile_expert, hs, wd, slot_gate, tm)
    dest2 = dest.reshape(m, 2)
    return h + jnp.take(ys, dest2[:, 0], axis=0) + jnp.take(ys, dest2[:, 1], axis=0)


def _rwkv_columns(w_in, layer):
    d = w_in.shape[0]
    o_w = C_A
    o_k = o_w + W_LORA
    o_v = o_k + C_A
    o_a = o_v + C_A
    o_g = o_a + A_LORA
    parts = [w_in[:, 0:C_A], w_in[:, o_k:o_k + C_A], w_in[:, o_v:o_v + C_A],
             w_in[:, o_w:o_w + W_LORA], w_in[:, o_a:o_a + A_LORA], w_in[:, o_g:o_g + G_LORA]]
    used = W_LORA + A_LORA + G_LORA
    if layer > 0:
        parts.append(w_in[:, N_IN0:N_IN0 + V_LORA])
        used += V_LORA
    parts.append(jnp.zeros((d, LORA_PAD - used), w_in.dtype))
    return jnp.concatenate(parts, axis=1)


def _rwkv_mu(mu, mu_v):
    o_w = C_A
    o_k = o_w + W_LORA
    o_v = o_k + C_A
    o_a = o_v + C_A
    o_g = o_a + A_LORA
    parts = [mu[0:C_A], mu[o_k:o_k + C_A], mu[o_v:o_v + C_A],
             mu[o_w:o_w + W_LORA], mu[o_a:o_a + A_LORA], mu[o_g:o_g + G_LORA]]
    used = W_LORA + A_LORA + G_LORA
    if mu_v is not None:
        parts.append(mu_v)
        used += V_LORA
    parts.append(jnp.zeros((LORA_PAD - used,), mu.dtype))
    return jnp.concatenate(parts).reshape(1, N_A_COLS).astype(F32)


def _pad_rows(w, offset):
    return jnp.pad(w, ((offset, LORA_PAD - offset - w.shape[0]), (0, 0))).astype(BF16)


def kernel(x, p, positions, attn_norm, w_in0, w_in_rest, w_out, rwkv_mu, rwkv_mu_v, rwkv_w0, rwkv_w_up, rwkv_a0, rwkv_a_up, rwkv_v0, rwkv_v_up, rwkv_g_up, rwkv_k_k, rwkv_k_a, rwkv_r_k, rwkv_lnx_g, rwkv_lnx_b, gm_ln_g, gm_ln_b, gm_w_s, gm_b_s, gm_out_g, sb_out_g, diff_lambda, diff_subln, ffn_norm, dense_w_gate, dense_w_up, dense_w_down, moe_router, moe_w_gate, moe_w_up, moe_w_down, ple_norm, ple_w_gate, ple_w_proj, final_norm):
    batch, seq, d = x.shape
    m = batch * seq
    h = x.reshape(m, d).astype(F32)

    inv_freq = 1.0 / (ROPE_THETA ** (jnp.arange(0, DH_D, 2, dtype=F32) / DH_D))
    invf = jnp.tile(inv_freq, LANES // (DH_D // 2)).reshape(1, LANES)
    cos_t, sin_t = rope_tables(positions.reshape(m, 1), invf)

    v_first = None
    for i in range(DEPTH):
        w_in = w_in0 if i == 0 else w_in_rest[i - 1]
        o_b = N_RWKV
        o_c = o_b + N_GMLP
        o_d = o_c + N_SB
        w_a = _rwkv_columns(w_in, i).astype(BF16)
        w_bd = jnp.concatenate([w_in[:, o_b:o_c], w_in[:, o_d:o_d + 2 * C_D]], axis=1).astype(BF16)
        w_cd = jnp.concatenate([w_in[:, o_c:o_d], w_in[:, o_d + 2 * C_D:o_d + 3 * C_D]], axis=1).astype(BF16)

        n = rmsnorm(h, attn_norm[i], BF16)
        proj_a = matmul(n, w_a, F32)
        proj_bd = matmul(n, w_bd, F32)
        proj_cd = matmul(n, w_cd, BF16)

        mu = _rwkv_mu(rwkv_mu[i], rwkv_mu_v[i - 1] if i > 0 else None)
        zeros_c = jnp.zeros((C_A,), F32)
        vec = jnp.stack([rwkv_w0[i], rwkv_a0[i], rwkv_v0[i - 1] if i > 0 else zeros_c, rwkv_k_k[i], rwkv_k_a[i],
                         rwkv_r_k[i].reshape(C_A), rwkv_lnx_g[i], rwkv_lnx_b[i]]).astype(F32)
        uw = _pad_rows(rwkv_w_up[i], 0)
        ua = _pad_rows(rwkv_a_up[i], W_LORA)
        ug = _pad_rows(rwkv_g_up[i], W_LORA + A_LORA)
        uv = _pad_rows(rwkv_v_up[i - 1], W_LORA + A_LORA + G_LORA) if i > 0 else jnp.zeros((LORA_PAD, C_A), BF16)
        vf_in = v_first if i > 0 else proj_a
        y_a, v_out = rwkv_mix(proj_a, vf_in, mu, vec, uw, ua, ug, uv, batch, seq, has_v_res=i > 0)
        if i == 0:
            v_first = v_out

        y_b = gmlp_mix(proj_bd, gm_ln_g[i].reshape(1, C_B), gm_ln_b[i].reshape(1, C_B), gm_w_s[i],
                       gm_b_s[i].T, gm_out_g[i].reshape(1, C_B))

        proj_cd3 = proj_cd.reshape(batch, seq, 4 * C_C)
        y_c = sb_attention(proj_cd3, sb_out_g[i].reshape(1, C_C), batch, seq).reshape(m, C_C)

        qk_rot = rope_qk(proj_bd, cos_t, sin_t).reshape(batch, seq, 2 * C_D)
        lam_init = 0.8 - 0.6 * math.exp(-0.3 * i)
        y_d = diff_attention(diff_lambda[i].astype(F32), qk_rot, proj_cd3, diff_subln[i].reshape(1, DV_D),
                             batch, seq, lam_init).reshape(m, C_D)

        h = mix_out_proj((y_a, y_b, y_c, y_d), w_out[i].astype(BF16), h)

        j = i // 2
        if i % 2 == 0:
            n = rmsnorm(h, ffn_norm[i], BF16)
            pad_c = ((0, 0), (0, D_FF_PAD - D_FF))
            hid = swiglu_up(n, jnp.pad(dense_w_gate[j], pad_c).astype(BF16), jnp.pad(dense_w_up[j], pad_c).astype(BF16))
            h = down_proj_residual(hid, jnp.pad(dense_w_down[j], ((0, D_FF_PAD - D_FF), (0, 0))).astype(BF16), h)
        else:
            h = moe_layer(h, ffn_norm[i], moe_router[j], moe_w_gate[j].astype(BF16), moe_w_up[j].astype(BF16),
                          moe_w_down[j].astype(BF16))

        n = rmsnorm(h, ple_norm[i], BF16)
        h = ple_residual(n, ple_w_gate[i].astype(BF16), p[i].reshape(m, P_DIM).astype(BF16),
                         ple_w_proj[i].astype(BF16), h)

    return rmsnorm(h, final_norm, x.dtype).reshape(batch, seq, d)
```

```python
import functools
import math

import jax
import jax.numpy as jnp
import numpy as np
from jax import lax
from jax.experimental import pallas as pl
from jax.experimental.pallas import tpu as pltpu

F32 = jnp.float32
BF16 = jnp.bfloat16

D_MODEL = 4096
DEPTH = 2
C_A = 1024
N_A = 64
W_LORA, A_LORA, V_LORA, G_LORA = 64, 64, 32, 160
C_B = 1024
H_B = 8
CHUNK = 128
C_C = 1024
H_C = 8
D_C = 128
C_D = 1024
H_D = 8
DV_D = 128
DH_D = 64
D_FF = 11008
N_EXPERTS = 8
D_FF_EXPERT = 5632
P_DIM = 256
ROPE_THETA = 10000.0
RMS_EPS = 1e-6
LN_EPS = 1e-5
GN_EPS = 64e-5
L2_EPS = 1e-12
N_RWKV = 3 * C_A + W_LORA + A_LORA + G_LORA
N_GMLP = 2 * C_B
N_SB = 3 * C_C
N_DIFF = 3 * C_D
N_IN0 = N_RWKV + N_GMLP + N_SB + N_DIFF

LANES = 128
VMEM_LIMIT = 56 * 1024 * 1024

LORA_PAD = 512
N_A_COLS = 3 * C_A + LORA_PAD
D_FF_PAD = 11264
RWKV_CHUNK = 64
SB_CUTOFF = -110.0

NT = (((1,), (1,)), ((), ()))
TN = (((0,), (0,)), ((), ()))


def _cparams(*sem):
    return pltpu.CompilerParams(dimension_semantics=sem, vmem_limit_bytes=VMEM_LIMIT)


def _dot(a, b, dims=None):
    if dims is None:
        return jnp.dot(a, b, preferred_element_type=F32)
    return lax.dot_general(a, b, dims, preferred_element_type=F32)


def _split2(x):
    hi = x.astype(BF16)
    lo = (x - hi.astype(F32)).astype(BF16)
    return hi, lo


def _split3(x):
    hi = x.astype(BF16)
    r = x - hi.astype(F32)
    mid = r.astype(BF16)
    lo = (r - mid.astype(F32)).astype(BF16)
    return hi, mid, lo


def _pick_tile(n, prefs):
    for t in prefs:
        if n % t == 0:
            return t
    raise ValueError(f"no tile for {n}")


def _rmsnorm_kernel(x_ref, g_ref, o_ref, *, eps):
    x = x_ref[...].astype(F32)
    ms = jnp.mean(x * x, axis=-1, keepdims=True)
    o_ref[...] = (x * lax.rsqrt(ms + eps) * g_ref[...]).astype(o_ref.dtype)


def rmsnorm(x, g, out_dtype, tm=512):
    m, d = x.shape
    return pl.pallas_call(
        functools.partial(_rmsnorm_kernel, eps=RMS_EPS),
        grid=(m // tm,),
        in_specs=[pl.BlockSpec((tm, d), lambda i: (i, 0)), pl.BlockSpec((1, d), lambda i: (0, 0))],
        out_specs=pl.BlockSpec((tm, d), lambda i: (i, 0)),
        out_shape=jax.ShapeDtypeStruct((m, d), out_dtype),
        compiler_params=_cparams("parallel"),
        name="rmsnorm",
    )(x, g.reshape(1, d).astype(F32))


def _mm_kernel(a_ref, b_ref, o_ref):
    o_ref[...] = _dot(a_ref[...], b_ref[...]).astype(o_ref.dtype)


def matmul(a, b, out_dtype, tm=1024):
    m, k = a.shape
    n = b.shape[1]
    tn = _pick_tile(n, (1024, 512, 256, 128))
    return pl.pallas_call(
        _mm_kernel,
        grid=(m // tm, n // tn),
        in_specs=[pl.BlockSpec((tm, k), lambda i, j: (i, 0)), pl.BlockSpec((k, tn), lambda i, j: (0, j))],
        out_specs=pl.BlockSpec((tm, tn), lambda i, j: (i, j)),
        out_shape=jax.ShapeDtypeStruct((m, n), out_dtype),
        compiler_params=_cparams("parallel", "arbitrary"),
        name="matmul",
    )(a, b)


def _mix_out_kernel(ya_ref, yb_ref, yc_ref, yd_ref, w_ref, res_ref, o_ref):
    kq = ya_ref.shape[1]
    acc = res_ref[...]
    for idx, y_ref in enumerate((ya_ref, yb_ref, yc_ref, yd_ref)):
        acc = acc + _dot(y_ref[...], w_ref[idx * kq:(idx + 1) * kq, :])
    o_ref[...] = acc


def mix_out_proj(ys, w, res, tm=1024, tn=512):
    m, kq = ys[0].shape
    k, n = w.shape
    y_spec = pl.BlockSpec((tm, kq), lambda i, j: (i, 0))
    return pl.pallas_call(
        _mix_out_kernel,
        grid=(m // tm, n // tn),
        in_specs=[y_spec, y_spec, y_spec, y_spec,
                  pl.BlockSpec((k, tn), lambda i, j: (0, j)),
                  pl.BlockSpec((tm, tn), lambda i, j: (i, j))],
        out_specs=pl.BlockSpec((tm, tn), lambda i, j: (i, j)),
        out_shape=jax.ShapeDtypeStruct((m, n), F32),
        compiler_params=_cparams("parallel", "arbitrary"),
        name="mix_out_proj",
    )(*ys, w, res)


def _swiglu_up_kernel(a_ref, wg_ref, wu_ref, o_ref):
    a = a_ref[...]
    g = _dot(a, wg_ref[...])
    u = _dot(a, wu_ref[...])
    o_ref[...] = (g * jax.nn.sigmoid(g) * u).astype(o_ref.dtype)


def swiglu_up(a, wg, wu, tm=1024, tn=512):
    m, k = a.shape
    n = wg.shape[1]
    w_spec = pl.BlockSpec((k, tn), lambda i, j: (0, j))
    return pl.pallas_call(
        _swiglu_up_kernel,
        grid=(m // tm, n // tn),
        in_specs=[pl.BlockSpec((tm, k), lambda i, j: (i, 0)), w_spec, w_spec],
        out_specs=pl.BlockSpec((tm, tn), lambda i, j: (i, j)),
        out_shape=jax.ShapeDtypeStruct((m, n), BF16),
        compiler_params=_cparams("parallel", "arbitrary"),
        name="swiglu_up",
    )(a, wg, wu)


def _down_res_kernel(a_ref, b_ref, res_ref, o_ref, acc_ref):
    kk = pl.program_id(2)

    @pl.when(kk == 0)
    def _():
        acc_ref[...] = res_ref[...]

    acc_ref[...] += _dot(a_ref[...], b_ref[...])

    @pl.when(kk == pl.num_programs(2) - 1)
    def _():
        o_ref[...] = acc_ref[...]


def down_proj_residual(a, b, res, tm=1024, tn=1024, tk=2816):
    m, k = a.shape
    n = b.shape[1]
    return pl.pallas_call(
        _down_res_kernel,
        grid=(m // tm, n // tn, k // tk),
        in_specs=[pl.BlockSpec((tm, tk), lambda i, j, kk: (i, kk)),
                  pl.BlockSpec((tk, tn), lambda i, j, kk: (kk, j)),
                  pl.BlockSpec((tm, tn), lambda i, j, kk: (i, j))],
        out_specs=pl.BlockSpec((tm, tn), lambda i, j, kk: (i, j)),
        out_shape=jax.ShapeDtypeStruct((m, n), F32),
        scratch_shapes=[pltpu.VMEM((tm, tn), F32)],
        compiler_params=_cparams("parallel", "parallel", "arbitrary"),
        name="down_proj_residual",
    )(a, b, res)


def _ple_kernel(n_ref, wg_ref, p_ref, wp_ref, res_ref, o_ref):
    gate = jax.nn.sigmoid(_dot(n_ref[...], wg_ref[...]))
    emb = _dot(p_ref[...], wp_ref[...])
    o_ref[...] = res_ref[...] + gate * emb


def ple_residual(n, wg, p, wp, res, tm=1024, tn=512):
    m, k = n.shape
    nn = wg.shape[1]
    pd = p.shape[1]
    return pl.pallas_call(
        _ple_kernel,
        grid=(m // tm, nn // tn),
        in_specs=[pl.BlockSpec((tm, k), lambda i, j: (i, 0)),
                  pl.BlockSpec((k, tn), lambda i, j: (0, j)),
                  pl.BlockSpec((tm, pd), lambda i, j: (i, 0)),
                  pl.BlockSpec((pd, tn), lambda i, j: (0, j)),
                  pl.BlockSpec((tm, tn), lambda i, j: (i, j))],
        out_specs=pl.BlockSpec((tm, tn), lambda i, j: (i, j)),
        out_shape=jax.ShapeDtypeStruct((m, nn), F32),
        compiler_params=_cparams("parallel", "arbitrary"),
        name="ple_residual",
    )(n, wg, p, wp, res)


def _head_sum(x, ones_bd):
    outs = []
    for p in range(x.shape[1] // LANES):
        hi, lo = _split2(x[:, p * LANES:(p + 1) * LANES])
        outs.append(_dot(hi, ones_bd) + _dot(lo, ones_bd))
    return jnp.concatenate(outs, axis=1)


def _rwkv_kernel(z_ref, vf_ref, mu_ref, vec_ref, uw_ref, ua_ref, ug_ref, uv_ref,
                 y_ref, vout_ref, prev_ref, h_ref, *, has_v_res):
    tc = RWKV_CHUNK
    c = pl.program_id(1)

    @pl.when(c == 0)
    def _():
        prev_ref[...] = jnp.zeros_like(prev_ref)
        h_ref[...] = jnp.zeros_like(h_ref)

    z = z_ref[...]
    row = lax.broadcasted_iota(jnp.int32, z.shape, 0)
    zp = jnp.where(row == 0, prev_ref[...], pltpu.roll(z, 1, 0))
    prev_ref[...] = z[tc - 1:tc, :]
    zs = z + (zp - z) * mu_ref[...]
    r = zs[:, 0:C_A]
    k = zs[:, C_A:2 * C_A]
    v = zs[:, 2 * C_A:3 * C_A]
    lr = zs[:, 3 * C_A:3 * C_A + LORA_PAD]

    w0 = vec_ref[0:1, :]
    a0 = vec_ref[1:2, :]
    v0 = vec_ref[2:3, :]
    k_k = vec_ref[3:4, :]
    k_a = vec_ref[4:5, :]
    r_k = vec_ref[5:6, :]
    lnx_g = vec_ref[6:7, :]
    lnx_b = vec_ref[7:8, :]

    lr_b = lr.astype(BF16)
    w_lin = _dot(jnp.tanh(lr).astype(BF16), uw_ref[...])
    a_lin = _dot(lr_b, ua_ref[...])
    g = _dot(jax.nn.sigmoid(lr).astype(BF16), ug_ref[...])
    w = -jax.nn.softplus(-(w0 + w_lin)) - 0.5
    ld = -jnp.exp(w)
    a_lr = jax.nn.sigmoid(a0 + a_lin)
    if has_v_res:
        v_lin = _dot(lr_b, uv_ref[...])
        v = v + (vf_ref[...] - v) * jax.nn.sigmoid(v0 + v_lin)
    vout_ref[...] = v

    li = lax.broadcasted_iota(jnp.int32, (LANES, LANES), 0)
    lj = lax.broadcasted_iota(jnp.int32, (LANES, LANES), 1)
    same_head = (li // N_A) == (lj // N_A)
    ones_bd = jnp.where(same_head, 1.0, 0.0).astype(BF16)

    kk = k * k_k
    kk = kk / jnp.maximum(jnp.sqrt(_head_sum(kk * kk, ones_bd)), L2_EPS)
    k = k * (1.0 + (a_lr - 1.0) * k_a)
    a_s = -kk
    b_s = kk * a_lr

    ti = lax.broadcasted_iota(jnp.int32, (tc, tc), 0)
    tj = lax.broadcasted_iota(jnp.int32, (tc, tc), 1)
    tri = jnp.where(ti >= tj, 1.0, 0.0).astype(BF16)
    ld_h, ld_m, ld_l = _split3(ld)
    cum = _dot(tri, ld_h) + _dot(tri, ld_m) + _dot(tri, ld_l)
    cum_end = cum[tc - 1:tc, :]
    p_inc = jnp.exp(cum)
    p_exc = jnp.exp(cum - ld)
    p_inv = jnp.exp(-cum)
    p_end = jnp.exp(cum_end - cum)
    decay_end = jnp.exp(cum_end)

    a_t = a_s * p_exc
    b_t = b_s * p_inv
    k_t = k * p_inv
    r_t = r * p_inc
    b_h = b_s * p_end
    k_h = k * p_end

    first_head = lax.broadcasted_iota(jnp.int32, (tc, LANES), 1) < N_A
    strict = same_head & ((li % N_A) > (lj % N_A))
    incl = same_head & ((li % N_A) >= (lj % N_A))
    eye = li == lj

    def stack(x):
        return jnp.concatenate([jnp.where(first_head, x, 0.0), jnp.where(first_head, 0.0, x)], axis=0)

    def dup(x):
        return jnp.concatenate([x, x], axis=0)

    pairs = range(C_A // LANES)
    sls = [slice(p * LANES, (p + 1) * LANES) for p in pairs]
    n2 = 2 * tc
    a_st = [stack(a_t[:, sl]) for sl in sls]
    r_st = [stack(r_t[:, sl]) for sl in sls]
    v_st = [stack(v[:, sl]).astype(BF16) for sl in sls]
    bh_st = [stack(b_h[:, sl]).astype(BF16) for sl in sls]
    kh_st = [stack(k_h[:, sl]).astype(BF16) for sl in sls]
    lhs = [jnp.concatenate([a_st[p], r_st[p]], axis=0).astype(BF16) for p in pairs]
    rhs = [jnp.concatenate([dup(b_t[:, sl]), dup(k_t[:, sl])], axis=0).astype(BF16) for sl in sls]
    x = [_dot(lhs[p], rhs[p], NT) for p in pairs]
    lk = [jnp.where(strict, x[p][0:n2, 0:n2], 0.0) for p in pairs]
    a_ak = [jnp.where(strict, x[p][0:n2, n2:2 * n2], 0.0).astype(BF16) for p in pairs]
    a_rb = [jnp.where(incl, x[p][n2:2 * n2, 0:n2], 0.0).astype(BF16) for p in pairs]
    a_rk = [jnp.where(incl, x[p][n2:2 * n2, n2:2 * n2], 0.0).astype(BF16) for p in pairs]
    zz = [jnp.concatenate([a_st[p], _dot(a_ak[p], v_st[p])], axis=1) for p in pairs]
    n_iter = int(math.log2(tc))
    for it in range(n_iter):
        lk_b = [lk[p].astype(BF16) for p in pairs]
        zz = [zz[p] + _dot(lk_b[p], zz[p].astype(BF16)) for p in pairs]
        if it < n_iter - 1:
            lk = [_dot(lk_b[p], lk_b[p]) for p in pairs]
    zz_b = [zz[p].astype(BF16) for p in pairs]
    qy = [_dot(a_rb[p], zz_b[p]) for p in pairs]
    y3 = [qy[p][:, LANES:] + _dot(a_rk[p], v_st[p]) for p in pairs]
    mg = [_dot(bh_st[p], zz_b[p], TN) for p in pairs]
    g_mat = [mg[p][:, LANES:] + _dot(kh_st[p], v_st[p], TN) for p in pairs]
    qm = [jnp.concatenate([r_st[p] + qy[p][:, 0:LANES],
                           jnp.where(eye, decay_end[:, sls[p]], 0.0) + mg[p][:, 0:LANES]], axis=0).astype(BF16)
          for p in pairs]
    h_split = [_split2(h_ref[p]) for p in pairs]
    out = [_dot(qm[p], h_split[p][0]) + _dot(qm[p], h_split[p][1]) for p in pairs]
    y_parts = []
    for p in pairs:
        y_st = out[p][0:n2, :] + y3[p]
        y_parts.append(y_st[0:tc, :] + y_st[tc:n2, :])
        h_ref[p] = out[p][n2:, :] + g_mat[p]
    y = jnp.concatenate(y_parts, axis=1)

    inv_n = 1.0 / N_A
    mean = _head_sum(y, ones_bd) * inv_n
    d = y - mean
    var = _head_sum(d * d, ones_bd) * inv_n
    yn = d * lax.rsqrt(var + GN_EPS) * lnx_g + lnx_b
    bonus = _head_sum(r * k * r_k, ones_bd) * v
    y_ref[...] = ((yn + bonus) * g).astype(y_ref.dtype)


def rwkv_mix(proj_a, v_first, mu, vec, uw, ua, ug, uv, batch, seq, has_v_res):
    tc = RWKV_CHUNK
    nc = seq // tc
    row_spec = lambda w: pl.BlockSpec((tc, w), lambda b, c: (b * nc + c, 0))
    full = lambda s: pl.BlockSpec(s, lambda b, c: tuple(0 for _ in s))
    m = batch * seq
    return pl.pallas_call(
        functools.partial(_rwkv_kernel, has_v_res=has_v_res),
        grid=(batch, nc),
        in_specs=[row_spec(N_A_COLS), row_spec(C_A), full((1, N_A_COLS)), full((8, C_A)),
                  full((LORA_PAD, C_A)), full((LORA_PAD, C_A)), full((LORA_PAD, C_A)), full((LORA_PAD, C_A))],
        out_specs=[row_spec(C_A), row_spec(C_A)],
        out_shape=[jax.ShapeDtypeStruct((m, C_A), BF16), jax.ShapeDtypeStruct((m, C_A), F32)],
        scratch_shapes=[pltpu.VMEM((1, N_A_COLS), F32), pltpu.VMEM((C_A // LANES, LANES, LANES), F32)],
        compiler_params=_cparams("parallel", "arbitrary"),
        name="rwkv_mix",
    )(proj_a, v_first, mu, vec, uw, ua, ug, uv)


def _gelu(x):
    return 0.5 * x * (1.0 + lax.erf(x * math.sqrt(0.5)))


def _gmlp_kernel(z_ref, lng_ref, lnb_ref, w_ref, bs_ref, og_ref, o_ref):
    u = _gelu(z_ref[:, 0:C_B])
    v = _gelu(z_ref[:, C_B:2 * C_B])
    mu = jnp.mean(v, axis=-1, keepdims=True)
    d = v - mu
    var = jnp.mean(d * d, axis=-1, keepdims=True)
    vn = d * lax.rsqrt(var + LN_EPS) * lng_ref[...] + lnb_ref[...]
    ti = lax.broadcasted_iota(jnp.int32, (CHUNK, CHUNK), 0)
    tj = lax.broadcasted_iota(jnp.int32, (CHUNK, CHUNK), 1)
    causal = ti >= tj
    dh = C_B // H_B
    for h in range(H_B):
        sl = slice(h * dh, (h + 1) * dh)
        w = jnp.where(causal, w_ref[h], 0.0).astype(BF16)
        s = _dot(w, vn[:, sl].astype(BF16)) + bs_ref[:, h:h + 1]
        y = u[:, sl] * s
        ms = jnp.mean(y * y, axis=-1, keepdims=True)
        o_ref[:, sl] = (y * lax.rsqrt(ms + RMS_EPS) * og_ref[:, sl]).astype(o_ref.dtype)


def gmlp_mix(proj_bd, ln_g, ln_b, w_s, b_s_t, out_g):
    m = proj_bd.shape[0]
    full = lambda s: pl.BlockSpec(s, lambda i: tuple(0 for _ in s))
    return pl.pallas_call(
        _gmlp_kernel,
        grid=(m // CHUNK,),
        in_specs=[pl.BlockSpec((CHUNK, 2 * C_B), lambda i: (i, 0)),
                  full((1, C_B)), full((1, C_B)), full((H_B, CHUNK, CHUNK)), full((CHUNK, H_B)), full((1, C_B))],
        out_specs=pl.BlockSpec((CHUNK, C_B), lambda i: (i, 0)),
        out_shape=jax.ShapeDtypeStruct((m, C_B), BF16),
        compiler_params=_cparams("parallel"),
        name="gmlp_mix",
    )(proj_bd, ln_g, ln_b, w_s, b_s_t, out_g)


def _sb_kernel(q_ref, k_ref, v_ref, g_ref, o_ref, *, tq, scale):
    qi = pl.program_id(2)
    q = q_ref[...]
    row = lax.broadcasted_iota(jnp.int32, (tq, tq), 0)
    col = lax.broadcasted_iota(jnp.int32, (tq, tq), 1)
    below = col < row
    upper = jnp.where(row > col, 1.0, 0.0).astype(BF16)

    def block(j, carry, acc, diagonal):
        start = pl.multiple_of(j * tq, tq)
        kb = k_ref[pl.ds(start, tq), :]
        vb = v_ref[pl.ds(start, tq), :]
        z = _dot(q, kb, NT) * scale
        log_beta = jnp.minimum(z, 0.0) - jnp.log(1.0 + jnp.exp(-jnp.abs(z)))
        log_1mb = log_beta - z
        if diagonal:
            log_1mb = jnp.where(below, log_1mb, 0.0)
        hi, lo = _split2(log_1mb)
        after = _dot(hi, upper) + _dot(lo, upper)
        att = jnp.exp(log_beta + after + carry)
        if diagonal:
            att = jnp.where(below, att, 0.0)
        acc = acc + _dot(att.astype(BF16), vb)
        carry = carry + after[:, 0:1] + log_1mb[:, 0:1]
        return carry, acc

    carry0 = jnp.zeros((tq, 1), F32)
    acc0 = jnp.zeros((tq, D_C), F32)
    carry, acc = block(qi, carry0, acc0, True)

    def more(state):
        return jnp.logical_and(state[0] >= 0, jnp.max(state[1]) > SB_CUTOFF)

    def body(state):
        carry_j, acc_j = block(state[0], state[1], state[2], False)
        return state[0] - 1, carry_j, acc_j

    _, carry, acc = lax.while_loop(more, body, (qi - 1, carry, acc))
    ms = jnp.mean(acc * acc, axis=-1, keepdims=True)
    o_ref[...] = (acc * lax.rsqrt(ms + RMS_EPS) * g_ref[...]).astype(o_ref.dtype)


def sb_attention(proj_cd, out_g, batch, seq, tq=256):
    nq = seq // tq
    return pl.pallas_call(
        functools.partial(_sb_kernel, tq=tq, scale=D_C ** -0.5),
        grid=(batch, H_C, nq),
        in_specs=[pl.BlockSpec((None, tq, D_C), lambda b, h, i: (b, i, h)),
                  pl.BlockSpec((None, seq, D_C), lambda b, h, i: (b, 0, H_C + h)),
                  pl.BlockSpec((None, seq, D_C), lambda b, h, i: (b, 0, 2 * H_C + h)),
                  pl.BlockSpec((1, D_C), lambda b, h, i: (0, h))],
        out_specs=pl.BlockSpec((None, tq, D_C), lambda b, h, i: (b, i, h)),
        out_shape=jax.ShapeDtypeStruct((batch, seq, C_C), BF16),
        compiler_params=_cparams("parallel", "parallel", "arbitrary"),
        name="sb_attention",
    )(proj_cd, proj_cd, proj_cd, out_g)


def _rope_table_kernel(pos_ref, invf_ref, cos_ref, sin_ref):
    ang = pos_ref[...].astype(F32) * invf_ref[...]
    cos_ref[...] = jnp.cos(ang)
    sin_ref[...] = jnp.sin(ang)


def rope_tables(pos_col, invf, tm=512):
    m = pos_col.shape[0]
    spec = pl.BlockSpec((tm, LANES), lambda i: (i, 0))
    return pl.pallas_call(
        _rope_table_kernel,
        grid=(m // tm,),
        in_specs=[pl.BlockSpec((tm, 1), lambda i: (i, 0)), pl.BlockSpec((1, LANES), lambda i: (0, 0))],
        out_specs=[spec, spec],
        out_shape=[jax.ShapeDtypeStruct((m, LANES), F32)] * 2,
        compiler_params=_cparams("parallel"),
        name="rope_tables",
    )(pos_col, invf)


def _rope_kernel(x_ref, cos_ref, sin_ref, o_ref):
    x = x_ref[...]
    width = x.shape[1]
    reps = width // LANES
    c = jnp.concatenate([cos_ref[...]] * reps, axis=1)
    s = jnp.concatenate([sin_ref[...]] * reps, axis=1)
    lane = lax.broadcasted_iota(jnp.int32, x.shape, 1)
    first_half = (lane % DH_D) < (DH_D // 2)
    rot = jnp.where(first_half, -pltpu.roll(x, width - DH_D // 2, 1), pltpu.roll(x, DH_D // 2, 1))
    o_ref[...] = (x * c + rot * s).astype(o_ref.dtype)


def rope_qk(proj_bd, cos_t, sin_t, tm=256):
    m = proj_bd.shape[0]
    width = 2 * C_D
    return pl.pallas_call(
        _rope_kernel,
        grid=(m // tm,),
        in_specs=[pl.BlockSpec((tm, width), lambda i: (i, 1)),
                  pl.BlockSpec((tm, LANES), lambda i: (i, 0)),
                  pl.BlockSpec((tm, LANES), lambda i: (i, 0))],
        out_specs=pl.BlockSpec((tm, width), lambda i: (i, 0)),
        out_shape=jax.ShapeDtypeStruct((m, width), BF16),
        compiler_params=_cparams("parallel"),
        name="rope_qk",
    )(proj_bd, cos_t, sin_t)


def _diff_kernel(lam_ref, q_ref, k_ref, v_ref, g_ref, o_ref, *, tq, lam_init):
    qi = pl.program_id(2)
    q = (q_ref[...].astype(F32) * (DH_D ** -0.5)).astype(BF16)
    lane = lax.broadcasted_iota(jnp.int32, q.shape, 1)
    zero = jnp.zeros_like(q)
    qs = jnp.concatenate([jnp.where(lane < DH_D, q, zero), jnp.where(lane < DH_D, zero, q)], axis=0)
    row = lax.broadcasted_iota(jnp.int32, (2 * tq, tq), 0) % tq
    col = lax.broadcasted_iota(jnp.int32, (2 * tq, tq), 1)
    visible = col <= row

    def block(j, m, l, acc, diagonal):
        start = pl.multiple_of(j * tq, tq)
        kb = k_ref[pl.ds(start, tq), :]
        vb = v_ref[pl.ds(start, tq), :]
        s = _dot(qs, kb, NT)
        if diagonal:
            s = jnp.where(visible, s, -jnp.inf)
        m_new = jnp.maximum(m, jnp.max(s, axis=-1, keepdims=True))
        alpha = jnp.exp(m - m_new)
        p = jnp.exp(s - m_new)
        l = alpha * l + jnp.sum(p, axis=-1, keepdims=True)
        acc = alpha * acc + _dot(p.astype(BF16), vb)
        return m_new, l, acc

    m0 = jnp.full((2 * tq, 1), -jnp.inf, F32)
    l0 = jnp.zeros((2 * tq, 1), F32)
    acc0 = jnp.zeros((2 * tq, DV_D), F32)
    m, l, acc = block(qi, m0, l0, acc0, True)

    def body(i, state):
        return block(i, state[0], state[1], state[2], False)

    m, l, acc = lax.fori_loop(0, qi, body, (m, l, acc))
    lv = lam_ref[...]
    lam = (jnp.exp(jnp.sum(lv[0:1, :] * lv[1:2, :], axis=-1, keepdims=True))
           - jnp.exp(jnp.sum(lv[2:3, :] * lv[3:4, :], axis=-1, keepdims=True)) + lam_init)
    o = acc / l
    y = o[0:tq, :] - lam * o[tq:2 * tq, :]
    ms = jnp.mean(y * y, axis=-1, keepdims=True)
    o_ref[...] = (y * lax.rsqrt(ms + LN_EPS) * g_ref[...] * (1.0 - lam_init)).astype(o_ref.dtype)


def diff_attention(lam_p, qk_rot, proj_cd, subln, batch, seq, lam_init, tq=512):
    nq = seq // tq
    return pl.pallas_call(
        functools.partial(_diff_kernel, tq=tq, lam_init=lam_init),
        grid=(batch, H_D, nq),
        in_specs=[pl.BlockSpec((4, DH_D), lambda b, h, i: (0, 0)),
                  pl.BlockSpec((None, tq, DV_D), lambda b, h, i: (b, i, h)),
                  pl.BlockSpec((None, seq, DV_D), lambda b, h, i: (b, 0, H_D + h)),
                  pl.BlockSpec((None, seq, DV_D), lambda b, h, i: (b, 0, 3 * H_C + h)),
                  pl.BlockSpec((1, DV_D), lambda b, h, i: (0, 0))],
        out_specs=pl.BlockSpec((None, tq, DV_D), lambda b, h, i: (b, i, h)),
        out_shape=jax.ShapeDtypeStruct((batch, seq, C_D), BF16),
        compiler_params=_cparams("parallel", "parallel", "arbitrary"),
        name="diff_attention",
    )(lam_p, qk_rot, qk_rot, proj_cd, subln)


def _router_kernel(h_ref, g_ref, wr_ref, n_ref, comb_ref, idx_ref):
    x = h_ref[...]
    ms = jnp.mean(x * x, axis=-1, keepdims=True)
    n = x * lax.rsqrt(ms + RMS_EPS) * g_ref[...]
    n_ref[...] = n.astype(n_ref.dtype)
    n_hi, n_lo = _split2(n)
    w_hi, w_lo = _split2(wr_ref[...])
    logits = _dot(n_hi, w_hi) + _dot(n_lo, w_hi) + _dot(n_hi, w_lo)
    lane = lax.broadcasted_iota(jnp.int32, logits.shape, 1)
    neg = -jnp.inf
    logits = jnp.where(lane < N_EXPERTS, logits, neg)
    m1 = jnp.max(logits, axis=-1, keepdims=True)
    i1 = jnp.min(jnp.where(logits == m1, lane, LANES), axis=-1, keepdims=True)
    rest = jnp.where(lane == i1, neg, logits)
    m2 = jnp.max(rest, axis=-1, keepdims=True)
    i2 = jnp.min(jnp.where(rest == m2, lane, LANES), axis=-1, keepdims=True)
    e = jnp.exp(m2 - m1)
    g1 = 1.0 / (1.0 + e)
    g2 = e / (1.0 + e)
    comb_ref[...] = jnp.where(lane == i1, g1, 0.0) + jnp.where(lane == i2, g2, 0.0)
    idx_ref[...] = jnp.where(lane == 0, i1, jnp.where(lane == 1, i2, 0))


def moe_router(h, g, w_router_pad, tm=256):
    m, d = h.shape
    spec = pl.BlockSpec((tm, LANES), lambda i: (i, 0))
    return pl.pallas_call(
        _router_kernel,
        grid=(m // tm,),
        in_specs=[pl.BlockSpec((tm, d), lambda i: (i, 0)), pl.BlockSpec((1, d), lambda i: (0, 0)),
                  pl.BlockSpec((d, LANES), lambda i: (0, 0))],
        out_specs=[pl.BlockSpec((tm, d), lambda i: (i, 0)), spec, spec],
        out_shape=[jax.ShapeDtypeStruct((m, d), BF16), jax.ShapeDtypeStruct((m, LANES), F32),
                   jax.ShapeDtypeStruct((m, LANES), jnp.int32)],
        compiler_params=_cparams("parallel"),
        name="moe_router",
    )(h, g, w_router_pad)


def _moe_up_kernel(te_ref, a_ref, wg_ref, wu_ref, o_ref):
    in_use = pl.program_id(0) < te_ref[pl.num_programs(0)]

    @pl.when(in_use)
    def _():
        a = a_ref[...]
        g = _dot(a, wg_ref[...])
        u = _dot(a, wu_ref[...])
        o_ref[...] = (g * jax.nn.sigmoid(g) * u).astype(o_ref.dtype)

    @pl.when(jnp.logical_not(in_use))
    def _():
        o_ref[...] = jnp.zeros_like(o_ref)


def _moe_weight_spec(k, tn, n_tiles):
    return pl.BlockSpec((None, k, tn), lambda i, j, te: (te[i], 0, jnp.where(i < te[n_tiles], j, 0)))


def moe_up(tile_table, xs, wg, wu, tm, tn=512):
    s, k = xs.shape
    n = wg.shape[2]
    w_spec = _moe_weight_spec(k, tn, s // tm)
    return pl.pallas_call(
        _moe_up_kernel,
        grid_spec=pltpu.PrefetchScalarGridSpec(
            num_scalar_prefetch=1,
            grid=(s // tm, n // tn),
            in_specs=[pl.BlockSpec((tm, k), lambda i, j, te: (i, 0)), w_spec, w_spec],
            out_specs=pl.BlockSpec((tm, tn), lambda i, j, te: (i, j)),
        ),
        out_shape=jax.ShapeDtypeStruct((s, n), BF16),
        compiler_params=_cparams("parallel", "arbitrary"),
        name="moe_up",
    )(tile_table, xs, wg, wu)


def _moe_down_kernel(te_ref, a_ref, w_ref, gate_ref, o_ref):
    in_use = pl.program_id(0) < te_ref[pl.num_programs(0)]

    @pl.when(in_use)
    def _():
        o_ref[...] = _dot(a_ref[...], w_ref[...]) * gate_ref[...]

    @pl.when(jnp.logical_not(in_use))
    def _():
        o_ref[...] = jnp.zeros_like(o_ref)


def moe_down(tile_table, hs, wd, slot_gate, tm, tn=1024):
    s, k = hs.shape
    n = wd.shape[2]
    return pl.pallas_call(
        _moe_down_kernel,
        grid_spec=pltpu.PrefetchScalarGridSpec(
            num_scalar_prefetch=1,
            grid=(s // tm, n // tn),
            in_specs=[pl.BlockSpec((tm, k), lambda i, j, te: (i, 0)),
                      _moe_weight_spec(k, tn, s // tm),
                      pl.BlockSpec((tm, 1), lambda i, j, te: (i, 0))],
            out_specs=pl.BlockSpec((tm, tn), lambda i, j, te: (i, j)),
        ),
        out_shape=jax.ShapeDtypeStruct((s, n), F32),
        compiler_params=_cparams("parallel", "arbitrary"),
        name="moe_down",
    )(tile_table, hs, wd, slot_gate)


def moe_layer(h, ffn_g, w_router, wg, wu, wd, tm=512):
    m, d = h.shape
    w_router_pad = jnp.pad(w_router.astype(F32), ((0, 0), (0, LANES - N_EXPERTS)))
    n_b, comb, idx = moe_router(h, ffn_g.reshape(1, d).astype(F32), w_router_pad)
    top_idx = idx[:, 0:2]
    gates = jnp.take_along_axis(comb[:, 0:N_EXPERTS], top_idx, axis=1)

    flat_e = top_idx.reshape(-1)
    n_pairs = 2 * m
    onehot = (flat_e[:, None] == jnp.arange(N_EXPERTS, dtype=jnp.int32)[None, :]).astype(jnp.int32)
    counts = jnp.sum(onehot, axis=0)
    rank = jnp.sum((jnp.cumsum(onehot, axis=0) - onehot) * onehot, axis=1)
    padded = ((counts + tm - 1) // tm) * tm
    pad_off = jnp.cumsum(padded) - padded
    raw_off = jnp.cumsum(counts) - counts
    dest = pad_off[flat_e] + rank
    n_tiles = n_pairs // tm + N_EXPERTS
    n_slots = n_tiles * tm
    order = jnp.argsort(flat_e, stable=True).astype(jnp.int32)
    tile_start = jnp.arange(n_tiles, dtype=jnp.int32) * tm
    pad_end = pad_off + padded
    tile_expert = jnp.minimum(jnp.sum((tile_start[:, None] >= pad_end[None, :]).astype(jnp.int32), axis=1),
                              N_EXPERTS - 1).astype(jnp.int32)
    slot = jnp.arange(n_slots, dtype=jnp.int32)
    slot_e = jnp.repeat(tile_expert, tm)
    local = slot - pad_off[slot_e]
    valid = (local < counts[slot_e]) & (slot < pad_end[N_EXPERTS - 1])
    src_pair = order[jnp.clip(raw_off[slot_e] + local, 0, n_pairs - 1)]
    slot_token = jnp.where(valid, src_pair // 2, 0)
    slot_gate = jnp.where(valid, gates.reshape(-1)[src_pair], 0.0).astype(F32).reshape(n_slots, 1)

    tiles_in_use = (pad_end[N_EXPERTS - 1] // tm).astype(jnp.int32).reshape(1)
    tile_table = jnp.concatenate([tile_expert, tiles_in_use])

    xs = jnp.take(n_b, slot_token, axis=0)
    hs = moe_up(tile_table, xs, wg, wu, tm)
    ys = moe_down(tile_table, hs, wd, slot_gate, tm)
    dest2 = dest.reshape(m, 2)
    return h + jnp.take(ys, dest2[:, 0], axis=0) + jnp.take(ys, dest2[:, 1], axis=0)


def _rwkv_columns(w_in, layer):
    d = w_in.shape[0]
    o_w = C_A
    o_k = o_w + W_LORA
    o_v = o_k + C_A
    o_a = o_v + C_A
    o_g = o_a + A_LORA
    parts = [w_in[:, 0:C_A], w_in[:, o_k:o_k + C_A], w_in[:, o_v:o_v + C_A],
             w_in[:, o_w:o_w + W_LORA], w_in[:, o_a:o_a + A_LORA], w_in[:, o_g:o_g + G_LORA]]
    used = W_LORA + A_LORA + G_LORA
    if layer > 0:
        parts.append(w_in[:, N_IN0:N_IN0 + V_LORA])
        used += V_LORA
    parts.append(jnp.zeros((d, LORA_PAD - used), w_in.dtype))
    return jnp.concatenate(parts, axis=1)


def _rwkv_mu(mu, mu_v):
    o_w = C_A
    o_k = o_w + W_LORA
    o_v = o_k + C_A
    o_a = o_v + C_A
    o_g = o_a + A_LORA
    parts = [mu[0:C_A], mu[o_k:o_k + C_A], mu[o_v:o_v + C_A],
             mu[o_w:o_w + W_LORA], mu[o_a:o_a + A_LORA], mu[o_g:o_g + G_LORA]]
    used = W_LORA + A_LORA + G_LORA
    if mu_v is not None:
        parts.append(mu_v)
        used += V_LORA
    parts.append(jnp.zeros((LORA_PAD - used,), mu.dtype))
    return jnp.concatenate(parts).reshape(1, N_A_COLS).astype(F32)


def _pad_rows(w, offset):
    return jnp.pad(w, ((offset, LORA_PAD - offset - w.shape[0]), (0, 0))).astype(BF16)


def kernel(x, p, positions, attn_norm, w_in0, w_in_rest, w_out, rwkv_mu, rwkv_mu_v, rwkv_w0, rwkv_w_up, rwkv_a0, rwkv_a_up, rwkv_v0, rwkv_v_up, rwkv_g_up, rwkv_k_k, rwkv_k_a, rwkv_r_k, rwkv_lnx_g, rwkv_lnx_b, gm_ln_g, gm_ln_b, gm_w_s, gm_b_s, gm_out_g, sb_out_g, diff_lambda, diff_subln, ffn_norm, dense_w_gate, dense_w_up, dense_w_down, moe_router, moe_w_gate, moe_w_up, moe_w_down, ple_norm, ple_w_gate, ple_w_proj, final_norm):
    batch, seq, d = x.shape
    m = batch * seq
    h = x.reshape(m, d).astype(F32)

    inv_freq = 1.0 / (ROPE_THETA ** (jnp.arange(0, DH_D, 2, dtype=F32) / DH_D))
    invf = jnp.tile(inv_freq, LANES // (DH_D // 2)).reshape(1, LANES)
    cos_t, sin_t = rope_tables(positions.reshape(m, 1), invf)

    v_first = None
    for i in range(DEPTH):
        w_in = w_in0 if i == 0 else w_in_rest[i - 1]
        o_b = N_RWKV
        o_c = o_b + N_GMLP
        o_d = o_c + N_SB
        w_a = _rwkv_columns(w_in, i).astype(BF16)
        w_bd = jnp.concatenate([w_in[:, o_b:o_c], w_in[:, o_d:o_d + 2 * C_D]], axis=1).astype(BF16)
        w_cd = jnp.concatenate([w_in[:, o_c:o_d], w_in[:, o_d + 2 * C_D:o_d + 3 * C_D]], axis=1).astype(BF16)

        n = rmsnorm(h, attn_norm[i], BF16)
        proj_a = matmul(n, w_a, F32)
        proj_bd = matmul(n, w_bd, F32)
        proj_cd = matmul(n, w_cd, BF16)

        mu = _rwkv_mu(rwkv_mu[i], rwkv_mu_v[i - 1] if i > 0 else None)
        zeros_c = jnp.zeros((C_A,), F32)
        vec = jnp.stack([rwkv_w0[i], rwkv_a0[i], rwkv_v0[i - 1] if i > 0 else zeros_c, rwkv_k_k[i], rwkv_k_a[i],
                         rwkv_r_k[i].reshape(C_A), rwkv_lnx_g[i], rwkv_lnx_b[i]]).astype(F32)
        uw = _pad_rows(rwkv_w_up[i], 0)
        ua = _pad_rows(rwkv_a_up[i], W_LORA)
        ug = _pad_rows(rwkv_g_up[i], W_LORA + A_LORA)
        uv = _pad_rows(rwkv_v_up[i - 1], W_LORA + A_LORA + G_LORA) if i > 0 else jnp.zeros((LORA_PAD, C_A), BF16)
        vf_in = v_first if i > 0 else proj_a
        y_a, v_out = rwkv_mix(proj_a, vf_in, mu, vec, uw, ua, ug, uv, batch, seq, has_v_res=i > 0)
        if i == 0:
            v_first = v_out

        y_b = gmlp_mix(proj_bd, gm_ln_g[i].reshape(1, C_B), gm_ln_b[i].reshape(1, C_B), gm_w_s[i],
                       gm_b_s[i].T, gm_out_g[i].reshape(1, C_B))

        proj_cd3 = proj_cd.reshape(batch, seq, 4 * C_C)
        y_c = sb_attention(proj_cd3, sb_out_g[i].reshape(1, C_C), batch, seq).reshape(m, C_C)

        qk_rot = rope_qk(proj_bd, cos_t, sin_t).reshape(batch, seq, 2 * C_D)
        lam_init = 0.8 - 0.6 * math.exp(-0.3 * i)
        y_d = diff_attention(diff_lambda[i].astype(F32), qk_rot, proj_cd3, diff_subln[i].reshape(1, DV_D),
                             batch, seq, lam_init).reshape(m, C_D)

        h = mix_out_proj((y_a, y_b, y_c, y_d), w_out[i].astype(BF16), h)

        j = i // 2
        if i % 2 == 0:
            n = rmsnorm(h, ffn_norm[i], BF16)
            pad_c = ((0, 0), (0, D_FF_PAD - D_FF))
            hid = swiglu_up(n, jnp.pad(dense_w_gate[j], pad_c).astype(BF16), jnp.pad(dense_w_up[j], pad_c).astype(BF16))
            h = down_proj_residual(hid, jnp.pad(dense_w_down[j], ((0, D_FF_PAD - D_FF), (0, 0))).astype(BF16), h)
        else:
            h = moe_layer(h, ffn_norm[i], moe_router[j], moe_w_gate[j].astype(BF16), moe_w_up[j].astype(BF16),
                          moe_w_down[j].astype(BF16))

        n = rmsnorm(h, ple_norm[i], BF16)
        h = ple_residual(n, ple_w_gate[i].astype(BF16), p[i].reshape(m, P_DIM).astype(BF16),
                         ple_w_proj[i].astype(BF16), h)

    return rmsnorm(h, final_norm, x.dtype).reshape(batch, seq, d)
```

```python
import functools
import math

import jax
import jax.numpy as jnp
import numpy as np
from jax import lax
from jax.experimental import pallas as pl
from jax.experimental.pallas import tpu as pltpu

F32 = jnp.float32
BF16 = jnp.bfloat16

D_MODEL = 4096
DEPTH = 2
C_A = 1024
N_A = 64
W_LORA, A_LORA, V_LORA, G_LORA = 64, 64, 32, 160
C_B = 1024
H_B = 8
CHUNK = 128
C_C = 1024
H_C = 8
D_C = 128
C_D = 1024
H_D = 8
DV_D = 128
DH_D = 64
D_FF = 11008
N_EXPERTS = 8
D_FF_EXPERT = 5632
P_DIM = 256
ROPE_THETA = 10000.0
RMS_EPS = 1e-6
LN_EPS = 1e-5
GN_EPS = 64e-5
L2_EPS = 1e-12
N_RWKV = 3 * C_A + W_LORA + A_LORA + G_LORA
N_GMLP = 2 * C_B
N_SB = 3 * C_C
N_DIFF = 3 * C_D
N_IN0 = N_RWKV + N_GMLP + N_SB + N_DIFF

LANES = 128
VMEM_LIMIT = 56 * 1024 * 1024

LORA_PAD = 512
N_A_COLS = 3 * C_A + LORA_PAD
DENSE_TM, DENSE_TN = 1024, 256
DENSE_TK = 5504
MOE_TM, MOE_TN = 512, 512
RWKV_CHUNK = 64
SB_CUTOFF = -110.0

NT = (((1,), (1,)), ((), ()))
TN = (((0,), (0,)), ((), ()))


def _cparams(*sem):
    return pltpu.CompilerParams(dimension_semantics=sem, vmem_limit_bytes=VMEM_LIMIT)


def _dot(a, b, dims=None):
    if dims is None:
        return jnp.dot(a, b, preferred_element_type=F32)
    return lax.dot_general(a, b, dims, preferred_element_type=F32)


def _split2(x):
    hi = x.astype(BF16)
    lo = (x - hi.astype(F32)).astype(BF16)
    return hi, lo


def _split3(x):
    hi = x.astype(BF16)
    r = x - hi.astype(F32)
    mid = r.astype(BF16)
    lo = (r - mid.astype(F32)).astype(BF16)
    return hi, mid, lo


def _pick_tile(n, prefs):
    for t in prefs:
        if n % t == 0:
            return t
    raise ValueError(f"no tile for {n}")


def _rmsnorm_kernel(x_ref, g_ref, o_ref, *, eps):
    x = x_ref[...].astype(F32)
    ms = jnp.mean(x * x, axis=-1, keepdims=True)
    o_ref[...] = (x * lax.rsqrt(ms + eps) * g_ref[...]).astype(o_ref.dtype)


def rmsnorm(x, g, out_dtype, tm=512):
    m, d = x.shape
    return pl.pallas_call(
        functools.partial(_rmsnorm_kernel, eps=RMS_EPS),
        grid=(m // tm,),
        in_specs=[pl.BlockSpec((tm, d), lambda i: (i, 0)), pl.BlockSpec((1, d), lambda i: (0, 0))],
        out_specs=pl.BlockSpec((tm, d), lambda i: (i, 0)),
        out_shape=jax.ShapeDtypeStruct((m, d), out_dtype),
        compiler_params=_cparams("parallel"),
        name="rmsnorm",
    )(x, g.reshape(1, d).astype(F32))


def _mm_kernel(a_ref, b_ref, o_ref):
    o_ref[...] = _dot(a_ref[...], b_ref[...]).astype(o_ref.dtype)


def matmul(a, b, out_dtype, tm=1024):
    m, k = a.shape
    n = b.shape[1]
    tn = _pick_tile(n, (1024, 512, 256, 128))
    return pl.pallas_call(
        _mm_kernel,
        grid=(m // tm, n // tn),
        in_specs=[pl.BlockSpec((tm, k), lambda i, j: (i, 0)), pl.BlockSpec((k, tn), lambda i, j: (0, j))],
        out_specs=pl.BlockSpec((tm, tn), lambda i, j: (i, j)),
        out_shape=jax.ShapeDtypeStruct((m, n), out_dtype),
        compiler_params=_cparams("parallel", "arbitrary"),
        name="matmul",
    )(a, b)


def _norm_mm_kernel(h_ref, g_ref, w_ref, o_ref, n_ref):
    @pl.when(pl.program_id(1) == 0)
    def _():
        x = h_ref[...]
        ms = jnp.mean(x * x, axis=-1, keepdims=True)
        n_ref[...] = (x * lax.rsqrt(ms + RMS_EPS) * g_ref[...]).astype(n_ref.dtype)

    o_ref[...] = _dot(n_ref[...], w_ref[...]).astype(o_ref.dtype)


def norm_matmul(h, g, w, out_dtype, tm=512):
    m, k = h.shape
    n = w.shape[1]
    tn = _pick_tile(n, (1024, 512, 256, 128))
    return pl.pallas_call(
        _norm_mm_kernel,
        grid=(m // tm, n // tn),
        in_specs=[pl.BlockSpec((tm, k), lambda i, j: (i, 0)), pl.BlockSpec((1, k), lambda i, j: (0, 0)),
                  pl.BlockSpec((k, tn), lambda i, j: (0, j))],
        out_specs=[pl.BlockSpec((tm, tn), lambda i, j: (i, j)), pl.BlockSpec((tm, k), lambda i, j: (i, 0))],
        out_shape=[jax.ShapeDtypeStruct((m, n), out_dtype), jax.ShapeDtypeStruct((m, k), BF16)],
        compiler_params=_cparams("parallel", "arbitrary"),
        name="norm_matmul",
    )(h, g.reshape(1, k).astype(F32), w)


def _mix_out_kernel(ya_ref, yb_ref, yc_ref, yd_ref, w_ref, res_ref, o_ref):
    kq = ya_ref.shape[1]
    acc = res_ref[...]
    for idx, y_ref in enumerate((ya_ref, yb_ref, yc_ref, yd_ref)):
        acc = acc + _dot(y_ref[...], w_ref[idx * kq:(idx + 1) * kq, :])
    o_ref[...] = acc


def mix_out_proj(ys, w, res, tm=1024, tn=512):
    m, kq = ys[0].shape
    k, n = w.shape
    y_spec = pl.BlockSpec((tm, kq), lambda i, j: (i, 0))
    return pl.pallas_call(
        _mix_out_kernel,
        grid=(m // tm, n // tn),
        in_specs=[y_spec, y_spec, y_spec, y_spec,
                  pl.BlockSpec((k, tn), lambda i, j: (0, j)),
                  pl.BlockSpec((tm, tn), lambda i, j: (i, j))],
        out_specs=pl.BlockSpec((tm, tn), lambda i, j: (i, j)),
        out_shape=jax.ShapeDtypeStruct((m, n), F32),
        compiler_params=_cparams("parallel", "arbitrary"),
        name="mix_out_proj",
    )(*ys, w, res)


def _down_res_kernel(a_ref, b_ref, res_ref, o_ref, acc_ref):
    kk = pl.program_id(2)

    @pl.when(kk == 0)
    def _():
        acc_ref[...] = res_ref[...]

    acc_ref[...] += _dot(a_ref[...], b_ref[...])

    @pl.when(kk == pl.num_programs(2) - 1)
    def _():
        o_ref[...] = acc_ref[...]


def down_proj_residual(a, b, res, tm=1024, tn=512, tk=DENSE_TK):
    m, k = a.shape
    n = b.shape[1]
    return pl.pallas_call(
        _down_res_kernel,
        grid=(m // tm, n // tn, k // tk),
        in_specs=[pl.BlockSpec((tm, tk), lambda i, j, kk: (i, kk)),
                  pl.BlockSpec((tk, tn), lambda i, j, kk: (kk, j)),
                  pl.BlockSpec((tm, tn), lambda i, j, kk: (i, j))],
        out_specs=pl.BlockSpec((tm, tn), lambda i, j, kk: (i, j)),
        out_shape=jax.ShapeDtypeStruct((m, n), F32),
        scratch_shapes=[pltpu.VMEM((tm, tn), F32)],
        compiler_params=_cparams("parallel", "parallel", "arbitrary"),
        name="down_proj_residual",
    )(a, b, res)


def _ple_kernel(h_ref, g_ref, wg_ref, p_ref, wp_ref, o_ref, n_ref):
    j = pl.program_id(1)
    tn = o_ref.shape[1]

    @pl.when(j == 0)
    def _():
        x = h_ref[...]
        ms = jnp.mean(x * x, axis=-1, keepdims=True)
        n_ref[...] = (x * lax.rsqrt(ms + RMS_EPS) * g_ref[...]).astype(n_ref.dtype)

    gate = jax.nn.sigmoid(_dot(n_ref[...], wg_ref[...]))
    emb = _dot(p_ref[...], wp_ref[...])
    o_ref[...] = h_ref[:, pl.ds(pl.multiple_of(j * tn, tn), tn)] + gate * emb


def ple_residual(h, g, wg, p, wp, tm=512, tn=512):
    m, k = h.shape
    nn = wg.shape[1]
    pd = p.shape[1]
    return pl.pallas_call(
        _ple_kernel,
        grid=(m // tm, nn // tn),
        in_specs=[pl.BlockSpec((tm, k), lambda i, j: (i, 0)),
                  pl.BlockSpec((1, k), lambda i, j: (0, 0)),
                  pl.BlockSpec((k, tn), lambda i, j: (0, j)),
                  pl.BlockSpec((tm, pd), lambda i, j: (i, 0)),
                  pl.BlockSpec((pd, tn), lambda i, j: (0, j))],
        out_specs=pl.BlockSpec((tm, tn), lambda i, j: (i, j)),
        out_shape=jax.ShapeDtypeStruct((m, nn), F32),
        scratch_shapes=[pltpu.VMEM((tm, k), BF16)],
        compiler_params=_cparams("parallel", "arbitrary"),
        name="ple_residual",
    )(h, g.reshape(1, k).astype(F32), wg, p, wp)


def _head_sum(x, ones_bd):
    outs = []
    for p in range(x.shape[1] // LANES):
        hi, lo = _split2(x[:, p * LANES:(p + 1) * LANES])
        outs.append(_dot(hi, ones_bd) + _dot(lo, ones_bd))
    return jnp.concatenate(outs, axis=1)


def _rwkv_kernel(z_ref, vf_ref, mu_ref, vec_ref, uw_ref, ua_ref, ug_ref, uv_ref,
                 y_ref, vout_ref, prev_ref, h_ref, *, has_v_res):
    tc = RWKV_CHUNK
    c = pl.program_id(1)

    @pl.when(c == 0)
    def _():
        prev_ref[...] = jnp.zeros_like(prev_ref)
        h_ref[...] = jnp.zeros_like(h_ref)

    z = z_ref[...]
    row = lax.broadcasted_iota(jnp.int32, z.shape, 0)
    zp = jnp.where(row == 0, prev_ref[...], pltpu.roll(z, 1, 0))
    prev_ref[...] = z[tc - 1:tc, :]
    zs = z + (zp - z) * mu_ref[...]
    r = zs[:, 0:C_A]
    k = zs[:, C_A:2 * C_A]
    v = zs[:, 2 * C_A:3 * C_A]
    lr = zs[:, 3 * C_A:3 * C_A + LORA_PAD]

    w0 = vec_ref[0:1, :]
    a0 = vec_ref[1:2, :]
    v0 = vec_ref[2:3, :]
    k_k = vec_ref[3:4, :]
    k_a = vec_ref[4:5, :]
    r_k = vec_ref[5:6, :]
    lnx_g = vec_ref[6:7, :]
    lnx_b = vec_ref[7:8, :]

    lr_b = lr.astype(BF16)
    w_lin = _dot(jnp.tanh(lr).astype(BF16), uw_ref[...])
    a_lin = _dot(lr_b, ua_ref[...])
    g = _dot(jax.nn.sigmoid(lr).astype(BF16), ug_ref[...])
    w = -jax.nn.softplus(-(w0 + w_lin)) - 0.5
    ld = -jnp.exp(w)
    a_lr = jax.nn.sigmoid(a0 + a_lin)
    if has_v_res:
        v_lin = _dot(lr_b, uv_ref[...])
        v = v + (vf_ref[...] - v) * jax.nn.sigmoid(v0 + v_lin)
    vout_ref[...] = v

    li = lax.broadcasted_iota(jnp.int32, (LANES, LANES), 0)
    lj = lax.broadcasted_iota(jnp.int32, (LANES, LANES), 1)
    same_head = (li // N_A) == (lj // N_A)
    ones_bd = jnp.where(same_head, 1.0, 0.0).astype(BF16)

    kk = k * k_k
    kk = kk / jnp.maximum(jnp.sqrt(_head_sum(kk * kk, ones_bd)), L2_EPS)
    k = k * (1.0 + (a_lr - 1.0) * k_a)
    a_s = -kk
    b_s = kk * a_lr

    ti = lax.broadcasted_iota(jnp.int32, (tc, tc), 0)
    tj = lax.broadcasted_iota(jnp.int32, (tc, tc), 1)
    tri = jnp.where(ti >= tj, 1.0, 0.0).astype(BF16)
    ld_h, ld_m, ld_l = _split3(ld)
    cum = _dot(tri, ld_h) + _dot(tri, ld_m) + _dot(tri, ld_l)
    cum_end = cum[tc - 1:tc, :]
    p_inc = jnp.exp(cum)
    p_exc = jnp.exp(cum - ld)
    p_inv = jnp.exp(-cum)
    p_end = jnp.exp(cum_end - cum)
    decay_end = jnp.exp(cum_end)

    a_t = a_s * p_exc
    b_t = b_s * p_inv
    k_t = k * p_inv
    r_t = r * p_inc
    b_h = b_s * p_end
    k_h = k * p_end

    first_head = lax.broadcasted_iota(jnp.int32, (tc, LANES), 1) < N_A
    strict = same_head & ((li % N_A) > (lj % N_A))
    incl = same_head & ((li % N_A) >= (lj % N_A))
    eye = li == lj

    def stack(x):
        return jnp.concatenate([jnp.where(first_head, x, 0.0), jnp.where(first_head, 0.0, x)], axis=0)

    def dup(x):
        return jnp.concatenate([x, x], axis=0)

    pairs = range(C_A // LANES)
    sls = [slice(p * LANES, (p + 1) * LANES) for p in pairs]
    n2 = 2 * tc
    a_st = [stack(a_t[:, sl]) for sl in sls]
    r_st = [stack(r_t[:, sl]) for sl in sls]
    v_st = [stack(v[:, sl]).astype(BF16) for sl in sls]
    bh_st = [stack(b_h[:, sl]).astype(BF16) for sl in sls]
    kh_st = [stack(k_h[:, sl]).astype(BF16) for sl in sls]
    lhs = [jnp.concatenate([a_st[p], r_st[p]], axis=0).astype(BF16) for p in pairs]
    rhs = [jnp.concatenate([dup(b_t[:, sl]), dup(k_t[:, sl])], axis=0).astype(BF16) for sl in sls]
    x = [_dot(lhs[p], rhs[p], NT) for p in pairs]
    lk = [jnp.where(strict, x[p][0:n2, 0:n2], 0.0) for p in pairs]
    a_ak = [jnp.where(strict, x[p][0:n2, n2:2 * n2], 0.0).astype(BF16) for p in pairs]
    a_rb = [jnp.where(incl, x[p][n2:2 * n2, 0:n2], 0.0).astype(BF16) for p in pairs]
    a_rk = [jnp.where(incl, x[p][n2:2 * n2, n2:2 * n2], 0.0).astype(BF16) for p in pairs]
    zz = [jnp.concatenate([a_st[p], _dot(a_ak[p], v_st[p])], axis=1) for p in pairs]
    n_iter = int(math.log2(tc))
    for it in range(n_iter):
        lk_b = [lk[p].astype(BF16) for p in pairs]
        zz = [zz[p] + _dot(lk_b[p], zz[p].astype(BF16)) for p in pairs]
        if it < n_iter - 1:
            lk = [_dot(lk_b[p], lk_b[p]) for p in pairs]
    zz_b = [zz[p].astype(BF16) for p in pairs]
    qy = [_dot(a_rb[p], zz_b[p]) for p in pairs]
    y3 = [qy[p][:, LANES:] + _dot(a_rk[p], v_st[p]) for p in pairs]
    mg = [_dot(bh_st[p], zz_b[p], TN) for p in pairs]
    g_mat = [mg[p][:, LANES:] + _dot(kh_st[p], v_st[p], TN) for p in pairs]
    qm = [jnp.concatenate([r_st[p] + qy[p][:, 0:LANES],
                           jnp.where(eye, decay_end[:, sls[p]], 0.0) + mg[p][:, 0:LANES]], axis=0).astype(BF16)
          for p in pairs]
    h_split = [_split2(h_ref[p]) for p in pairs]
    out = [_dot(qm[p], h_split[p][0]) + _dot(qm[p], h_split[p][1]) for p in pairs]
    y_parts = []
    for p in pairs:
        y_st = out[p][0:n2, :] + y3[p]
        y_parts.append(y_st[0:tc, :] + y_st[tc:n2, :])
        h_ref[p] = out[p][n2:, :] + g_mat[p]
    y = jnp.concatenate(y_parts, axis=1)

    inv_n = 1.0 / N_A
    mean = _head_sum(y, ones_bd) * inv_n
    d = y - mean
    var = _head_sum(d * d, ones_bd) * inv_n
    yn = d * lax.rsqrt(var + GN_EPS) * lnx_g + lnx_b
    bonus = _head_sum(r * k * r_k, ones_bd) * v
    y_ref[...] = ((yn + bonus) * g).astype(y_ref.dtype)


def rwkv_mix(proj_a, v_first, mu, vec, uw, ua, ug, uv, batch, seq, has_v_res):
    tc = RWKV_CHUNK
    nc = seq // tc
    row_spec = lambda w: pl.BlockSpec((tc, w), lambda b, c: (b * nc + c, 0))
    full = lambda s: pl.BlockSpec(s, lambda b, c: tuple(0 for _ in s))
    m = batch * seq
    return pl.pallas_call(
        functools.partial(_rwkv_kernel, has_v_res=has_v_res),
        grid=(batch, nc),
        in_specs=[row_spec(N_A_COLS), row_spec(C_A), full((1, N_A_COLS)), full((8, C_A)),
                  full((LORA_PAD, C_A)), full((LORA_PAD, C_A)), full((LORA_PAD, C_A)), full((LORA_PAD, C_A))],
        out_specs=[row_spec(C_A), row_spec(C_A)],
        out_shape=[jax.ShapeDtypeStruct((m, C_A), BF16), jax.ShapeDtypeStruct((m, C_A), F32)],
        scratch_shapes=[pltpu.VMEM((1, N_A_COLS), F32), pltpu.VMEM((C_A // LANES, LANES, LANES), F32)],
        compiler_params=_cparams("parallel", "arbitrary"),
        name="rwkv_mix",
    )(proj_a, v_first, mu, vec, uw, ua, ug, uv)


def _gelu(x):
    return 0.5 * x * (1.0 + lax.erf(x * math.sqrt(0.5)))


def _gmlp_kernel(z_ref, lng_ref, lnb_ref, w_ref, bs_ref, og_ref, o_ref):
    u = _gelu(z_ref[:, 0:C_B])
    v = _gelu(z_ref[:, C_B:2 * C_B])
    mu = jnp.mean(v, axis=-1, keepdims=True)
    d = v - mu
    var = jnp.mean(d * d, axis=-1, keepdims=True)
    vn = d * lax.rsqrt(var + LN_EPS) * lng_ref[...] + lnb_ref[...]
    ti = lax.broadcasted_iota(jnp.int32, (CHUNK, CHUNK), 0)
    tj = lax.broadcasted_iota(jnp.int32, (CHUNK, CHUNK), 1)
    causal = ti >= tj
    dh = C_B // H_B
    for h in range(H_B):
        sl = slice(h * dh, (h + 1) * dh)
        w = jnp.where(causal, w_ref[h], 0.0).astype(BF16)
        s = _dot(w, vn[:, sl].astype(BF16)) + bs_ref[:, h:h + 1]
        y = u[:, sl] * s
        ms = jnp.mean(y * y, axis=-1, keepdims=True)
        o_ref[:, sl] = (y * lax.rsqrt(ms + RMS_EPS) * og_ref[:, sl]).astype(o_ref.dtype)


def gmlp_mix(proj_bd, ln_g, ln_b, w_s, b_s_t, out_g):
    m = proj_bd.shape[0]
    full = lambda s: pl.BlockSpec(s, lambda i: tuple(0 for _ in s))
    return pl.pallas_call(
        _gmlp_kernel,
        grid=(m // CHUNK,),
        in_specs=[pl.BlockSpec((CHUNK, 2 * C_B), lambda i: (i, 0)),
                  full((1, C_B)), full((1, C_B)), full((H_B, CHUNK, CHUNK)), full((CHUNK, H_B)), full((1, C_B))],
        out_specs=pl.BlockSpec((CHUNK, C_B), lambda i: (i, 0)),
        out_shape=jax.ShapeDtypeStruct((m, C_B), BF16),
        compiler_params=_cparams("parallel"),
        name="gmlp_mix",
    )(proj_bd, ln_g, ln_b, w_s, b_s_t, out_g)


def _sb_kernel(q_ref, k_ref, v_ref, g_ref, o_ref, *, tq, scale):
    qi = pl.program_id(2)
    q = q_ref[...]
    row = lax.broadcasted_iota(jnp.int32, (tq, tq), 0)
    col = lax.broadcasted_iota(jnp.int32, (tq, tq), 1)
    below = col < row
    upper = jnp.where(row > col, 1.0, 0.0).astype(BF16)

    def block(j, carry, acc, diagonal):
        start = pl.multiple_of(j * tq, tq)
        kb = k_ref[pl.ds(start, tq), :]
        vb = v_ref[pl.ds(start, tq), :]
        z = _dot(q, kb, NT) * scale
        log_beta = jnp.minimum(z, 0.0) - jnp.log(1.0 + jnp.exp(-jnp.abs(z)))
        log_1mb = log_beta - z
        if diagonal:
            log_1mb = jnp.where(below, log_1mb, 0.0)
        hi, lo = _split2(log_1mb)
        after = _dot(hi, upper) + _dot(lo, upper)
        att = jnp.exp(log_beta + after + carry)
        if diagonal:
            att = jnp.where(below, att, 0.0)
        acc = acc + _dot(att.astype(BF16), vb)
        carry = carry + after[:, 0:1] + log_1mb[:, 0:1]
        return carry, acc

    carry0 = jnp.zeros((tq, 1), F32)
    acc0 = jnp.zeros((tq, D_C), F32)
    carry, acc = block(qi, carry0, acc0, True)

    def more(state):
        return jnp.logical_and(state[0] >= 0, jnp.max(state[1]) > SB_CUTOFF)

    def body(state):
        carry_j, acc_j = block(state[0], state[1], state[2], False)
        return state[0] - 1, carry_j, acc_j

    _, carry, acc = lax.while_loop(more, body, (qi - 1, carry, acc))
    ms = jnp.mean(acc * acc, axis=-1, keepdims=True)
    o_ref[...] = (acc * lax.rsqrt(ms + RMS_EPS) * g_ref[...]).astype(o_ref.dtype)


def sb_attention(proj_cd, out_g, batch, seq, tq=256):
    nq = seq // tq
    return pl.pallas_call(
        functools.partial(_sb_kernel, tq=tq, scale=D_C ** -0.5),
        grid=(batch, H_C, nq),
        in_specs=[pl.BlockSpec((None, tq, D_C), lambda b, h, i: (b, i, h)),
                  pl.BlockSpec((None, seq, D_C), lambda b, h, i: (b, 0, H_C + h)),
                  pl.BlockSpec((None, seq, D_C), lambda b, h, i: (b, 0, 2 * H_C + h)),
                  pl.BlockSpec((1, D_C), lambda b, h, i: (0, h))],
        out_specs=pl.BlockSpec((None, tq, D_C), lambda b, h, i: (b, i, h)),
        out_shape=jax.ShapeDtypeStruct((batch, seq, C_C), BF16),
        compiler_params=_cparams("parallel", "parallel", "arbitrary"),
        name="sb_attention",
    )(proj_cd, proj_cd, proj_cd, out_g)


def _rope_table_kernel(pos_ref, invf_ref, cos_ref, sin_ref):
    ang = pos_ref[...].astype(F32) * invf_ref[...]
    cos_ref[...] = jnp.cos(ang)
    sin_ref[...] = jnp.sin(ang)


def rope_tables(pos_col, invf, tm=512):
    m = pos_col.shape[0]
    spec = pl.BlockSpec((tm, LANES), lambda i: (i, 0))
    return pl.pallas_call(
        _rope_table_kernel,
        grid=(m // tm,),
        in_specs=[pl.BlockSpec((tm, 1), lambda i: (i, 0)), pl.BlockSpec((1, LANES), lambda i: (0, 0))],
        out_specs=[spec, spec],
        out_shape=[jax.ShapeDtypeStruct((m, LANES), F32)] * 2,
        compiler_params=_cparams("parallel"),
        name="rope_tables",
    )(pos_col, invf)


def _rope_kernel(x_ref, cos_ref, sin_ref, o_ref):
    x = x_ref[...]
    width = x.shape[1]
    reps = width // LANES
    c = jnp.concatenate([cos_ref[...]] * reps, axis=1)
    s = jnp.concatenate([sin_ref[...]] * reps, axis=1)
    lane = lax.broadcasted_iota(jnp.int32, x.shape, 1)
    first_half = (lane % DH_D) < (DH_D // 2)
    rot = jnp.where(first_half, -pltpu.roll(x, width - DH_D // 2, 1), pltpu.roll(x, DH_D // 2, 1))
    o_ref[...] = (x * c + rot * s).astype(o_ref.dtype)


def rope_qk(proj_bd, cos_t, sin_t, tm=256):
    m = proj_bd.shape[0]
    width = 2 * C_D
    return pl.pallas_call(
        _rope_kernel,
        grid=(m // tm,),
        in_specs=[pl.BlockSpec((tm, width), lambda i: (i, 1)),
                  pl.BlockSpec((tm, LANES), lambda i: (i, 0)),
                  pl.BlockSpec((tm, LANES), lambda i: (i, 0))],
        out_specs=pl.BlockSpec((tm, width), lambda i: (i, 0)),
        out_shape=jax.ShapeDtypeStruct((m, width), BF16),
        compiler_params=_cparams("parallel"),
        name="rope_qk",
    )(proj_bd, cos_t, sin_t)


def _diff_kernel(lam_ref, q_ref, k_ref, v_ref, g_ref, o_ref, *, tq, lam_init):
    qi = pl.program_id(2)
    q = (q_ref[...].astype(F32) * (DH_D ** -0.5)).astype(BF16)
    lane = lax.broadcasted_iota(jnp.int32, q.shape, 1)
    zero = jnp.zeros_like(q)
    qs = jnp.concatenate([jnp.where(lane < DH_D, q, zero), jnp.where(lane < DH_D, zero, q)], axis=0)
    row = lax.broadcasted_iota(jnp.int32, (2 * tq, tq), 0) % tq
    col = lax.broadcasted_iota(jnp.int32, (2 * tq, tq), 1)
    visible = col <= row

    def block(j, m, l, acc, diagonal):
        start = pl.multiple_of(j * tq, tq)
        kb = k_ref[pl.ds(start, tq), :]
        vb = v_ref[pl.ds(start, tq), :]
        s = _dot(qs, kb, NT)
        if diagonal:
            s = jnp.where(visible, s, -jnp.inf)
        m_new = jnp.maximum(m, jnp.max(s, axis=-1, keepdims=True))
        alpha = jnp.exp(m - m_new)
        p = jnp.exp(s - m_new)
        l = alpha * l + jnp.sum(p, axis=-1, keepdims=True)
        acc = alpha * acc + _dot(p.astype(BF16), vb)
        return m_new, l, acc

    m0 = jnp.full((2 * tq, 1), -jnp.inf, F32)
    l0 = jnp.zeros((2 * tq, 1), F32)
    acc0 = jnp.zeros((2 * tq, DV_D), F32)
    m, l, acc = block(qi, m0, l0, acc0, True)

    def body(i, state):
        return block(i, state[0], state[1], state[2], False)

    m, l, acc = lax.fori_loop(0, qi, body, (m, l, acc))
    lv = lam_ref[...]
    lam = (jnp.exp(jnp.sum(lv[0:1, :] * lv[1:2, :], axis=-1, keepdims=True))
           - jnp.exp(jnp.sum(lv[2:3, :] * lv[3:4, :], axis=-1, keepdims=True)) + lam_init)
    o = acc / l
    y = o[0:tq, :] - lam * o[tq:2 * tq, :]
    ms = jnp.mean(y * y, axis=-1, keepdims=True)
    o_ref[...] = (y * lax.rsqrt(ms + LN_EPS) * g_ref[...] * (1.0 - lam_init)).astype(o_ref.dtype)


def diff_attention(lam_p, qk_rot, proj_cd, subln, batch, seq, lam_init, tq=512):
    nq = seq // tq
    return pl.pallas_call(
        functools.partial(_diff_kernel, tq=tq, lam_init=lam_init),
        grid=(batch, H_D, nq),
        in_specs=[pl.BlockSpec((4, DH_D), lambda b, h, i: (0, 0)),
                  pl.BlockSpec((None, tq, DV_D), lambda b, h, i: (b, i, h)),
                  pl.BlockSpec((None, seq, DV_D), lambda b, h, i: (b, 0, H_D + h)),
                  pl.BlockSpec((None, seq, DV_D), lambda b, h, i: (b, 0, 3 * H_C + h)),
                  pl.BlockSpec((1, DV_D), lambda b, h, i: (0, 0))],
        out_specs=pl.BlockSpec((None, tq, DV_D), lambda b, h, i: (b, i, h)),
        out_shape=jax.ShapeDtypeStruct((batch, seq, C_D), BF16),
        compiler_params=_cparams("parallel", "parallel", "arbitrary"),
        name="diff_attention",
    )(lam_p, qk_rot, qk_rot, proj_cd, subln)


def _router_kernel(h_ref, g_ref, wr_ref, n_ref, comb_ref, idx_ref):
    x = h_ref[...]
    ms = jnp.mean(x * x, axis=-1, keepdims=True)
    n = x * lax.rsqrt(ms + RMS_EPS) * g_ref[...]
    n_ref[...] = n.astype(n_ref.dtype)
    n_hi, n_lo = _split2(n)
    w_hi, w_lo = _split2(wr_ref[...])
    logits = _dot(n_hi, w_hi) + _dot(n_lo, w_hi) + _dot(n_hi, w_lo)
    lane = lax.broadcasted_iota(jnp.int32, logits.shape, 1)
    neg = -jnp.inf
    logits = jnp.where(lane < N_EXPERTS, logits, neg)
    m1 = jnp.max(logits, axis=-1, keepdims=True)
    i1 = jnp.min(jnp.where(logits == m1, lane, LANES), axis=-1, keepdims=True)
    rest = jnp.where(lane == i1, neg, logits)
    m2 = jnp.max(rest, axis=-1, keepdims=True)
    i2 = jnp.min(jnp.where(rest == m2, lane, LANES), axis=-1, keepdims=True)
    e = jnp.exp(m2 - m1)
    g1 = 1.0 / (1.0 + e)
    g2 = e / (1.0 + e)
    comb_ref[...] = jnp.where(lane == i1, g1, 0.0) + jnp.where(lane == i2, g2, 0.0)
    idx_ref[...] = jnp.where(lane == 0, i1, jnp.where(lane == 1, i2, 0))


def moe_router(h, g, w_router_pad, tm=256):
    m, d = h.shape
    spec = pl.BlockSpec((tm, LANES), lambda i: (i, 0))
    return pl.pallas_call(
        _router_kernel,
        grid=(m // tm,),
        in_specs=[pl.BlockSpec((tm, d), lambda i: (i, 0)), pl.BlockSpec((1, d), lambda i: (0, 0)),
                  pl.BlockSpec((d, LANES), lambda i: (0, 0))],
        out_specs=[pl.BlockSpec((tm, d), lambda i: (i, 0)), spec, spec],
        out_shape=[jax.ShapeDtypeStruct((m, d), BF16), jax.ShapeDtypeStruct((m, LANES), F32),
                   jax.ShapeDtypeStruct((m, LANES), jnp.int32)],
        compiler_params=_cparams("parallel"),
        name="moe_router",
    )(h, g, w_router_pad)


def _tile_state(te_ref):
    i = pl.program_id(1)
    in_use = i < te_ref[pl.num_programs(1)]
    new_weights = jnp.logical_or(i == 0, te_ref[i] != te_ref[jnp.maximum(i - 1, 0)])
    return in_use, new_weights


def _gmm_up_kernel(te_ref, a_ref, wg_ref, wu_ref, o_ref, wg_bf, wu_bf):
    in_use, new_weights = _tile_state(te_ref)

    @pl.when(new_weights)
    def _():
        wg_bf[...] = wg_ref[...].astype(BF16)
        wu_bf[...] = wu_ref[...].astype(BF16)

    @pl.when(in_use)
    def _():
        a = a_ref[...]
        g = _dot(a, wg_bf[...])
        u = _dot(a, wu_bf[...])
        o_ref[...] = (g * jax.nn.sigmoid(g) * u).astype(o_ref.dtype)

    @pl.when(jnp.logical_not(in_use))
    def _():
        o_ref[...] = jnp.zeros_like(o_ref)


def _row_spec(tm, width, n_tiles):
    return pl.BlockSpec((tm, width), lambda j, i, te: (jnp.minimum(i, te[n_tiles] - 1), 0))


def gmm_swiglu_up(tile_table, xs, wg, wu, tm, tn):
    s, k = xs.shape
    n = wg.shape[2]
    n_tiles = s // tm
    w_spec = pl.BlockSpec((None, k, tn), lambda j, i, te: (te[i], 0, j))
    return pl.pallas_call(
        _gmm_up_kernel,
        grid_spec=pltpu.PrefetchScalarGridSpec(
            num_scalar_prefetch=1,
            grid=(n // tn, n_tiles),
            in_specs=[_row_spec(tm, k, n_tiles), w_spec, w_spec],
            out_specs=pl.BlockSpec((tm, tn), lambda j, i, te: (i, j)),
            scratch_shapes=[pltpu.VMEM((k, tn), BF16), pltpu.VMEM((k, tn), BF16)],
        ),
        out_shape=jax.ShapeDtypeStruct((s, n), BF16),
        compiler_params=_cparams("arbitrary", "arbitrary"),
        name="gmm_swiglu_up",
    )(tile_table, xs, wg, wu)


def _gmm_down_kernel(te_ref, a_ref, w_ref, gate_ref, o_ref, w_bf):
    in_use, new_weights = _tile_state(te_ref)

    @pl.when(new_weights)
    def _():
        w_bf[...] = w_ref[...].astype(BF16)

    @pl.when(in_use)
    def _():
        o_ref[...] = _dot(a_ref[...], w_bf[...]) * gate_ref[...]

    @pl.when(jnp.logical_not(in_use))
    def _():
        o_ref[...] = jnp.zeros_like(o_ref)


def gmm_down(tile_table, hs, wd, slot_gate, tm, tn=512):
    s, k = hs.shape
    n = wd.shape[2]
    n_tiles = s // tm
    return pl.pallas_call(
        _gmm_down_kernel,
        grid_spec=pltpu.PrefetchScalarGridSpec(
            num_scalar_prefetch=1,
            grid=(n // tn, n_tiles),
            in_specs=[_row_spec(tm, k, n_tiles),
                      pl.BlockSpec((None, k, tn), lambda j, i, te: (te[i], 0, j)),
                      _row_spec(tm, 1, n_tiles)],
            out_specs=pl.BlockSpec((tm, tn), lambda j, i, te: (i, j)),
            scratch_shapes=[pltpu.VMEM((k, tn), BF16)],
        ),
        out_shape=jax.ShapeDtypeStruct((s, n), F32),
        compiler_params=_cparams("arbitrary", "arbitrary"),
        name="gmm_down",
    )(tile_table, hs, wd, slot_gate)


def moe_layer(h, ffn_g, w_router, wg, wu, wd, tm=MOE_TM):
    m, d = h.shape
    w_router_pad = jnp.pad(w_router.astype(F32), ((0, 0), (0, LANES - N_EXPERTS)))
    n_b, comb, idx = moe_router(h, ffn_g.reshape(1, d).astype(F32), w_router_pad)
    top_idx = idx[:, 0:2]
    gates = jnp.take_along_axis(comb[:, 0:N_EXPERTS], top_idx, axis=1)

    flat_e = top_idx.reshape(-1)
    n_pairs = 2 * m
    onehot = (flat_e[:, None] == jnp.arange(N_EXPERTS, dtype=jnp.int32)[None, :]).astype(jnp.int32)
    counts = jnp.sum(onehot, axis=0)
    rank = jnp.sum((jnp.cumsum(onehot, axis=0) - onehot) * onehot, axis=1)
    padded = ((counts + tm - 1) // tm) * tm
    pad_off = jnp.cumsum(padded) - padded
    raw_off = jnp.cumsum(counts) - counts
    dest = pad_off[flat_e] + rank
    n_tiles = n_pairs // tm + N_EXPERTS
    n_slots = n_tiles * tm
    order = jnp.argsort(flat_e, stable=True).astype(jnp.int32)
    tile_start = jnp.arange(n_tiles, dtype=jnp.int32) * tm
    pad_end = pad_off + padded
    tile_expert = jnp.minimum(jnp.sum((tile_start[:, None] >= pad_end[None, :]).astype(jnp.int32), axis=1),
                              N_EXPERTS - 1).astype(jnp.int32)
    slot = jnp.arange(n_slots, dtype=jnp.int32)
    slot_e = jnp.repeat(tile_expert, tm)
    local = slot - pad_off[slot_e]
    valid = (local < counts[slot_e]) & (slot < pad_end[N_EXPERTS - 1])
    src_pair = order[jnp.clip(raw_off[slot_e] + local, 0, n_pairs - 1)]
    slot_token = jnp.where(valid, src_pair // 2, 0)
    slot_gate = jnp.where(valid, gates.reshape(-1)[src_pair], 0.0).astype(F32).reshape(n_slots, 1)

    tiles_in_use = (pad_end[N_EXPERTS - 1] // tm).astype(jnp.int32).reshape(1)
    tile_table = jnp.concatenate([tile_expert, tiles_in_use])

    xs = jnp.take(n_b, slot_token, axis=0, mode="clip")
    hs = gmm_swiglu_up(tile_table, xs, wg, wu, tm, min(MOE_TN, wg.shape[2]))
    ys = gmm_down(tile_table, hs, wd, slot_gate, tm)
    dest2 = dest.reshape(m, 2)
    return h + jnp.take(ys, dest2[:, 0], axis=0, mode="clip") + jnp.take(ys, dest2[:, 1], axis=0, mode="clip")


def _rwkv_columns(w_in, layer):
    d = w_in.shape[0]
    o_w = C_A
    o_k = o_w + W_LORA
    o_v = o_k + C_A
    o_a = o_v + C_A
    o_g = o_a + A_LORA
    parts = [w_in[:, 0:C_A], w_in[:, o_k:o_k + C_A], w_in[:, o_v:o_v + C_A],
             w_in[:, o_w:o_w + W_LORA], w_in[:, o_a:o_a + A_LORA], w_in[:, o_g:o_g + G_LORA]]
    used = W_LORA + A_LORA + G_LORA
    if layer > 0:
        parts.append(w_in[:, N_IN0:N_IN0 + V_LORA])
        used += V_LORA
    parts.append(jnp.zeros((d, LORA_PAD - used), w_in.dtype))
    return jnp.concatenate(parts, axis=1)


def _rwkv_mu(mu, mu_v):
    o_w = C_A
    o_k = o_w + W_LORA
    o_v = o_k + C_A
    o_a = o_v + C_A
    o_g = o_a + A_LORA
    parts = [mu[0:C_A], mu[o_k:o_k + C_A], mu[o_v:o_v + C_A],
             mu[o_w:o_w + W_LORA], mu[o_a:o_a + A_LORA], mu[o_g:o_g + G_LORA]]
    used = W_LORA + A_LORA + G_LORA
    if mu_v is not None:
        parts.append(mu_v)
        used += V_LORA
    parts.append(jnp.zeros((LORA_PAD - used,), mu.dtype))
    return jnp.concatenate(parts).reshape(1, N_A_COLS).astype(F32)


def _pad_rows(w, offset):
    return jnp.pad(w, ((offset, LORA_PAD - offset - w.shape[0]), (0, 0))).astype(BF16)


def kernel(x, p, positions, attn_norm, w_in0, w_in_rest, w_out, rwkv_mu, rwkv_mu_v, rwkv_w0, rwkv_w_up, rwkv_a0, rwkv_a_up, rwkv_v0, rwkv_v_up, rwkv_g_up, rwkv_k_k, rwkv_k_a, rwkv_r_k, rwkv_lnx_g, rwkv_lnx_b, gm_ln_g, gm_ln_b, gm_w_s, gm_b_s, gm_out_g, sb_out_g, diff_lambda, diff_subln, ffn_norm, dense_w_gate, dense_w_up, dense_w_down, moe_router, moe_w_gate, moe_w_up, moe_w_down, ple_norm, ple_w_gate, ple_w_proj, final_norm):
    batch, seq, d = x.shape
    m = batch * seq
    h = x.reshape(m, d).astype(F32)

    inv_freq = 1.0 / (ROPE_THETA ** (jnp.arange(0, DH_D, 2, dtype=F32) / DH_D))
    invf = jnp.tile(inv_freq, LANES // (DH_D // 2)).reshape(1, LANES)
    cos_t, sin_t = rope_tables(positions.reshape(m, 1), invf)

    v_first = None
    for i in range(DEPTH):
        w_in = w_in0 if i == 0 else w_in_rest[i - 1]
        o_b = N_RWKV
        o_c = o_b + N_GMLP
        o_d = o_c + N_SB
        w_a = _rwkv_columns(w_in, i).astype(BF16)
        w_bd = jnp.concatenate([w_in[:, o_b:o_c], w_in[:, o_d:o_d + 2 * C_D]], axis=1).astype(BF16)
        w_cd = jnp.concatenate([w_in[:, o_c:o_d], w_in[:, o_d + 2 * C_D:o_d + 3 * C_D]], axis=1).astype(BF16)

        proj_a, n = norm_matmul(h, attn_norm[i], w_a, F32)
        proj_bd = matmul(n, w_bd, F32)
        proj_cd = matmul(n, w_cd, BF16)

        mu = _rwkv_mu(rwkv_mu[i], rwkv_mu_v[i - 1] if i > 0 else None)
        zeros_c = jnp.zeros((C_A,), F32)
        vec = jnp.stack([rwkv_w0[i], rwkv_a0[i], rwkv_v0[i - 1] if i > 0 else zeros_c, rwkv_k_k[i], rwkv_k_a[i],
                         rwkv_r_k[i].reshape(C_A), rwkv_lnx_g[i], rwkv_lnx_b[i]]).astype(F32)
        uw = _pad_rows(rwkv_w_up[i], 0)
        ua = _pad_rows(rwkv_a_up[i], W_LORA)
        ug = _pad_rows(rwkv_g_up[i], W_LORA + A_LORA)
        uv = _pad_rows(rwkv_v_up[i - 1], W_LORA + A_LORA + G_LORA) if i > 0 else jnp.zeros((LORA_PAD, C_A), BF16)
        vf_in = v_first if i > 0 else proj_a
        y_a, v_out = rwkv_mix(proj_a, vf_in, mu, vec, uw, ua, ug, uv, batch, seq, has_v_res=i > 0)
        if i == 0:
            v_first = v_out

        y_b = gmlp_mix(proj_bd, gm_ln_g[i].reshape(1, C_B), gm_ln_b[i].reshape(1, C_B), gm_w_s[i],
                       gm_b_s[i].T, gm_out_g[i].reshape(1, C_B))

        proj_cd3 = proj_cd.reshape(batch, seq, 4 * C_C)
        y_c = sb_attention(proj_cd3, sb_out_g[i].reshape(1, C_C), batch, seq).reshape(m, C_C)

        qk_rot = rope_qk(proj_bd, cos_t, sin_t).reshape(batch, seq, 2 * C_D)
        lam_init = 0.8 - 0.6 * math.exp(-0.3 * i)
        y_d = diff_attention(diff_lambda[i].astype(F32), qk_rot, proj_cd3, diff_subln[i].reshape(1, DV_D),
                             batch, seq, lam_init).reshape(m, C_D)

        h = mix_out_proj((y_a, y_b, y_c, y_d), w_out[i].astype(BF16), h)

        j = i // 2
        if i % 2 == 0:
            n = rmsnorm(h, ffn_norm[i], BF16)
            n_row_tiles = m // DENSE_TM
            one_group = jnp.concatenate([jnp.zeros((n_row_tiles,), jnp.int32), jnp.full((1,), n_row_tiles, jnp.int32)])
            hid = gmm_swiglu_up(one_group, n, dense_w_gate[j][None], dense_w_up[j][None], DENSE_TM, DENSE_TN)
            h = down_proj_residual(hid, dense_w_down[j].astype(BF16), h)
        else:
            h = moe_layer(h, ffn_norm[i], moe_router[j], moe_w_gate[j], moe_w_up[j], moe_w_down[j])

        h = ple_residual(h, ple_norm[i], ple_w_gate[i].astype(BF16), p[i].reshape(m, P_DIM).astype(BF16),
                         ple_w_proj[i].astype(BF16))

    return rmsnorm(h, final_norm, x.dtype).reshape(batch, seq, d)
```

```python
import functools
import math

import jax
import jax.numpy as jnp
import numpy as np
from jax import lax
from jax.experimental import pallas as pl
from jax.experimental.pallas import tpu as pltpu

F32 = jnp.float32
BF16 = jnp.bfloat16

D_MODEL = 4096
DEPTH = 2
C_A = 1024
N_A = 64
W_LORA, A_LORA, V_LORA, G_LORA = 64, 64, 32, 160
C_B = 1024
H_B = 8
CHUNK = 128
C_C = 1024
H_C = 8
D_C = 128
C_D = 1024
H_D = 8
DV_D = 128
DH_D = 64
D_FF = 11008
N_EXPERTS = 8
D_FF_EXPERT = 5632
P_DIM = 256
ROPE_THETA = 10000.0
RMS_EPS = 1e-6
LN_EPS = 1e-5
GN_EPS = 64e-5
L2_EPS = 1e-12
N_RWKV = 3 * C_A + W_LORA + A_LORA + G_LORA
N_GMLP = 2 * C_B
N_SB = 3 * C_C
N_DIFF = 3 * C_D
N_IN0 = N_RWKV + N_GMLP + N_SB + N_DIFF

LANES = 128
VMEM_LIMIT = 56 * 1024 * 1024

LORA_PAD = 512
N_A_COLS = 3 * C_A + LORA_PAD
DENSE_TM, DENSE_TN = 1024, 256
DENSE_TK = 5504
MOE_TM, MOE_TN = 512, 512
RWKV_CHUNK = 64
SB_CUTOFF = -110.0

NT = (((1,), (1,)), ((), ()))
TN = (((0,), (0,)), ((), ()))


def _cparams(*sem):
    return pltpu.CompilerParams(dimension_semantics=sem, vmem_limit_bytes=VMEM_LIMIT)


def _dot(a, b, dims=None):
    if dims is None:
        return jnp.dot(a, b, preferred_element_type=F32)
    return lax.dot_general(a, b, dims, preferred_element_type=F32)


def _split2(x):
    hi = x.astype(BF16)
    lo = (x - hi.astype(F32)).astype(BF16)
    return hi, lo


def _split3(x):
    hi = x.astype(BF16)
    r = x - hi.astype(F32)
    mid = r.astype(BF16)
    lo = (r - mid.astype(F32)).astype(BF16)
    return hi, mid, lo


def _pick_tile(n, prefs):
    for t in prefs:
        if n % t == 0:
            return t
    raise ValueError(f"no tile for {n}")


def _rmsnorm_kernel(x_ref, g_ref, o_ref, *, eps):
    x = x_ref[...].astype(F32)
    ms = jnp.mean(x * x, axis=-1, keepdims=True)
    o_ref[...] = (x * lax.rsqrt(ms + eps) * g_ref[...]).astype(o_ref.dtype)


def rmsnorm(x, g, out_dtype, tm=512):
    m, d = x.shape
    return pl.pallas_call(
        functools.partial(_rmsnorm_kernel, eps=RMS_EPS),
        grid=(m // tm,),
        in_specs=[pl.BlockSpec((tm, d), lambda i: (i, 0)), pl.BlockSpec((1, d), lambda i: (0, 0))],
        out_specs=pl.BlockSpec((tm, d), lambda i: (i, 0)),
        out_shape=jax.ShapeDtypeStruct((m, d), out_dtype),
        compiler_params=_cparams("parallel"),
        name="rmsnorm",
    )(x, g.reshape(1, d).astype(F32))


def _mm_kernel(a_ref, b_ref, o_ref):
    o_ref[...] = _dot(a_ref[...], b_ref[...]).astype(o_ref.dtype)


def matmul(a, b, out_dtype, tm=1024):
    m, k = a.shape
    n = b.shape[1]
    tn = _pick_tile(n, (1024, 512, 256, 128))
    return pl.pallas_call(
        _mm_kernel,
        grid=(m // tm, n // tn),
        in_specs=[pl.BlockSpec((tm, k), lambda i, j: (i, 0)), pl.BlockSpec((k, tn), lambda i, j: (0, j))],
        out_specs=pl.BlockSpec((tm, tn), lambda i, j: (i, j)),
        out_shape=jax.ShapeDtypeStruct((m, n), out_dtype),
        compiler_params=_cparams("parallel", "arbitrary"),
        name="matmul",
    )(a, b)


def _mix_out_kernel(ya_ref, yb_ref, yc_ref, yd_ref, w_ref, res_ref, o_ref):
    kq = ya_ref.shape[1]
    acc = res_ref[...]
    for idx, y_ref in enumerate((ya_ref, yb_ref, yc_ref, yd_ref)):
        acc = acc + _dot(y_ref[...], w_ref[idx * kq:(idx + 1) * kq, :])
    o_ref[...] = acc


def mix_out_proj(ys, w, res, tm=1024, tn=512):
    m, kq = ys[0].shape
    k, n = w.shape
    y_spec = pl.BlockSpec((tm, kq), lambda i, j: (i, 0))
    return pl.pallas_call(
        _mix_out_kernel,
        grid=(m // tm, n // tn),
        in_specs=[y_spec, y_spec, y_spec, y_spec,
                  pl.BlockSpec((k, tn), lambda i, j: (0, j)),
                  pl.BlockSpec((tm, tn), lambda i, j: (i, j))],
        out_specs=pl.BlockSpec((tm, tn), lambda i, j: (i, j)),
        out_shape=jax.ShapeDtypeStruct((m, n), F32),
        compiler_params=_cparams("parallel", "arbitrary"),
        name="mix_out_proj",
    )(*ys, w, res)


def _down_res_kernel(a_ref, b_ref, res_ref, o_ref, acc_ref):
    kk = pl.program_id(2)

    @pl.when(kk == 0)
    def _():
        acc_ref[...] = res_ref[...]

    acc_ref[...] += _dot(a_ref[...], b_ref[...])

    @pl.when(kk == pl.num_programs(2) - 1)
    def _():
        o_ref[...] = acc_ref[...]


def down_proj_residual(a, b, res, tm=1024, tn=512, tk=DENSE_TK):
    m, k = a.shape
    n = b.shape[1]
    return pl.pallas_call(
        _down_res_kernel,
        grid=(m // tm, n // tn, k // tk),
        in_specs=[pl.BlockSpec((tm, tk), lambda i, j, kk: (i, kk)),
                  pl.BlockSpec((tk, tn), lambda i, j, kk: (kk, j)),
                  pl.BlockSpec((tm, tn), lambda i, j, kk: (i, j))],
        out_specs=pl.BlockSpec((tm, tn), lambda i, j, kk: (i, j)),
        out_shape=jax.ShapeDtypeStruct((m, n), F32),
        scratch_shapes=[pltpu.VMEM((tm, tn), F32)],
        compiler_params=_cparams("parallel", "parallel", "arbitrary"),
        name="down_proj_residual",
    )(a, b, res)


def _ple_kernel(n_ref, wg_ref, p_ref, wp_ref, res_ref, o_ref):
    gate = jax.nn.sigmoid(_dot(n_ref[...], wg_ref[...]))
    emb = _dot(p_ref[...], wp_ref[...])
    o_ref[...] = res_ref[...] + gate * emb


def ple_residual(n, wg, p, wp, res, tm=1024, tn=512):
    m, k = n.shape
    nn = wg.shape[1]
    pd = p.shape[1]
    return pl.pallas_call(
        _ple_kernel,
        grid=(m // tm, nn // tn),
        in_specs=[pl.BlockSpec((tm, k), lambda i, j: (i, 0)),
                  pl.BlockSpec((k, tn), lambda i, j: (0, j)),
                  pl.BlockSpec((tm, pd), lambda i, j: (i, 0)),
                  pl.BlockSpec((pd, tn), lambda i, j: (0, j)),
                  pl.BlockSpec((tm, tn), lambda i, j: (i, j))],
        out_specs=pl.BlockSpec((tm, tn), lambda i, j: (i, j)),
        out_shape=jax.ShapeDtypeStruct((m, nn), F32),
        compiler_params=_cparams("parallel", "arbitrary"),
        name="ple_residual",
    )(n, wg, p, wp, res)


def _head_sum(x, ones_bd):
    outs = []
    for p in range(x.shape[1] // LANES):
        hi, lo = _split2(x[:, p * LANES:(p + 1) * LANES])
        outs.append(_dot(hi, ones_bd) + _dot(lo, ones_bd))
    return jnp.concatenate(outs, axis=1)


def _rwkv_kernel(z_ref, vf_ref, mu_ref, vec_ref, uw_ref, ua_ref, ug_ref, uv_ref,
                 y_ref, vout_ref, prev_ref, h_ref, *, has_v_res):
    tc = RWKV_CHUNK
    c = pl.program_id(1)

    @pl.when(c == 0)
    def _():
        prev_ref[...] = jnp.zeros_like(prev_ref)
        h_ref[...] = jnp.zeros_like(h_ref)

    z = z_ref[...]
    row = lax.broadcasted_iota(jnp.int32, z.shape, 0)
    zp = jnp.where(row == 0, prev_ref[...], pltpu.roll(z, 1, 0))
    prev_ref[...] = z[tc - 1:tc, :]
    zs = z + (zp - z) * mu_ref[...]
    r = zs[:, 0:C_A]
    k = zs[:, C_A:2 * C_A]
    v = zs[:, 2 * C_A:3 * C_A]
    lr = zs[:, 3 * C_A:3 * C_A + LORA_PAD]

    w0 = vec_ref[0:1, :]
    a0 = vec_ref[1:2, :]
    v0 = vec_ref[2:3, :]
    k_k = vec_ref[3:4, :]
    k_a = vec_ref[4:5, :]
    r_k = vec_ref[5:6, :]
    lnx_g = vec_ref[6:7, :]
    lnx_b = vec_ref[7:8, :]

    lr_b = lr.astype(BF16)
    w_lin = _dot(jnp.tanh(lr).astype(BF16), uw_ref[...])
    a_lin = _dot(lr_b, ua_ref[...])
    g = _dot(jax.nn.sigmoid(lr).astype(BF16), ug_ref[...])
    w = -jax.nn.softplus(-(w0 + w_lin)) - 0.5
    ld = -jnp.exp(w)
    a_lr = jax.nn.sigmoid(a0 + a_lin)
    if has_v_res:
        v_lin = _dot(lr_b, uv_ref[...])
        v = v + (vf_ref[...] - v) * jax.nn.sigmoid(v0 + v_lin)
    vout_ref[...] = v

    li = lax.broadcasted_iota(jnp.int32, (LANES, LANES), 0)
    lj = lax.broadcasted_iota(jnp.int32, (LANES, LANES), 1)
    same_head = (li // N_A) == (lj // N_A)
    ones_bd = jnp.where(same_head, 1.0, 0.0).astype(BF16)

    kk = k * k_k
    kk = kk / jnp.maximum(jnp.sqrt(_head_sum(kk * kk, ones_bd)), L2_EPS)
    k = k * (1.0 + (a_lr - 1.0) * k_a)
    a_s = -kk
    b_s = kk * a_lr

    ti = lax.broadcasted_iota(jnp.int32, (tc, tc), 0)
    tj = lax.broadcasted_iota(jnp.int32, (tc, tc), 1)
    tri = jnp.where(ti >= tj, 1.0, 0.0).astype(BF16)
    ld_h, ld_m, ld_l = _split3(ld)
    cum = _dot(tri, ld_h) + _dot(tri, ld_m) + _dot(tri, ld_l)
    cum_end = cum[tc - 1:tc, :]
    p_inc = jnp.exp(cum)
    p_exc = jnp.exp(cum - ld)
    p_inv = jnp.exp(-cum)
    p_end = jnp.exp(cum_end - cum)
    decay_end = jnp.exp(cum_end)

    a_t = a_s * p_exc
    b_t = b_s * p_inv
    k_t = k * p_inv
    r_t = r * p_inc
    b_h = b_s * p_end
    k_h = k * p_end

    first_head = lax.broadcasted_iota(jnp.int32, (tc, LANES), 1) < N_A
    strict = same_head & ((li % N_A) > (lj % N_A))
    incl = same_head & ((li % N_A) >= (lj % N_A))
    eye = li == lj

    def stack(x):
        return jnp.concatenate([jnp.where(first_head, x, 0.0), jnp.where(first_head, 0.0, x)], axis=0)

    def dup(x):
        return jnp.concatenate([x, x], axis=0)

    pairs = range(C_A // LANES)
    sls = [slice(p * LANES, (p + 1) * LANES) for p in pairs]
    n2 = 2 * tc
    a_st = [stack(a_t[:, sl]) for sl in sls]
    r_st = [stack(r_t[:, sl]) for sl in sls]
    v_st = [stack(v[:, sl]).astype(BF16) for sl in sls]
    bh_st = [stack(b_h[:, sl]).astype(BF16) for sl in sls]
    kh_st = [stack(k_h[:, sl]).astype(BF16) for sl in sls]
    lhs = [jnp.concatenate([a_st[p], r_st[p]], axis=0).astype(BF16) for p in pairs]
    rhs = [jnp.concatenate([dup(b_t[:, sl]), dup(k_t[:, sl])], axis=0).astype(BF16) for sl in sls]
    x = [_dot(lhs[p], rhs[p], NT) for p in pairs]
    lk = [jnp.where(strict, x[p][0:n2, 0:n2], 0.0) for p in pairs]
    a_ak = [jnp.where(strict, x[p][0:n2, n2:2 * n2], 0.0).astype(BF16) for p in pairs]
    a_rb = [jnp.where(incl, x[p][n2:2 * n2, 0:n2], 0.0).astype(BF16) for p in pairs]
    a_rk = [jnp.where(incl, x[p][n2:2 * n2, n2:2 * n2], 0.0).astype(BF16) for p in pairs]
    zz = [jnp.concatenate([a_st[p], _dot(a_ak[p], v_st[p])], axis=1) for p in pairs]
    n_iter = int(math.log2(tc))
    for it in range(n_iter):
        lk_b = [lk[p].astype(BF16) for p in pairs]
        zz = [zz[p] + _dot(lk_b[p], zz[p].astype(BF16)) for p in pairs]
        if it < n_iter - 1:
            lk = [_dot(lk_b[p], lk_b[p]) for p in pairs]
    zz_b = [zz[p].astype(BF16) for p in pairs]
    qy = [_dot(a_rb[p], zz_b[p]) for p in pairs]
    y3 = [qy[p][:, LANES:] + _dot(a_rk[p], v_st[p]) for p in pairs]
    mg = [_dot(bh_st[p], zz_b[p], TN) for p in pairs]
    g_mat = [mg[p][:, LANES:] + _dot(kh_st[p], v_st[p], TN) for p in pairs]
    qm = [jnp.concatenate([r_st[p] + qy[p][:, 0:LANES],
                           jnp.where(eye, decay_end[:, sls[p]], 0.0) + mg[p][:, 0:LANES]], axis=0).astype(BF16)
          for p in pairs]
    h_split = [_split2(h_ref[p]) for p in pairs]
    out = [_dot(qm[p], h_split[p][0]) + _dot(qm[p], h_split[p][1]) for p in pairs]
    y_parts = []
    for p in pairs:
        y_st = out[p][0:n2, :] + y3[p]
        y_parts.append(y_st[0:tc, :] + y_st[tc:n2, :])
        h_ref[p] = out[p][n2:, :] + g_mat[p]
    y = jnp.concatenate(y_parts, axis=1)

    inv_n = 1.0 / N_A
    mean = _head_sum(y, ones_bd) * inv_n
    d = y - mean
    var = _head_sum(d * d, ones_bd) * inv_n
    yn = d * lax.rsqrt(var + GN_EPS) * lnx_g + lnx_b
    bonus = _head_sum(r * k * r_k, ones_bd) * v
    y_ref[...] = ((yn + bonus) * g).astype(y_ref.dtype)


def rwkv_mix(proj_a, v_first, mu, vec, uw, ua, ug, uv, batch, seq, has_v_res):
    tc = RWKV_CHUNK
    nc = seq // tc
    row_spec = lambda w: pl.BlockSpec((tc, w), lambda b, c: (b * nc + c, 0))
    full = lambda s: pl.BlockSpec(s, lambda b, c: tuple(0 for _ in s))
    m = batch * seq
    return pl.pallas_call(
        functools.partial(_rwkv_kernel, has_v_res=has_v_res),
        grid=(batch, nc),
        in_specs=[row_spec(N_A_COLS), row_spec(C_A), full((1, N_A_COLS)), full((8, C_A)),
                  full((LORA_PAD, C_A)), full((LORA_PAD, C_A)), full((LORA_PAD, C_A)), full((LORA_PAD, C_A))],
        out_specs=[row_spec(C_A), row_spec(C_A)],
        out_shape=[jax.ShapeDtypeStruct((m, C_A), BF16), jax.ShapeDtypeStruct((m, C_A), F32)],
        scratch_shapes=[pltpu.VMEM((1, N_A_COLS), F32), pltpu.VMEM((C_A // LANES, LANES, LANES), F32)],
        compiler_params=_cparams("parallel", "arbitrary"),
        name="rwkv_mix",
    )(proj_a, v_first, mu, vec, uw, ua, ug, uv)


def _gelu(x):
    return 0.5 * x * (1.0 + lax.erf(x * math.sqrt(0.5)))


def _gmlp_kernel(z_ref, lng_ref, lnb_ref, w_ref, bs_ref, og_ref, o_ref):
    u = _gelu(z_ref[:, 0:C_B])
    v = _gelu(z_ref[:, C_B:2 * C_B])
    mu = jnp.mean(v, axis=-1, keepdims=True)
    d = v - mu
    var = jnp.mean(d * d, axis=-1, keepdims=True)
    vn = d * lax.rsqrt(var + LN_EPS) * lng_ref[...] + lnb_ref[...]
    ti = lax.broadcasted_iota(jnp.int32, (CHUNK, CHUNK), 0)
    tj = lax.broadcasted_iota(jnp.int32, (CHUNK, CHUNK), 1)
    causal = ti >= tj
    dh = C_B // H_B
    for h in range(H_B):
        sl = slice(h * dh, (h + 1) * dh)
        w = jnp.where(causal, w_ref[h], 0.0).astype(BF16)
        s = _dot(w, vn[:, sl].astype(BF16)) + bs_ref[:, h:h + 1]
        y = u[:, sl] * s
        ms = jnp.mean(y * y, axis=-1, keepdims=True)
        o_ref[:, sl] = (y * lax.rsqrt(ms + RMS_EPS) * og_ref[:, sl]).astype(o_ref.dtype)


def gmlp_mix(proj_bd, ln_g, ln_b, w_s, b_s_t, out_g):
    m = proj_bd.shape[0]
    full = lambda s: pl.BlockSpec(s, lambda i: tuple(0 for _ in s))
    return pl.pallas_call(
        _gmlp_kernel,
        grid=(m // CHUNK,),
        in_specs=[pl.BlockSpec((CHUNK, 2 * C_B), lambda i: (i, 0)),
                  full((1, C_B)), full((1, C_B)), full((H_B, CHUNK, CHUNK)), full((CHUNK, H_B)), full((1, C_B))],
        out_specs=pl.BlockSpec((CHUNK, C_B), lambda i: (i, 0)),
        out_shape=jax.ShapeDtypeStruct((m, C_B), BF16),
        compiler_params=_cparams("parallel"),
        name="gmlp_mix",
    )(proj_bd, ln_g, ln_b, w_s, b_s_t, out_g)


def _sb_kernel(q_ref, k_ref, v_ref, g_ref, o_ref, *, tq, scale):
    qi = pl.program_id(2)
    q = q_ref[...]
    row = lax.broadcasted_iota(jnp.int32, (tq, tq), 0)
    col = lax.broadcasted_iota(jnp.int32, (tq, tq), 1)
    below = col < row
    upper = jnp.where(row > col, 1.0, 0.0).astype(BF16)

    def block(j, carry, acc, keep):
        start = pl.multiple_of(j * tq, tq)
        kb = k_ref[pl.ds(start, tq), :]
        vb = v_ref[pl.ds(start, tq), :]
        z = _dot(q, kb, NT) * scale
        log_beta = jnp.minimum(z, 0.0) - jnp.log(1.0 + jnp.exp(-jnp.abs(z)))
        log_1mb = log_beta - z
        if keep is not None:
            log_1mb = jnp.where(keep, log_1mb, 0.0)
        hi, lo = _split2(log_1mb)
        after = _dot(hi, upper) + _dot(lo, upper)
        att = jnp.exp(log_beta + after + carry)
        if keep is not None:
            att = jnp.where(keep, att, 0.0)
        acc = acc + _dot(att.astype(BF16), vb)
        carry = carry + after[:, 0:1] + log_1mb[:, 0:1]
        return carry, acc

    carry0 = jnp.zeros((tq, 1), F32)
    acc0 = jnp.zeros((tq, D_C), F32)
    carry, acc = block(qi, carry0, acc0, below)
    has_prev = jnp.broadcast_to(qi >= 1, (tq, tq))
    carry, acc = block(jnp.maximum(qi - 1, 0), carry, acc, has_prev)

    def more(state):
        return jnp.logical_and(state[0] >= 0, jnp.max(state[1]) > SB_CUTOFF)

    def body(state):
        carry_j, acc_j = block(state[0], state[1], state[2], None)
        return state[0] - 1, carry_j, acc_j

    _, carry, acc = lax.while_loop(more, body, (qi - 2, carry, acc))
    ms = jnp.mean(acc * acc, axis=-1, keepdims=True)
    o_ref[...] = (acc * lax.rsqrt(ms + RMS_EPS) * g_ref[...]).astype(o_ref.dtype)


def sb_attention(proj_cd, out_g, batch, seq, tq=256):
    nq = seq // tq
    return pl.pallas_call(
        functools.partial(_sb_kernel, tq=tq, scale=D_C ** -0.5),
        grid=(batch, H_C, nq),
        in_specs=[pl.BlockSpec((None, tq, D_C), lambda b, h, i: (b, i, h)),
                  pl.BlockSpec((None, seq, D_C), lambda b, h, i: (b, 0, H_C + h)),
                  pl.BlockSpec((None, seq, D_C), lambda b, h, i: (b, 0, 2 * H_C + h)),
                  pl.BlockSpec((1, D_C), lambda b, h, i: (0, h))],
        out_specs=pl.BlockSpec((None, tq, D_C), lambda b, h, i: (b, i, h)),
        out_shape=jax.ShapeDtypeStruct((batch, seq, C_C), BF16),
        compiler_params=_cparams("parallel", "parallel", "arbitrary"),
        name="sb_attention",
    )(proj_cd, proj_cd, proj_cd, out_g)


def _rope_table_kernel(pos_ref, invf_ref, cos_ref, sin_ref):
    ang = pos_ref[...].astype(F32) * invf_ref[...]
    cos_ref[...] = jnp.cos(ang)
    sin_ref[...] = jnp.sin(ang)


def rope_tables(pos_col, invf, tm=512):
    m = pos_col.shape[0]
    spec = pl.BlockSpec((tm, LANES), lambda i: (i, 0))
    return pl.pallas_call(
        _rope_table_kernel,
        grid=(m // tm,),
        in_specs=[pl.BlockSpec((tm, 1), lambda i: (i, 0)), pl.BlockSpec((1, LANES), lambda i: (0, 0))],
        out_specs=[spec, spec],
        out_shape=[jax.ShapeDtypeStruct((m, LANES), F32)] * 2,
        compiler_params=_cparams("parallel"),
        name="rope_tables",
    )(pos_col, invf)


def _rope_kernel(x_ref, cos_ref, sin_ref, o_ref):
    x = x_ref[...]
    width = x.shape[1]
    reps = width // LANES
    c = jnp.concatenate([cos_ref[...]] * reps, axis=1)
    s = jnp.concatenate([sin_ref[...]] * reps, axis=1)
    lane = lax.broadcasted_iota(jnp.int32, x.shape, 1)
    first_half = (lane % DH_D) < (DH_D // 2)
    rot = jnp.where(first_half, -pltpu.roll(x, width - DH_D // 2, 1), pltpu.roll(x, DH_D // 2, 1))
    o_ref[...] = (x * c + rot * s).astype(o_ref.dtype)


def rope_qk(proj_bd, cos_t, sin_t, tm=256):
    m = proj_bd.shape[0]
    width = 2 * C_D
    return pl.pallas_call(
        _rope_kernel,
        grid=(m // tm,),
        in_specs=[pl.BlockSpec((tm, width), lambda i: (i, 1)),
                  pl.BlockSpec((tm, LANES), lambda i: (i, 0)),
                  pl.BlockSpec((tm, LANES), lambda i: (i, 0))],
        out_specs=pl.BlockSpec((tm, width), lambda i: (i, 0)),
        out_shape=jax.ShapeDtypeStruct((m, width), BF16),
        compiler_params=_cparams("parallel"),
        name="rope_qk",
    )(proj_bd, cos_t, sin_t)


def _diff_kernel(lam_ref, q_ref, k_ref, v_ref, g_ref, o_ref, *, tq, lam_init):
    qi = pl.program_id(2)
    q = (q_ref[...].astype(F32) * (DH_D ** -0.5)).astype(BF16)
    lane = lax.broadcasted_iota(jnp.int32, q.shape, 1)
    zero = jnp.zeros_like(q)
    qs = jnp.concatenate([jnp.where(lane < DH_D, q, zero), jnp.where(lane < DH_D, zero, q)], axis=0)
    row = lax.broadcasted_iota(jnp.int32, (2 * tq, tq), 0) % tq
    col = lax.broadcasted_iota(jnp.int32, (2 * tq, tq), 1)
    visible = col <= row

    def block(j, m, l, acc, diagonal):
        start = pl.multiple_of(j * tq, tq)
        kb = k_ref[pl.ds(start, tq), :]
        vb = v_ref[pl.ds(start, tq), :]
        s = _dot(qs, kb, NT)
        if diagonal:
            s = jnp.where(visible, s, -jnp.inf)
        m_new = jnp.maximum(m, jnp.max(s, axis=-1, keepdims=True))
        alpha = jnp.exp(m - m_new)
        p = jnp.exp(s - m_new)
        l = alpha * l + jnp.sum(p, axis=-1, keepdims=True)
        acc = alpha * acc + _dot(p.astype(BF16), vb)
        return m_new, l, acc

    m0 = jnp.full((2 * tq, 1), -jnp.inf, F32)
    l0 = jnp.zeros((2 * tq, 1), F32)
    acc0 = jnp.zeros((2 * tq, DV_D), F32)
    m, l, acc = block(qi, m0, l0, acc0, True)

    def grouped(first, group):
        def body(i, state):
            for t in range(group):
                state = block(first + i * group + t, state[0], state[1], state[2], False)
            return state
        return body

    n4 = qi // 4
    n2 = (qi - 4 * n4) // 2
    n1 = qi - 4 * n4 - 2 * n2
    state = lax.fori_loop(0, n4, grouped(0, 4), (m, l, acc))
    state = lax.fori_loop(0, n2, grouped(4 * n4, 2), state)
    m, l, acc = lax.fori_loop(0, n1, grouped(4 * n4 + 2 * n2, 1), state)
    lv = lam_ref[...]
    lam = (jnp.exp(jnp.sum(lv[0:1, :] * lv[1:2, :], axis=-1, keepdims=True))
           - jnp.exp(jnp.sum(lv[2:3, :] * lv[3:4, :], axis=-1, keepdims=True)) + lam_init)
    o = acc / l
    y = o[0:tq, :] - lam * o[tq:2 * tq, :]
    ms = jnp.mean(y * y, axis=-1, keepdims=True)
    o_ref[...] = (y * lax.rsqrt(ms + LN_EPS) * g_ref[...] * (1.0 - lam_init)).astype(o_ref.dtype)


def diff_attention(lam_p, qk_rot, proj_cd, subln, batch, seq, lam_init, tq=512):
    nq = seq // tq
    return pl.pallas_call(
        functools.partial(_diff_kernel, tq=tq, lam_init=lam_init),
        grid=(batch, H_D, nq),
        in_specs=[pl.BlockSpec((4, DH_D), lambda b, h, i: (0, 0)),
                  pl.BlockSpec((None, tq, DV_D), lambda b, h, i: (b, i, h)),
                  pl.BlockSpec((None, seq, DV_D), lambda b, h, i: (b, 0, H_D + h)),
                  pl.BlockSpec((None, seq, DV_D), lambda b, h, i: (b, 0, 3 * H_C + h)),
                  pl.BlockSpec((1, DV_D), lambda b, h, i: (0, 0))],
        out_specs=pl.BlockSpec((None, tq, DV_D), lambda b, h, i: (b, i, h)),
        out_shape=jax.ShapeDtypeStruct((batch, seq, C_D), BF16),
        compiler_params=_cparams("parallel", "parallel", "arbitrary"),
        name="diff_attention",
    )(lam_p, qk_rot, qk_rot, proj_cd, subln)


def _router_kernel(h_ref, g_ref, wr_ref, n_ref, comb_ref, idx_ref):
    x = h_ref[...]
    ms = jnp.mean(x * x, axis=-1, keepdims=True)
    n = x * lax.rsqrt(ms + RMS_EPS) * g_ref[...]
    n_ref[...] = n.astype(n_ref.dtype)
    n_hi, n_lo = _split2(n)
    w_hi, w_lo = _split2(wr_ref[...])
    logits = _dot(n_hi, w_hi) + _dot(n_lo, w_hi) + _dot(n_hi, w_lo)
    lane = lax.broadcasted_iota(jnp.int32, logits.shape, 1)
    neg = -jnp.inf
    logits = jnp.where(lane < N_EXPERTS, logits, neg)
    m1 = jnp.max(logits, axis=-1, keepdims=True)
    i1 = jnp.min(jnp.where(logits == m1, lane, LANES), axis=-1, keepdims=True)
    rest = jnp.where(lane == i1, neg, logits)
    m2 = jnp.max(rest, axis=-1, keepdims=True)
    i2 = jnp.min(jnp.where(rest == m2, lane, LANES), axis=-1, keepdims=True)
    e = jnp.exp(m2 - m1)
    g1 = 1.0 / (1.0 + e)
    g2 = e / (1.0 + e)
    comb_ref[...] = jnp.where(lane == i1, g1, 0.0) + jnp.where(lane == i2, g2, 0.0)
    idx_ref[...] = jnp.where(lane == 0, i1, jnp.where(lane == 1, i2, 0))


def moe_router(h, g, w_router_pad, tm=256):
    m, d = h.shape
    spec = pl.BlockSpec((tm, LANES), lambda i: (i, 0))
    return pl.pallas_call(
        _router_kernel,
        grid=(m // tm,),
        in_specs=[pl.BlockSpec((tm, d), lambda i: (i, 0)), pl.BlockSpec((1, d), lambda i: (0, 0)),
                  pl.BlockSpec((d, LANES), lambda i: (0, 0))],
        out_specs=[pl.BlockSpec((tm, d), lambda i: (i, 0)), spec, spec],
        out_shape=[jax.ShapeDtypeStruct((m, d), BF16), jax.ShapeDtypeStruct((m, LANES), F32),
                   jax.ShapeDtypeStruct((m, LANES), jnp.int32)],
        compiler_params=_cparams("parallel"),
        name="moe_router",
    )(h, g, w_router_pad)


def _tile_state(te_ref):
    i = pl.program_id(1)
    in_use = i < te_ref[pl.num_programs(1)]
    new_weights = jnp.logical_or(i == 0, te_ref[i] != te_ref[jnp.maximum(i - 1, 0)])
    return in_use, new_weights


def _gmm_up_kernel(te_ref, a_ref, wg_ref, wu_ref, o_ref, wg_bf, wu_bf):
    in_use, new_weights = _tile_state(te_ref)

    @pl.when(new_weights)
    def _():
        wg_bf[...] = wg_ref[...].astype(BF16)
        wu_bf[...] = wu_ref[...].astype(BF16)

    @pl.when(in_use)
    def _():
        a = a_ref[...]
        g = _dot(a, wg_bf[...])
        u = _dot(a, wu_bf[...])
        o_ref[...] = (g * jax.nn.sigmoid(g) * u).astype(o_ref.dtype)

    @pl.when(jnp.logical_not(in_use))
    def _():
        o_ref[...] = jnp.zeros_like(o_ref)


def _row_spec(tm, width, n_tiles):
    return pl.BlockSpec((tm, width), lambda j, i, te: (jnp.minimum(i, te[n_tiles] - 1), 0))


def gmm_swiglu_up(tile_table, xs, wg, wu, tm, tn):
    s, k = xs.shape
    n = wg.shape[2]
    n_tiles = s // tm
    w_spec = pl.BlockSpec((None, k, tn), lambda j, i, te: (te[i], 0, j))
    return pl.pallas_call(
        _gmm_up_kernel,
        grid_spec=pltpu.PrefetchScalarGridSpec(
            num_scalar_prefetch=1,
            grid=(n // tn, n_tiles),
            in_specs=[_row_spec(tm, k, n_tiles), w_spec, w_spec],
            out_specs=pl.BlockSpec((tm, tn), lambda j, i, te: (i, j)),
            scratch_shapes=[pltpu.VMEM((k, tn), BF16), pltpu.VMEM((k, tn), BF16)],
        ),
        out_shape=jax.ShapeDtypeStruct((s, n), BF16),
        compiler_params=_cparams("arbitrary", "arbitrary"),
        name="gmm_swiglu_up",
    )(tile_table, xs, wg, wu)


def _gmm_down_kernel(te_ref, a_ref, w_ref, gate_ref, o_ref, w_bf):
    in_use, new_weights = _tile_state(te_ref)

    @pl.when(new_weights)
    def _():
        w_bf[...] = w_ref[...].astype(BF16)

    @pl.when(in_use)
    def _():
        o_ref[...] = _dot(a_ref[...], w_bf[...]) * gate_ref[...]

    @pl.when(jnp.logical_not(in_use))
    def _():
        o_ref[...] = jnp.zeros_like(o_ref)


def gmm_down(tile_table, hs, wd, slot_gate, tm, tn=512):
    s, k = hs.shape
    n = wd.shape[2]
    n_tiles = s // tm
    return pl.pallas_call(
        _gmm_down_kernel,
        grid_spec=pltpu.PrefetchScalarGridSpec(
            num_scalar_prefetch=1,
            grid=(n // tn, n_tiles),
            in_specs=[_row_spec(tm, k, n_tiles),
                      pl.BlockSpec((None, k, tn), lambda j, i, te: (te[i], 0, j)),
                      _row_spec(tm, 1, n_tiles)],
            out_specs=pl.BlockSpec((tm, tn), lambda j, i, te: (i, j)),
            scratch_shapes=[pltpu.VMEM((k, tn), BF16)],
        ),
        out_shape=jax.ShapeDtypeStruct((s, n), F32),
        compiler_params=_cparams("arbitrary", "arbitrary"),
        name="gmm_down",
    )(tile_table, hs, wd, slot_gate)


def moe_layer(h, ffn_g, w_router, wg, wu, wd, tm=MOE_TM):
    m, d = h.shape
    w_router_pad = jnp.pad(w_router.astype(F32), ((0, 0), (0, LANES - N_EXPERTS)))
    n_b, comb, idx = moe_router(h, ffn_g.reshape(1, d).astype(F32), w_router_pad)
    top_idx = idx[:, 0:2]
    gates = jnp.take_along_axis(comb[:, 0:N_EXPERTS], top_idx, axis=1)

    flat_e = top_idx.reshape(-1)
    n_pairs = 2 * m
    onehot = (flat_e[:, None] == jnp.arange(N_EXPERTS, dtype=jnp.int32)[None, :]).astype(jnp.int32)
    counts = jnp.sum(onehot, axis=0)
    rank = jnp.sum((jnp.cumsum(onehot, axis=0) - onehot) * onehot, axis=1)
    padded = ((counts + tm - 1) // tm) * tm
    pad_off = jnp.cumsum(padded) - padded
    raw_off = jnp.cumsum(counts) - counts
    dest = pad_off[flat_e] + rank
    n_tiles = n_pairs // tm + N_EXPERTS
    n_slots = n_tiles * tm
    order = jnp.argsort(flat_e, stable=True).astype(jnp.int32)
    tile_start = jnp.arange(n_tiles, dtype=jnp.int32) * tm
    pad_end = pad_off + padded
    tile_expert = jnp.minimum(jnp.sum((tile_start[:, None] >= pad_end[None, :]).astype(jnp.int32), axis=1),
                              N_EXPERTS - 1).astype(jnp.int32)
    slot = jnp.arange(n_slots, dtype=jnp.int32)
    slot_e = jnp.repeat(tile_expert, tm)
    local = slot - pad_off[slot_e]
    valid = (local < counts[slot_e]) & (slot < pad_end[N_EXPERTS - 1])
    src_pair = order[jnp.clip(raw_off[slot_e] + local, 0, n_pairs - 1)]
    slot_token = jnp.where(valid, src_pair // 2, 0)
    slot_gate = jnp.where(valid, gates.reshape(-1)[src_pair], 0.0).astype(F32).reshape(n_slots, 1)

    tiles_in_use = (pad_end[N_EXPERTS - 1] // tm).astype(jnp.int32).reshape(1)
    tile_table = jnp.concatenate([tile_expert, tiles_in_use])

    xs = jnp.take(n_b, slot_token, axis=0)
    hs = gmm_swiglu_up(tile_table, xs, wg, wu, tm, min(MOE_TN, wg.shape[2]))
    ys = gmm_down(tile_table, hs, wd, slot_gate, tm)
    dest2 = dest.reshape(m, 2)
    return h + jnp.take(ys, dest2[:, 0], axis=0) + jnp.take(ys, dest2[:, 1], axis=0)


def _rwkv_columns(w_in, layer):
    d = w_in.shape[0]
    o_w = C_A
    o_k = o_w + W_LORA
    o_v = o_k + C_A
    o_a = o_v + C_A
    o_g = o_a + A_LORA
    parts = [w_in[:, 0:C_A], w_in[:, o_k:o_k + C_A], w_in[:, o_v:o_v + C_A],
             w_in[:, o_w:o_w + W_LORA], w_in[:, o_a:o_a + A_LORA], w_in[:, o_g:o_g + G_LORA]]
    used = W_LORA + A_LORA + G_LORA
    if layer > 0:
        parts.append(w_in[:, N_IN0:N_IN0 + V_LORA])
        used += V_LORA
    parts.append(jnp.zeros((d, LORA_PAD - used), w_in.dtype))
    return jnp.concatenate(parts, axis=1)


def _rwkv_mu(mu, mu_v):
    o_w = C_A
    o_k = o_w + W_LORA
    o_v = o_k + C_A
    o_a = o_v + C_A
    o_g = o_a + A_LORA
    parts = [mu[0:C_A], mu[o_k:o_k + C_A], mu[o_v:o_v + C_A],
             mu[o_w:o_w + W_LORA], mu[o_a:o_a + A_LORA], mu[o_g:o_g + G_LORA]]
    used = W_LORA + A_LORA + G_LORA
    if mu_v is not None:
        parts.append(mu_v)
        used += V_LORA
    parts.append(jnp.zeros((LORA_PAD - used,), mu.dtype))
    return jnp.concatenate(parts).reshape(1, N_A_COLS).astype(F32)


def _pad_rows(w, offset):
    return jnp.pad(w, ((offset, LORA_PAD - offset - w.shape[0]), (0, 0))).astype(BF16)


def kernel(x, p, positions, attn_norm, w_in0, w_in_rest, w_out, rwkv_mu, rwkv_mu_v, rwkv_w0, rwkv_w_up, rwkv_a0, rwkv_a_up, rwkv_v0, rwkv_v_up, rwkv_g_up, rwkv_k_k, rwkv_k_a, rwkv_r_k, rwkv_lnx_g, rwkv_lnx_b, gm_ln_g, gm_ln_b, gm_w_s, gm_b_s, gm_out_g, sb_out_g, diff_lambda, diff_subln, ffn_norm, dense_w_gate, dense_w_up, dense_w_down, moe_router, moe_w_gate, moe_w_up, moe_w_down, ple_norm, ple_w_gate, ple_w_proj, final_norm):
    batch, seq, d = x.shape
    m = batch * seq
    h = x.reshape(m, d).astype(F32)

    inv_freq = 1.0 / (ROPE_THETA ** (jnp.arange(0, DH_D, 2, dtype=F32) / DH_D))
    invf = jnp.tile(inv_freq, LANES // (DH_D // 2)).reshape(1, LANES)
    cos_t, sin_t = rope_tables(positions.reshape(m, 1), invf)

    v_first = None
    for i in range(DEPTH):
        w_in = w_in0 if i == 0 else w_in_rest[i - 1]
        o_b = N_RWKV
        o_c = o_b + N_GMLP
        o_d = o_c + N_SB
        w_a = _rwkv_columns(w_in, i).astype(BF16)
        w_bd = jnp.concatenate([w_in[:, o_b:o_c], w_in[:, o_d:o_d + 2 * C_D]], axis=1).astype(BF16)
        w_cd = jnp.concatenate([w_in[:, o_c:o_d], w_in[:, o_d + 2 * C_D:o_d + 3 * C_D]], axis=1).astype(BF16)

        n = rmsnorm(h, attn_norm[i], BF16)
        proj_a = matmul(n, w_a, F32)
        proj_bd = matmul(n, w_bd, F32)
        proj_cd = matmul(n, w_cd, BF16)

        mu = _rwkv_mu(rwkv_mu[i], rwkv_mu_v[i - 1] if i > 0 else None)
        zeros_c = jnp.zeros((C_A,), F32)
        vec = jnp.stack([rwkv_w0[i], rwkv_a0[i], rwkv_v0[i - 1] if i > 0 else zeros_c, rwkv_k_k[i], rwkv_k_a[i],
                         rwkv_r_k[i].reshape(C_A), rwkv_lnx_g[i], rwkv_lnx_b[i]]).astype(F32)
        uw = _pad_rows(rwkv_w_up[i], 0)
        ua = _pad_rows(rwkv_a_up[i], W_LORA)
        ug = _pad_rows(rwkv_g_up[i], W_LORA + A_LORA)
        uv = _pad_rows(rwkv_v_up[i - 1], W_LORA + A_LORA + G_LORA) if i > 0 else jnp.zeros((LORA_PAD, C_A), BF16)
        vf_in = v_first if i > 0 else proj_a
        y_a, v_out = rwkv_mix(proj_a, vf_in, mu, vec, uw, ua, ug, uv, batch, seq, has_v_res=i > 0)
        if i == 0:
            v_first = v_out

        y_b = gmlp_mix(proj_bd, gm_ln_g[i].reshape(1, C_B), gm_ln_b[i].reshape(1, C_B), gm_w_s[i],
                       gm_b_s[i].T, gm_out_g[i].reshape(1, C_B))

        proj_cd3 = proj_cd.reshape(batch, seq, 4 * C_C)
        y_c = sb_attention(proj_cd3, sb_out_g[i].reshape(1, C_C), batch, seq).reshape(m, C_C)

        qk_rot = rope_qk(proj_bd, cos_t, sin_t).reshape(batch, seq, 2 * C_D)
        lam_init = 0.8 - 0.6 * math.exp(-0.3 * i)
        y_d = diff_attention(diff_lambda[i].astype(F32), qk_rot, proj_cd3, diff_subln[i].reshape(1, DV_D),
                             batch, seq, lam_init).reshape(m, C_D)

        h = mix_out_proj((y_a, y_b, y_c, y_d), w_out[i].astype(BF16), h)

        j = i // 2
        if i % 2 == 0:
            n = rmsnorm(h, ffn_norm[i], BF16)
            n_row_tiles = m // DENSE_TM
            one_group = jnp.concatenate([jnp.zeros((n_row_tiles,), jnp.int32), jnp.full((1,), n_row_tiles, jnp.int32)])
            hid = gmm_swiglu_up(one_group, n, dense_w_gate[j][None], dense_w_up[j][None], DENSE_TM, DENSE_TN)
            h = down_proj_residual(hid, dense_w_down[j].astype(BF16), h)
        else:
            h = moe_layer(h, ffn_norm[i], moe_router[j], moe_w_gate[j], moe_w_up[j], moe_w_down[j])

        n = rmsnorm(h, ple_norm[i], BF16)
        h = ple_residual(n, ple_w_gate[i].astype(BF16), p[i].reshape(m, P_DIM).astype(BF16),
                         ple_w_proj[i].astype(BF16), h)

    return rmsnorm(h, final_norm, x.dtype).reshape(batch, seq, d)
```

```python
import functools
import math

import jax
import jax.numpy as jnp
import numpy as np
from jax import lax
from jax.experimental import pallas as pl
from jax.experimental.pallas import tpu as pltpu

F32 = jnp.float32
BF16 = jnp.bfloat16

D_MODEL = 4096
DEPTH = 2
C_A = 1024
N_A = 64
W_LORA, A_LORA, V_LORA, G_LORA = 64, 64, 32, 160
C_B = 1024
H_B = 8
CHUNK = 128
C_C = 1024
H_C = 8
D_C = 128
C_D = 1024
H_D = 8
DV_D = 128
DH_D = 64
D_FF = 11008
N_EXPERTS = 8
D_FF_EXPERT = 5632
P_DIM = 256
ROPE_THETA = 10000.0
RMS_EPS = 1e-6
LN_EPS = 1e-5
GN_EPS = 64e-5
L2_EPS = 1e-12
N_RWKV = 3 * C_A + W_LORA + A_LORA + G_LORA
N_GMLP = 2 * C_B
N_SB = 3 * C_C
N_DIFF = 3 * C_D
N_IN0 = N_RWKV + N_GMLP + N_SB + N_DIFF

LANES = 128
VMEM_LIMIT = 56 * 1024 * 1024

LORA_PAD = 512
N_A_COLS = 3 * C_A + LORA_PAD
DENSE_TM, DENSE_TN = 1024, 256
DENSE_TK = 5504
MOE_TM, MOE_TN = 512, 512
RWKV_CHUNK = 64
SB_CUTOFF = -110.0
SB_HEADS = 4

NT = (((1,), (1,)), ((), ()))
TN = (((0,), (0,)), ((), ()))


def _cparams(*sem):
    return pltpu.CompilerParams(dimension_semantics=sem, vmem_limit_bytes=VMEM_LIMIT)


def _dot(a, b, dims=None):
    if dims is None:
        return jnp.dot(a, b, preferred_element_type=F32)
    return lax.dot_general(a, b, dims, preferred_element_type=F32)


def _split2(x):
    hi = x.astype(BF16)
    lo = (x - hi.astype(F32)).astype(BF16)
    return hi, lo


def _split3(x):
    hi = x.astype(BF16)
    r = x - hi.astype(F32)
    mid = r.astype(BF16)
    lo = (r - mid.astype(F32)).astype(BF16)
    return hi, mid, lo


def _pick_tile(n, prefs):
    for t in prefs:
        if n % t == 0:
            return t
    raise ValueError(f"no tile for {n}")


def _rmsnorm_kernel(x_ref, g_ref, o_ref, *, eps):
    x = x_ref[...].astype(F32)
    ms = jnp.mean(x * x, axis=-1, keepdims=True)
    o_ref[...] = (x * lax.rsqrt(ms + eps) * g_ref[...]).astype(o_ref.dtype)


def rmsnorm(x, g, out_dtype, tm=512):
    m, d = x.shape
    return pl.pallas_call(
        functools.partial(_rmsnorm_kernel, eps=RMS_EPS),
        grid=(m // tm,),
        in_specs=[pl.BlockSpec((tm, d), lambda i: (i, 0)), pl.BlockSpec((1, d), lambda i: (0, 0))],
        out_specs=pl.BlockSpec((tm, d), lambda i: (i, 0)),
        out_shape=jax.ShapeDtypeStruct((m, d), out_dtype),
        compiler_params=_cparams("parallel"),
        name="rmsnorm",
    )(x, g.reshape(1, d).astype(F32))


def _mm_kernel(a_ref, b_ref, o_ref):
    o_ref[...] = _dot(a_ref[...], b_ref[...]).astype(o_ref.dtype)


def matmul(a, b, out_dtype, tm=1024):
    m, k = a.shape
    n = b.shape[1]
    tn = _pick_tile(n, (1024, 512, 256, 128))
    return pl.pallas_call(
        _mm_kernel,
        grid=(m // tm, n // tn),
        in_specs=[pl.BlockSpec((tm, k), lambda i, j: (i, 0)), pl.BlockSpec((k, tn), lambda i, j: (0, j))],
        out_specs=pl.BlockSpec((tm, tn), lambda i, j: (i, j)),
        out_shape=jax.ShapeDtypeStruct((m, n), out_dtype),
        compiler_params=_cparams("parallel", "arbitrary"),
        name="matmul",
    )(a, b)


def _mix_out_kernel(ya_ref, yb_ref, yc_ref, yd_ref, w_ref, res_ref, o_ref):
    kq = ya_ref.shape[1]
    acc = res_ref[...]
    for idx, y_ref in enumerate((ya_ref, yb_ref, yc_ref, yd_ref)):
        acc = acc + _dot(y_ref[...], w_ref[idx * kq:(idx + 1) * kq, :])
    o_ref[...] = acc


def mix_out_proj(ys, w, res, tm=1024, tn=512):
    m, kq = ys[0].shape
    k, n = w.shape
    y_spec = pl.BlockSpec((tm, kq), lambda i, j: (i, 0))
    return pl.pallas_call(
        _mix_out_kernel,
        grid=(m // tm, n // tn),
        in_specs=[y_spec, y_spec, y_spec, y_spec,
                  pl.BlockSpec((k, tn), lambda i, j: (0, j)),
                  pl.BlockSpec((tm, tn), lambda i, j: (i, j))],
        out_specs=pl.BlockSpec((tm, tn), lambda i, j: (i, j)),
        out_shape=jax.ShapeDtypeStruct((m, n), F32),
        compiler_params=_cparams("parallel", "arbitrary"),
        name="mix_out_proj",
    )(*ys, w, res)


def _down_res_kernel(a_ref, b_ref, res_ref, o_ref, acc_ref):
    kk = pl.program_id(2)

    @pl.when(kk == 0)
    def _():
        acc_ref[...] = res_ref[...]

    acc_ref[...] += _dot(a_ref[...], b_ref[...])

    @pl.when(kk == pl.num_programs(2) - 1)
    def _():
        o_ref[...] = acc_ref[...]


def down_proj_residual(a, b, res, tm=1024, tn=512, tk=DENSE_TK):
    m, k = a.shape
    n = b.shape[1]
    return pl.pallas_call(
        _down_res_kernel,
        grid=(m // tm, n // tn, k // tk),
        in_specs=[pl.BlockSpec((tm, tk), lambda i, j, kk: (i, kk)),
                  pl.BlockSpec((tk, tn), lambda i, j, kk: (kk, j)),
                  pl.BlockSpec((tm, tn), lambda i, j, kk: (i, j))],
        out_specs=pl.BlockSpec((tm, tn), lambda i, j, kk: (i, j)),
        out_shape=jax.ShapeDtypeStruct((m, n), F32),
        scratch_shapes=[pltpu.VMEM((tm, tn), F32)],
        compiler_params=_cparams("parallel", "parallel", "arbitrary"),
        name="down_proj_residual",
    )(a, b, res)


def _ple_kernel(n_ref, wg_ref, p_ref, wp_ref, res_ref, o_ref):
    gate = jax.nn.sigmoid(_dot(n_ref[...], wg_ref[...]))
    emb = _dot(p_ref[...], wp_ref[...])
    o_ref[...] = res_ref[...] + gate * emb


def ple_residual(n, wg, p, wp, res, tm=1024, tn=512):
    m, k = n.shape
    nn = wg.shape[1]
    pd = p.shape[1]
    return pl.pallas_call(
        _ple_kernel,
        grid=(m // tm, nn // tn),
        in_specs=[pl.BlockSpec((tm, k), lambda i, j: (i, 0)),
                  pl.BlockSpec((k, tn), lambda i, j: (0, j)),
                  pl.BlockSpec((tm, pd), lambda i, j: (i, 0)),
                  pl.BlockSpec((pd, tn), lambda i, j: (0, j)),
                  pl.BlockSpec((tm, tn), lambda i, j: (i, j))],
        out_specs=pl.BlockSpec((tm, tn), lambda i, j: (i, j)),
        out_shape=jax.ShapeDtypeStruct((m, nn), F32),
        compiler_params=_cparams("parallel", "arbitrary"),
        name="ple_residual",
    )(n, wg, p, wp, res)


def _head_sum(x, ones_bd, split=False):
    outs = []
    for p in range(x.shape[1] // LANES):
        xp = x[:, p * LANES:(p + 1) * LANES]
        if split:
            hi, lo = _split2(xp)
            outs.append(_dot(hi, ones_bd) + _dot(lo, ones_bd))
        else:
            outs.append(_dot(xp.astype(BF16), ones_bd))
    return jnp.concatenate(outs, axis=1)


def _rwkv_kernel(z_ref, vf_ref, mu_ref, vec_ref, uw_ref, ua_ref, ug_ref, uv_ref,
                 y_ref, vout_ref, prev_ref, h_ref, *, has_v_res):
    tc = RWKV_CHUNK
    c = pl.program_id(1)

    @pl.when(c == 0)
    def _():
        prev_ref[...] = jnp.zeros_like(prev_ref)
        h_ref[...] = jnp.zeros_like(h_ref)

    z = z_ref[...]
    row = lax.broadcasted_iota(jnp.int32, z.shape, 0)
    zp = jnp.where(row == 0, prev_ref[...], pltpu.roll(z, 1, 0))
    prev_ref[...] = z[tc - 1:tc, :]
    zs = z + (zp - z) * mu_ref[...]
    r = zs[:, 0:C_A]
    k = zs[:, C_A:2 * C_A]
    v = zs[:, 2 * C_A:3 * C_A]
    lr = zs[:, 3 * C_A:3 * C_A + LORA_PAD]

    w0 = vec_ref[0:1, :]
    a0 = vec_ref[1:2, :]
    v0 = vec_ref[2:3, :]
    k_k = vec_ref[3:4, :]
    k_a = vec_ref[4:5, :]
    r_k = vec_ref[5:6, :]
    lnx_g = vec_ref[6:7, :]
    lnx_b = vec_ref[7:8, :]

    lr_b = lr.astype(BF16)
    w_lin = _dot(jnp.tanh(lr).astype(BF16), uw_ref[...])
    a_lin = _dot(lr_b, ua_ref[...])
    g = _dot(jax.nn.sigmoid(lr).astype(BF16), ug_ref[...])
    w = -jax.nn.softplus(-(w0 + w_lin)) - 0.5
    ld = -jnp.exp(w)
    a_lr = jax.nn.sigmoid(a0 + a_lin)
    if has_v_res:
        v_lin = _dot(lr_b, uv_ref[...])
        v = v + (vf_ref[...] - v) * jax.nn.sigmoid(v0 + v_lin)
    vout_ref[...] = v

    li = lax.broadcasted_iota(jnp.int32, (LANES, LANES), 0)
    lj = lax.broadcasted_iota(jnp.int32, (LANES, LANES), 1)
    same_head = (li // N_A) == (lj // N_A)
    ones_bd = jnp.where(same_head, 1.0, 0.0).astype(BF16)

    kk = k * k_k
    kk = kk / jnp.maximum(jnp.sqrt(_head_sum(kk * kk, ones_bd, split=True)), L2_EPS)
    k = k * (1.0 + (a_lr - 1.0) * k_a)
    a_s = -kk
    b_s = kk * a_lr

    ti = lax.broadcasted_iota(jnp.int32, (tc, tc), 0)
    tj = lax.broadcasted_iota(jnp.int32, (tc, tc), 1)
    tri = jnp.where(ti >= tj, 1.0, 0.0).astype(BF16)
    ld_h, ld_m, ld_l = _split3(ld)
    cum = _dot(tri, ld_h) + _dot(tri, ld_m) + _dot(tri, ld_l)
    cum_end = cum[tc - 1:tc, :]
    p_inc = jnp.exp(cum)
    p_exc = jnp.exp(cum - ld)
    p_inv = jnp.exp(-cum)
    p_end = jnp.exp(cum_end - cum)
    decay_end = jnp.exp(cum_end)

    a_t = a_s * p_exc
    b_t = b_s * p_inv
    k_t = k * p_inv
    r_t = r * p_inc
    b_h = b_s * p_end
    k_h = k * p_end

    first_head = lax.broadcasted_iota(jnp.int32, (tc, LANES), 1) < N_A
    strict = same_head & ((li % N_A) > (lj % N_A))
    incl = same_head & ((li % N_A) >= (lj % N_A))
    eye = li == lj

    def stack(x):
        return jnp.concatenate([jnp.where(first_head, x, 0.0), jnp.where(first_head, 0.0, x)], axis=0)

    def dup(x):
        return jnp.concatenate([x, x], axis=0)

    pairs = range(C_A // LANES)
    sls = [slice(p * LANES, (p + 1) * LANES) for p in pairs]
    n2 = 2 * tc
    a_st = [stack(a_t[:, sl]) for sl in sls]
    r_st = [stack(r_t[:, sl]) for sl in sls]
    v_st = [stack(v[:, sl]).astype(BF16) for sl in sls]
    bh_st = [stack(b_h[:, sl]).astype(BF16) for sl in sls]
    kh_st = [stack(k_h[:, sl]).astype(BF16) for sl in sls]
    lhs = [jnp.concatenate([a_st[p], r_st[p]], axis=0).astype(BF16) for p in pairs]
    rhs = [jnp.concatenate([dup(b_t[:, sl]), dup(k_t[:, sl])], axis=0).astype(BF16) for sl in sls]
    x = [_dot(lhs[p], rhs[p], NT) for p in pairs]
    lk = [jnp.where(strict, x[p][0:n2, 0:n2], 0.0) for p in pairs]
    a_ak = [jnp.where(strict, x[p][0:n2, n2:2 * n2], 0.0).astype(BF16) for p in pairs]
    a_rb = [jnp.where(incl, x[p][n2:2 * n2, 0:n2], 0.0).astype(BF16) for p in pairs]
    a_rk = [jnp.where(incl, x[p][n2:2 * n2, n2:2 * n2], 0.0).astype(BF16) for p in pairs]
    zz = [jnp.concatenate([a_st[p], _dot(a_ak[p], v_st[p])], axis=1) for p in pairs]
    n_iter = int(math.log2(tc))
    for it in range(n_iter):
        lk_b = [lk[p].astype(BF16) for p in pairs]
        zz = [zz[p] + _dot(lk_b[p], zz[p].astype(BF16)) for p in pairs]
        if it < n_iter - 1:
            lk = [_dot(lk_b[p], lk_b[p]) for p in pairs]
    zz_b = [zz[p].astype(BF16) for p in pairs]
    qy = [_dot(a_rb[p], zz_b[p]) for p in pairs]
    y3 = [qy[p][:, LANES:] + _dot(a_rk[p], v_st[p]) for p in pairs]
    mg = [_dot(bh_st[p], zz_b[p], TN) for p in pairs]
    g_mat = [mg[p][:, LANES:] + _dot(kh_st[p], v_st[p], TN) for p in pairs]
    qm = [jnp.concatenate([r_st[p] + qy[p][:, 0:LANES],
                           jnp.where(eye, decay_end[:, sls[p]], 0.0) + mg[p][:, 0:LANES]], axis=0).astype(BF16)
          for p in pairs]
    out = [_dot(qm[p], h_ref[p].astype(BF16)) for p in pairs]
    y_parts = []
    for p in pairs:
        y_st = out[p][0:n2, :] + y3[p]
        y_parts.append(y_st[0:tc, :] + y_st[tc:n2, :])
        h_ref[p] = out[p][n2:, :] + g_mat[p]
    y = jnp.concatenate(y_parts, axis=1)

    inv_n = 1.0 / N_A
    mean = _head_sum(y, ones_bd) * inv_n
    d = y - mean
    var = _head_sum(d * d, ones_bd) * inv_n
    yn = d * lax.rsqrt(var + GN_EPS) * lnx_g + lnx_b
    bonus = _head_sum(r * k * r_k, ones_bd) * v
    y_ref[...] = ((yn + bonus) * g).astype(y_ref.dtype)


def rwkv_mix(proj_a, v_first, mu, vec, uw, ua, ug, uv, batch, seq, has_v_res):
    tc = RWKV_CHUNK
    nc = seq // tc
    row_spec = lambda w: pl.BlockSpec((tc, w), lambda b, c: (b * nc + c, 0))
    full = lambda s: pl.BlockSpec(s, lambda b, c: tuple(0 for _ in s))
    m = batch * seq
    return pl.pallas_call(
        functools.partial(_rwkv_kernel, has_v_res=has_v_res),
        grid=(batch, nc),
        in_specs=[row_spec(N_A_COLS), row_spec(C_A), full((1, N_A_COLS)), full((8, C_A)),
                  full((LORA_PAD, C_A)), full((LORA_PAD, C_A)), full((LORA_PAD, C_A)), full((LORA_PAD, C_A))],
        out_specs=[row_spec(C_A), row_spec(C_A)],
        out_shape=[jax.ShapeDtypeStruct((m, C_A), BF16), jax.ShapeDtypeStruct((m, C_A), F32)],
        scratch_shapes=[pltpu.VMEM((1, N_A_COLS), F32), pltpu.VMEM((C_A // LANES, LANES, LANES), F32)],
        compiler_params=_cparams("parallel", "arbitrary"),
        name="rwkv_mix",
    )(proj_a, v_first, mu, vec, uw, ua, ug, uv)


def _gelu(x):
    return 0.5 * x * (1.0 + lax.erf(x * math.sqrt(0.5)))


def _gmlp_kernel(z_ref, lng_ref, lnb_ref, w_ref, bs_ref, og_ref, o_ref):
    u = _gelu(z_ref[:, 0:C_B])
    v = _gelu(z_ref[:, C_B:2 * C_B])
    mu = jnp.mean(v, axis=-1, keepdims=True)
    d = v - mu
    var = jnp.mean(d * d, axis=-1, keepdims=True)
    vn = d * lax.rsqrt(var + LN_EPS) * lng_ref[...] + lnb_ref[...]
    ti = lax.broadcasted_iota(jnp.int32, (CHUNK, CHUNK), 0)
    tj = lax.broadcasted_iota(jnp.int32, (CHUNK, CHUNK), 1)
    causal = ti >= tj
    dh = C_B // H_B
    for h in range(H_B):
        sl = slice(h * dh, (h + 1) * dh)
        w = jnp.where(causal, w_ref[h], 0.0).astype(BF16)
        s = _dot(w, vn[:, sl].astype(BF16)) + bs_ref[:, h:h + 1]
        y = u[:, sl] * s
        ms = jnp.mean(y * y, axis=-1, keepdims=True)
        o_ref[:, sl] = (y * lax.rsqrt(ms + RMS_EPS) * og_ref[:, sl]).astype(o_ref.dtype)


def gmlp_mix(proj_bd, ln_g, ln_b, w_s, b_s_t, out_g):
    m = proj_bd.shape[0]
    full = lambda s: pl.BlockSpec(s, lambda i: tuple(0 for _ in s))
    return pl.pallas_call(
        _gmlp_kernel,
        grid=(m // CHUNK,),
        in_specs=[pl.BlockSpec((CHUNK, 2 * C_B), lambda i: (i, 0)),
                  full((1, C_B)), full((1, C_B)), full((H_B, CHUNK, CHUNK)), full((CHUNK, H_B)), full((1, C_B))],
        out_specs=pl.BlockSpec((CHUNK, C_B), lambda i: (i, 0)),
        out_shape=jax.ShapeDtypeStruct((m, C_B), BF16),
        compiler_params=_cparams("parallel"),
        name="gmlp_mix",
    )(proj_bd, ln_g, ln_b, w_s, b_s_t, out_g)


def _sb_kernel(q_ref, k_ref, v_ref, g_ref, o_ref, *, tq, scale):
    qi = pl.program_id(2)
    heads = range(q_ref.shape[1] // D_C)
    hs = [slice(h * D_C, (h + 1) * D_C) for h in heads]
    q = [q_ref[:, s] for s in hs]
    row = lax.broadcasted_iota(jnp.int32, (tq, tq), 0)
    col = lax.broadcasted_iota(jnp.int32, (tq, tq), 1)
    below = col < row
    upper = jnp.where(row > col, 1.0, 0.0).astype(BF16)

    def block(j, carry, acc, keep):
        start = pl.multiple_of(j * tq, tq)
        z = [_dot(q[h], k_ref[pl.ds(start, tq), hs[h]], NT) * scale for h in heads]
        log_beta = [jnp.minimum(z[h], 0.0) - jnp.log(1.0 + jnp.exp(-jnp.abs(z[h]))) for h in heads]
        log_1mb = [log_beta[h] - z[h] for h in heads]
        if keep is not None:
            log_1mb = [jnp.where(keep, x, 0.0) for x in log_1mb]
        parts = [_split2(x) for x in log_1mb]
        after = [_dot(hi, upper) + _dot(lo, upper) for hi, lo in parts]
        att = [jnp.exp(log_beta[h] + after[h] + carry[h]) for h in heads]
        if keep is not None:
            att = [jnp.where(keep, x, 0.0) for x in att]
        acc = [acc[h] + _dot(att[h].astype(BF16), v_ref[pl.ds(start, tq), hs[h]]) for h in heads]
        carry = [carry[h] + after[h][:, 0:1] + log_1mb[h][:, 0:1] for h in heads]
        return carry, acc

    carry = [jnp.zeros((tq, 1), F32) for _ in heads]
    acc = [jnp.zeros((tq, D_C), F32) for _ in heads]
    carry, acc = block(qi, carry, acc, below)
    has_prev = jnp.broadcast_to(qi >= 1, (tq, tq))
    carry, acc = block(jnp.maximum(qi - 1, 0), carry, acc, has_prev)

    def more(state):
        top = functools.reduce(jnp.maximum, [jnp.max(c) for c in state[1]])
        return jnp.logical_and(state[0] >= 0, top > SB_CUTOFF)

    def body(state):
        carry_j, acc_j = block(state[0], list(state[1]), list(state[2]), None)
        return state[0] - 1, tuple(carry_j), tuple(acc_j)

    _, carry, acc = lax.while_loop(more, body, (qi - 2, tuple(carry), tuple(acc)))
    for h in heads:
        ms = jnp.mean(acc[h] * acc[h], axis=-1, keepdims=True)
        o_ref[:, hs[h]] = (acc[h] * lax.rsqrt(ms + RMS_EPS) * g_ref[:, hs[h]]).astype(o_ref.dtype)


def sb_attention(proj_cd, out_g, batch, seq, tq=256, heads=SB_HEADS):
    nq = seq // tq
    w = heads * D_C
    groups = H_C // heads
    return pl.pallas_call(
        functools.partial(_sb_kernel, tq=tq, scale=D_C ** -0.5),
        grid=(batch, groups, nq),
        in_specs=[pl.BlockSpec((None, tq, w), lambda b, h, i: (b, i, h)),
                  pl.BlockSpec((None, seq, w), lambda b, h, i: (b, 0, groups + h)),
                  pl.BlockSpec((None, seq, w), lambda b, h, i: (b, 0, 2 * groups + h)),
                  pl.BlockSpec((1, w), lambda b, h, i: (0, h))],
        out_specs=pl.BlockSpec((None, tq, w), lambda b, h, i: (b, i, h)),
        out_shape=jax.ShapeDtypeStruct((batch, seq, C_C), BF16),
        compiler_params=_cparams("parallel", "parallel", "arbitrary"),
        name="sb_attention",
    )(proj_cd, proj_cd, proj_cd, out_g)


def _rope_table_kernel(pos_ref, invf_ref, cos_ref, sin_ref):
    ang = pos_ref[...].astype(F32) * invf_ref[...]
    cos_ref[...] = jnp.cos(ang)
    sin_ref[...] = jnp.sin(ang)


def rope_tables(pos_col, invf, tm=512):
    m = pos_col.shape[0]
    spec = pl.BlockSpec((tm, LANES), lambda i: (i, 0))
    return pl.pallas_call(
        _rope_table_kernel,
        grid=(m // tm,),
        in_specs=[pl.BlockSpec((tm, 1), lambda i: (i, 0)), pl.BlockSpec((1, LANES), lambda i: (0, 0))],
        out_specs=[spec, spec],
        out_shape=[jax.ShapeDtypeStruct((m, LANES), F32)] * 2,
        compiler_params=_cparams("parallel"),
        name="rope_tables",
    )(pos_col, invf)


def _rope_kernel(x_ref, cos_ref, sin_ref, o_ref):
    x = x_ref[...]
    width = x.shape[1]
    reps = width // LANES
    c = jnp.concatenate([cos_ref[...]] * reps, axis=1)
    s = jnp.concatenate([sin_ref[...]] * reps, axis=1)
    lane = lax.broadcasted_iota(jnp.int32, x.shape, 1)
    first_half = (lane % DH_D) < (DH_D // 2)
    rot = jnp.where(first_half, -pltpu.roll(x, width - DH_D // 2, 1), pltpu.roll(x, DH_D // 2, 1))
    o_ref[...] = (x * c + rot * s).astype(o_ref.dtype)


def rope_qk(proj_bd, cos_t, sin_t, tm=256):
    m = proj_bd.shape[0]
    width = 2 * C_D
    return pl.pallas_call(
        _rope_kernel,
        grid=(m // tm,),
        in_specs=[pl.BlockSpec((tm, width), lambda i: (i, 1)),
                  pl.BlockSpec((tm, LANES), lambda i: (i, 0)),
                  pl.BlockSpec((tm, LANES), lambda i: (i, 0))],
        out_specs=pl.BlockSpec((tm, width), lambda i: (i, 0)),
        out_shape=jax.ShapeDtypeStruct((m, width), BF16),
        compiler_params=_cparams("parallel"),
        name="rope_qk",
    )(proj_bd, cos_t, sin_t)


def _diff_kernel(lam_ref, q_ref, k_ref, v_ref, g_ref, o_ref, *, tq, lam_init):
    qi = pl.program_id(2)
    q = (q_ref[...].astype(F32) * (DH_D ** -0.5)).astype(BF16)
    lane = lax.broadcasted_iota(jnp.int32, q.shape, 1)
    zero = jnp.zeros_like(q)
    qs = jnp.concatenate([jnp.where(lane < DH_D, q, zero), jnp.where(lane < DH_D, zero, q)], axis=0)
    row = lax.broadcasted_iota(jnp.int32, (2 * tq, tq), 0) % tq
    col = lax.broadcasted_iota(jnp.int32, (2 * tq, tq), 1)
    visible = col <= row

    def block(j, m, l, acc, diagonal):
        start = pl.multiple_of(j * tq, tq)
        kb = k_ref[pl.ds(start, tq), :]
        vb = v_ref[pl.ds(start, tq), :]
        s = _dot(qs, kb, NT)
        if diagonal:
            s = jnp.where(visible, s, -jnp.inf)
        m_new = jnp.maximum(m, jnp.max(s, axis=-1, keepdims=True))
        alpha = jnp.exp(m - m_new)
        p = jnp.exp(s - m_new)
        l = alpha * l + jnp.sum(p, axis=-1, keepdims=True)
        acc = alpha * acc + _dot(p.astype(BF16), vb)
        return m_new, l, acc

    m0 = jnp.full((2 * tq, 1), -jnp.inf, F32)
    l0 = jnp.zeros((2 * tq, 1), F32)
    acc0 = jnp.zeros((2 * tq, DV_D), F32)
    m, l, acc = block(qi, m0, l0, acc0, True)

    def grouped(first, group):
        def body(i, state):
            for t in range(group):
                state = block(first + i * group + t, state[0], state[1], state[2], False)
            return state
        return body

    n4 = qi // 4
    n2 = (qi - 4 * n4) // 2
    n1 = qi - 4 * n4 - 2 * n2
    state = lax.fori_loop(0, n4, grouped(0, 4), (m, l, acc))
    state = lax.fori_loop(0, n2, grouped(4 * n4, 2), state)
    m, l, acc = lax.fori_loop(0, n1, grouped(4 * n4 + 2 * n2, 1), state)
    lv = lam_ref[...]
    lam = (jnp.exp(jnp.sum(lv[0:1, :] * lv[1:2, :], axis=-1, keepdims=True))
           - jnp.exp(jnp.sum(lv[2:3, :] * lv[3:4, :], axis=-1, keepdims=True)) + lam_init)
    o = acc / l
    y = o[0:tq, :] - lam * o[tq:2 * tq, :]
    ms = jnp.mean(y * y, axis=-1, keepdims=True)
    o_ref[...] = (y * lax.rsqrt(ms + LN_EPS) * g_ref[...] * (1.0 - lam_init)).astype(o_ref.dtype)


def diff_attention(lam_p, qk_rot, proj_cd, subln, batch, seq, lam_init, tq=512):
    nq = seq // tq
    return pl.pallas_call(
        functools.partial(_diff_kernel, tq=tq, lam_init=lam_init),
        grid=(batch, H_D, nq),
        in_specs=[pl.BlockSpec((4, DH_D), lambda b, h, i: (0, 0)),
                  pl.BlockSpec((None, tq, DV_D), lambda b, h, i: (b, i, h)),
                  pl.BlockSpec((None, seq, DV_D), lambda b, h, i: (b, 0, H_D + h)),
                  pl.BlockSpec((None, seq, DV_D), lambda b, h, i: (b, 0, 3 * H_C + h)),
                  pl.BlockSpec((1, DV_D), lambda b, h, i: (0, 0))],
        out_specs=pl.BlockSpec((None, tq, DV_D), lambda b, h, i: (b, i, h)),
        out_shape=jax.ShapeDtypeStruct((batch, seq, C_D), BF16),
        compiler_params=_cparams("parallel", "parallel", "arbitrary"),
        name="diff_attention",
    )(lam_p, qk_rot, qk_rot, proj_cd, subln)


def _router_kernel(h_ref, g_ref, wr_ref, n_ref, comb_ref, idx_ref):
    x = h_ref[...]
    ms = jnp.mean(x * x, axis=-1, keepdims=True)
    n = x * lax.rsqrt(ms + RMS_EPS) * g_ref[...]
    n_ref[...] = n.astype(n_ref.dtype)
    n_hi, n_lo = _split2(n)
    w_hi, w_lo = _split2(wr_ref[...])
    logits = _dot(n_hi, w_hi) + _dot(n_lo, w_hi) + _dot(n_hi, w_lo)
    lane = lax.broadcasted_iota(jnp.int32, logits.shape, 1)
    neg = -jnp.inf
    logits = jnp.where(lane < N_EXPERTS, logits, neg)
    m1 = jnp.max(logits, axis=-1, keepdims=True)
    i1 = jnp.min(jnp.where(logits == m1, lane, LANES), axis=-1, keepdims=True)
    rest = jnp.where(lane == i1, neg, logits)
    m2 = jnp.max(rest, axis=-1, keepdims=True)
    i2 = jnp.min(jnp.where(rest == m2, lane, LANES), axis=-1, keepdims=True)
    e = jnp.exp(m2 - m1)
    g1 = 1.0 / (1.0 + e)
    g2 = e / (1.0 + e)
    comb_ref[...] = jnp.where(lane == i1, g1, 0.0) + jnp.where(lane == i2, g2, 0.0)
    idx_ref[...] = jnp.where(lane == 0, i1, jnp.where(lane == 1, i2, 0))


def moe_router(h, g, w_router_pad, tm=256):
    m, d = h.shape
    spec = pl.BlockSpec((tm, LANES), lambda i: (i, 0))
    return pl.pallas_call(
        _router_kernel,
        grid=(m // tm,),
        in_specs=[pl.BlockSpec((tm, d), lambda i: (i, 0)), pl.BlockSpec((1, d), lambda i: (0, 0)),
                  pl.BlockSpec((d, LANES), lambda i: (0, 0))],
        out_specs=[pl.BlockSpec((tm, d), lambda i: (i, 0)), spec, spec],
        out_shape=[jax.ShapeDtypeStruct((m, d), BF16), jax.ShapeDtypeStruct((m, LANES), F32),
                   jax.ShapeDtypeStruct((m, LANES), jnp.int32)],
        compiler_params=_cparams("parallel"),
        name="moe_router",
    )(h, g, w_router_pad)


def _tile_state(te_ref):
    i = pl.program_id(1)
    in_use = i < te_ref[pl.num_programs(1)]
    new_weights = jnp.logical_or(i == 0, te_ref[i] != te_ref[jnp.maximum(i - 1, 0)])
    return in_use, new_weights


def _gmm_up_kernel(te_ref, a_ref, wg_ref, wu_ref, o_ref, wg_bf, wu_bf):
    in_use, new_weights = _tile_state(te_ref)

    @pl.when(new_weights)
    def _():
        wg_bf[...] = wg_ref[...].astype(BF16)
        wu_bf[...] = wu_ref[...].astype(BF16)

    @pl.when(in_use)
    def _():
        a = a_ref[...]
        g = _dot(a, wg_bf[...])
        u = _dot(a, wu_bf[...])
        o_ref[...] = (g * jax.nn.sigmoid(g) * u).astype(o_ref.dtype)

    @pl.when(jnp.logical_not(in_use))
    def _():
        o_ref[...] = jnp.zeros_like(o_ref)


def _row_spec(tm, width, n_tiles):
    return pl.BlockSpec((tm, width), lambda j, i, te: (jnp.minimum(i, te[n_tiles] - 1), 0))


def gmm_swiglu_up(tile_table, xs, wg, wu, tm, tn):
    s, k = xs.shape
    n = wg.shape[2]
    n_tiles = s // tm
    w_spec = pl.BlockSpec((None, k, tn), lambda j, i, te: (te[i], 0, j))
    return pl.pallas_call(
        _gmm_up_kernel,
        grid_spec=pltpu.PrefetchScalarGridSpec(
            num_scalar_prefetch=1,
            grid=(n // tn, n_tiles),
            in_specs=[_row_spec(tm, k, n_tiles), w_spec, w_spec],
            out_specs=pl.BlockSpec((tm, tn), lambda j, i, te: (i, j)),
            scratch_shapes=[pltpu.VMEM((k, tn), BF16), pltpu.VMEM((k, tn), BF16)],
        ),
        out_shape=jax.ShapeDtypeStruct((s, n), BF16),
        compiler_params=_cparams("arbitrary", "arbitrary"),
        name="gmm_swiglu_up",
    )(tile_table, xs, wg, wu)


def _gmm_down_kernel(te_ref, a_ref, w_ref, gate_ref, o_ref):
    in_use = pl.program_id(0) < te_ref[pl.num_programs(0)]

    @pl.when(in_use)
    def _():
        o_ref[...] = _dot(a_ref[...], w_ref[...]) * gate_ref[...]

    @pl.when(jnp.logical_not(in_use))
    def _():
        o_ref[...] = jnp.zeros_like(o_ref)


def gmm_down(tile_table, hs, wd, slot_gate, tm, tn=1024):
    s, k = hs.shape
    n = wd.shape[2]
    n_tiles = s // tm
    return pl.pallas_call(
        _gmm_down_kernel,
        grid_spec=pltpu.PrefetchScalarGridSpec(
            num_scalar_prefetch=1,
            grid=(n_tiles, n // tn),
            in_specs=[pl.BlockSpec((tm, k), lambda i, j, te: (jnp.minimum(i, te[n_tiles] - 1), 0)),
                      pl.BlockSpec((None, k, tn), lambda i, j, te: (te[i], 0, jnp.where(i < te[n_tiles], j, 0))),
                      pl.BlockSpec((tm, 1), lambda i, j, te: (jnp.minimum(i, te[n_tiles] - 1), 0))],
            out_specs=pl.BlockSpec((tm, tn), lambda i, j, te: (i, j)),
        ),
        out_shape=jax.ShapeDtypeStruct((s, n), F32),
        compiler_params=_cparams("parallel", "arbitrary"),
        name="gmm_down",
    )(tile_table, hs, wd, slot_gate)


def moe_layer(h, ffn_g, w_router, wg, wu, wd, tm=MOE_TM):
    m, d = h.shape
    w_router_pad = jnp.pad(w_router.astype(F32), ((0, 0), (0, LANES - N_EXPERTS)))
    n_b, comb, idx = moe_router(h, ffn_g.reshape(1, d).astype(F32), w_router_pad)
    top_idx = idx[:, 0:2]
    gates = jnp.take_along_axis(comb[:, 0:N_EXPERTS], top_idx, axis=1)

    flat_e = top_idx.reshape(-1)
    n_pairs = 2 * m
    onehot = (flat_e[:, None] == jnp.arange(N_EXPERTS, dtype=jnp.int32)[None, :]).astype(jnp.int32)
    counts = jnp.sum(onehot, axis=0)
    rank = jnp.sum((jnp.cumsum(onehot, axis=0) - onehot) * onehot, axis=1)
    padded = ((counts + tm - 1) // tm) * tm
    pad_off = jnp.cumsum(padded) - padded
    raw_off = jnp.cumsum(counts) - counts
    dest = pad_off[flat_e] + rank
    n_tiles = n_pairs // tm + N_EXPERTS
    n_slots = n_tiles * tm
    order = jnp.argsort(flat_e, stable=True).astype(jnp.int32)
    tile_start = jnp.arange(n_tiles, dtype=jnp.int32) * tm
    pad_end = pad_off + padded
    tile_expert = jnp.minimum(jnp.sum((tile_start[:, None] >= pad_end[None, :]).astype(jnp.int32), axis=1),
                              N_EXPERTS - 1).astype(jnp.int32)
    slot = jnp.arange(n_slots, dtype=jnp.int32)
    slot_e = jnp.repeat(tile_expert, tm)
    local = slot - pad_off[slot_e]
    valid = (local < counts[slot_e]) & (slot < pad_end[N_EXPERTS - 1])
    src_pair = order[jnp.clip(raw_off[slot_e] + local, 0, n_pairs - 1)]
    slot_token = jnp.where(valid, src_pair // 2, 0)
    slot_gate = jnp.where(valid, gates.reshape(-1)[src_pair], 0.0).astype(F32).reshape(n_slots, 1)

    tiles_in_use = (pad_end[N_EXPERTS - 1] // tm).astype(jnp.int32).reshape(1)
    tile_table = jnp.concatenate([tile_expert, tiles_in_use])

    xs = jnp.take(n_b, slot_token, axis=0, mode="clip")
    hs = gmm_swiglu_up(tile_table, xs, wg, wu, tm, min(MOE_TN, wg.shape[2]))
    ys = gmm_down(tile_table, hs, wd.astype(BF16), slot_gate, tm, min(1024, wd.shape[2]))
    dest2 = dest.reshape(m, 2)
    return h + jnp.take(ys, dest2[:, 0], axis=0) + jnp.take(ys, dest2[:, 1], axis=0)


def _rwkv_columns(w_in, layer):
    d = w_in.shape[0]
    o_w = C_A
    o_k = o_w + W_LORA
    o_v = o_k + C_A
    o_a = o_v + C_A
    o_g = o_a + A_LORA
    parts = [w_in[:, 0:C_A], w_in[:, o_k:o_k + C_A], w_in[:, o_v:o_v + C_A],
             w_in[:, o_w:o_w + W_LORA], w_in[:, o_a:o_a + A_LORA], w_in[:, o_g:o_g + G_LORA]]
    used = W_LORA + A_LORA + G_LORA
    if layer > 0:
        parts.append(w_in[:, N_IN0:N_IN0 + V_LORA])
        used += V_LORA
    parts.append(jnp.zeros((d, LORA_PAD - used), w_in.dtype))
    return jnp.concatenate(parts, axis=1)


def _rwkv_mu(mu, mu_v):
    o_w = C_A
    o_k = o_w + W_LORA
    o_v = o_k + C_A
    o_a = o_v + C_A
    o_g = o_a + A_LORA
    parts = [mu[0:C_A], mu[o_k:o_k + C_A], mu[o_v:o_v + C_A],
             mu[o_w:o_w + W_LORA], mu[o_a:o_a + A_LORA], mu[o_g:o_g + G_LORA]]
    used = W_LORA + A_LORA + G_LORA
    if mu_v is not None:
        parts.append(mu_v)
        used += V_LORA
    parts.append(jnp.zeros((LORA_PAD - used,), mu.dtype))
    return jnp.concatenate(parts).reshape(1, N_A_COLS).astype(F32)


def _pad_rows(w, offset):
    return jnp.pad(w, ((offset, LORA_PAD - offset - w.shape[0]), (0, 0))).astype(BF16)


def kernel(x, p, positions, attn_norm, w_in0, w_in_rest, w_out, rwkv_mu, rwkv_mu_v, rwkv_w0, rwkv_w_up, rwkv_a0, rwkv_a_up, rwkv_v0, rwkv_v_up, rwkv_g_up, rwkv_k_k, rwkv_k_a, rwkv_r_k, rwkv_lnx_g, rwkv_lnx_b, gm_ln_g, gm_ln_b, gm_w_s, gm_b_s, gm_out_g, sb_out_g, diff_lambda, diff_subln, ffn_norm, dense_w_gate, dense_w_up, dense_w_down, moe_router, moe_w_gate, moe_w_up, moe_w_down, ple_norm, ple_w_gate, ple_w_proj, final_norm):
    batch, seq, d = x.shape
    m = batch * seq
    h = x.reshape(m, d).astype(F32)

    inv_freq = 1.0 / (ROPE_THETA ** (jnp.arange(0, DH_D, 2, dtype=F32) / DH_D))
    invf = jnp.tile(inv_freq, LANES // (DH_D // 2)).reshape(1, LANES)
    cos_t, sin_t = rope_tables(positions.reshape(m, 1), invf)

    v_first = None
    for i in range(DEPTH):
        w_in = w_in0 if i == 0 else w_in_rest[i - 1]
        o_b = N_RWKV
        o_c = o_b + N_GMLP
        o_d = o_c + N_SB
        w_a = _rwkv_columns(w_in, i).astype(BF16)
        w_bd = jnp.concatenate([w_in[:, o_b:o_c], w_in[:, o_d:o_d + 2 * C_D]], axis=1).astype(BF16)
        w_cd = jnp.concatenate([w_in[:, o_c:o_d], w_in[:, o_d + 2 * C_D:o_d + 3 * C_D]], axis=1).astype(BF16)

        n = rmsnorm(h, attn_norm[i], BF16)
        proj_a = matmul(n, w_a, F32)
        proj_bd = matmul(n, w_bd, F32)
        proj_cd = matmul(n, w_cd, BF16)

        mu = _rwkv_mu(rwkv_mu[i], rwkv_mu_v[i - 1] if i > 0 else None)
        zeros_c = jnp.zeros((C_A,), F32)
        vec = jnp.stack([rwkv_w0[i], rwkv_a0[i], rwkv_v0[i - 1] if i > 0 else zeros_c, rwkv_k_k[i], rwkv_k_a[i],
                         rwkv_r_k[i].reshape(C_A), rwkv_lnx_g[i], rwkv_lnx_b[i]]).astype(F32)
        uw = _pad_rows(rwkv_w_up[i], 0)
        ua = _pad_rows(rwkv_a_up[i], W_LORA)
        ug = _pad_rows(rwkv_g_up[i], W_LORA + A_LORA)
        uv = _pad_rows(rwkv_v_up[i - 1], W_LORA + A_LORA + G_LORA) if i > 0 else jnp.zeros((LORA_PAD, C_A), BF16)
        vf_in = v_first if i > 0 else proj_a
        y_a, v_out = rwkv_mix(proj_a, vf_in, mu, vec, uw, ua, ug, uv, batch, seq, has_v_res=i > 0)
        if i == 0:
            v_first = v_out

        y_b = gmlp_mix(proj_bd, gm_ln_g[i].reshape(1, C_B), gm_ln_b[i].reshape(1, C_B), gm_w_s[i],
                       gm_b_s[i].T, gm_out_g[i].reshape(1, C_B))

        proj_cd3 = proj_cd.reshape(batch, seq, 4 * C_C)
        y_c = sb_attention(proj_cd3, sb_out_g[i].reshape(1, C_C), batch, seq).reshape(m, C_C)

        qk_rot = rope_qk(proj_bd, cos_t, sin_t).reshape(batch, seq, 2 * C_D)
        lam_init = 0.8 - 0.6 * math.exp(-0.3 * i)
        y_d = diff_attention(diff_lambda[i].astype(F32), qk_rot, proj_cd3, diff_subln[i].reshape(1, DV_D),
                             batch, seq, lam_init).reshape(m, C_D)

        h = mix_out_proj((y_a, y_b, y_c, y_d), w_out[i].astype(BF16), h)

        j = i // 2
        if i % 2 == 0:
            n = rmsnorm(h, ffn_norm[i], BF16)
            n_row_tiles = m // DENSE_TM
            one_group = jnp.concatenate([jnp.zeros((n_row_tiles,), jnp.int32), jnp.full((1,), n_row_tiles, jnp.int32)])
            hid = gmm_swiglu_up(one_group, n, dense_w_gate[j][None], dense_w_up[j][None], DENSE_TM, DENSE_TN)
            h = down_proj_residual(hid, dense_w_down[j].astype(BF16), h)
        else:
            h = moe_layer(h, ffn_norm[i], moe_router[j], moe_w_gate[j], moe_w_up[j], moe_w_down[j])

        n = rmsnorm(h, ple_norm[i], BF16)
        h = ple_residual(n, ple_w_gate[i].astype(BF16), p[i].reshape(m, P_DIM).astype(BF16),
                         ple_w_proj[i].astype(BF16), h)

    return rmsnorm(h, final_norm, x.dtype).reshape(batch, seq, d)
```

```python
import functools
import math

import jax
import jax.numpy as jnp
import numpy as np
from jax import lax
from jax.experimental import pallas as pl
from jax.experimental.pallas import tpu as pltpu

F32 = jnp.float32
BF16 = jnp.bfloat16

D_MODEL = 4096
DEPTH = 2
C_A = 1024
N_A = 64
W_LORA, A_LORA, V_LORA, G_LORA = 64, 64, 32, 160
C_B = 1024
H_B = 8
CHUNK = 128
C_C = 1024
H_C = 8
D_C = 128
C_D = 1024
H_D = 8
DV_D = 128
DH_D = 64
D_FF = 11008
N_EXPERTS = 8
D_FF_EXPERT = 5632
P_DIM = 256
ROPE_THETA = 10000.0
RMS_EPS = 1e-6
LN_EPS = 1e-5
GN_EPS = 64e-5
L2_EPS = 1e-12
N_RWKV = 3 * C_A + W_LORA + A_LORA + G_LORA
N_GMLP = 2 * C_B
N_SB = 3 * C_C
N_DIFF = 3 * C_D
N_IN0 = N_RWKV + N_GMLP + N_SB + N_DIFF

LANES = 128
VMEM_LIMIT = 56 * 1024 * 1024

LORA_PAD = 512
N_A_COLS = 3 * C_A + LORA_PAD
DENSE_TM, DENSE_TN = 1024, 256
DENSE_TK = 5504
MOE_TM, MOE_TN = 512, 512
RWKV_CHUNK = 64
RWKV_GROUP = 2
SB_CUTOFF = -110.0
SB_HEADS = 4

NT = (((1,), (1,)), ((), ()))
TN = (((0,), (0,)), ((), ()))


def _cparams(*sem):
    return pltpu.CompilerParams(dimension_semantics=sem, vmem_limit_bytes=VMEM_LIMIT)


def _dot(a, b, dims=None):
    if dims is None:
        return jnp.dot(a, b, preferred_element_type=F32)
    return lax.dot_general(a, b, dims, preferred_element_type=F32)


def _split2(x):
    hi = x.astype(BF16)
    lo = (x - hi.astype(F32)).astype(BF16)
    return hi, lo


def _split3(x):
    hi = x.astype(BF16)
    r = x - hi.astype(F32)
    mid = r.astype(BF16)
    lo = (r - mid.astype(F32)).astype(BF16)
    return hi, mid, lo


def _pick_tile(n, prefs):
    for t in prefs:
        if n % t == 0:
            return t
    raise ValueError(f"no tile for {n}")


def _rmsnorm_kernel(x_ref, g_ref, o_ref, *, eps):
    x = x_ref[...].astype(F32)
    ms = jnp.mean(x * x, axis=-1, keepdims=True)
    o_ref[...] = (x * lax.rsqrt(ms + eps) * g_ref[...]).astype(o_ref.dtype)


def rmsnorm(x, g, out_dtype, tm=512):
    m, d = x.shape
    return pl.pallas_call(
        functools.partial(_rmsnorm_kernel, eps=RMS_EPS),
        grid=(m // tm,),
        in_specs=[pl.BlockSpec((tm, d), lambda i: (i, 0)), pl.BlockSpec((1, d), lambda i: (0, 0))],
        out_specs=pl.BlockSpec((tm, d), lambda i: (i, 0)),
        out_shape=jax.ShapeDtypeStruct((m, d), out_dtype),
        compiler_params=_cparams("parallel"),
        name="rmsnorm",
    )(x, g.reshape(1, d).astype(F32))


def _cast_kernel(x_ref, o_ref):
    o_ref[...] = x_ref[...].astype(o_ref.dtype)


def cast_experts_bf16(w, rows=512):
    e, k, n = w.shape
    rows = min(rows, k)
    spec = pl.BlockSpec((None, rows, n), lambda i, j: (i, j, 0))
    return pl.pallas_call(
        _cast_kernel,
        grid=(e, k // rows),
        in_specs=[spec],
        out_specs=spec,
        out_shape=jax.ShapeDtypeStruct(w.shape, BF16),
        compiler_params=_cparams("parallel", "parallel"),
        name="cast_experts_bf16",
    )(w)


def _mm_kernel(a_ref, b_ref, o_ref):
    o_ref[...] = _dot(a_ref[...], b_ref[...]).astype(o_ref.dtype)


def matmul(a, b, out_dtype, tm=1024):
    m, k = a.shape
    n = b.shape[1]
    tn = _pick_tile(n, (1024, 512, 256, 128))
    return pl.pallas_call(
        _mm_kernel,
        grid=(m // tm, n // tn),
        in_specs=[pl.BlockSpec((tm, k), lambda i, j: (i, 0)), pl.BlockSpec((k, tn), lambda i, j: (0, j))],
        out_specs=pl.BlockSpec((tm, tn), lambda i, j: (i, j)),
        out_shape=jax.ShapeDtypeStruct((m, n), out_dtype),
        compiler_params=_cparams("parallel", "arbitrary"),
        name="matmul",
    )(a, b)


def _mix_out_kernel(ya_ref, yb_ref, yc_ref, yd_ref, w_ref, res_ref, o_ref):
    kq = ya_ref.shape[1]
    acc = res_ref[...]
    for idx, y_ref in enumerate((ya_ref, yb_ref, yc_ref, yd_ref)):
        acc = acc + _dot(y_ref[...], w_ref[idx * kq:(idx + 1) * kq, :])
    o_ref[...] = acc


def mix_out_proj(ys, w, res, tm=1024, tn=512):
    m, kq = ys[0].shape
    k, n = w.shape
    y_spec = pl.BlockSpec((tm, kq), lambda i, j: (i, 0))
    return pl.pallas_call(
        _mix_out_kernel,
        grid=(m // tm, n // tn),
        in_specs=[y_spec, y_spec, y_spec, y_spec,
                  pl.BlockSpec((k, tn), lambda i, j: (0, j)),
                  pl.BlockSpec((tm, tn), lambda i, j: (i, j))],
        out_specs=pl.BlockSpec((tm, tn), lambda i, j: (i, j)),
        out_shape=jax.ShapeDtypeStruct((m, n), F32),
        compiler_params=_cparams("parallel", "arbitrary"),
        name="mix_out_proj",
    )(*ys, w, res)


def _down_res_kernel(a_ref, b_ref, res_ref, o_ref, acc_ref):
    kk = pl.program_id(2)

    @pl.when(kk == 0)
    def _():
        acc_ref[...] = res_ref[...]

    acc_ref[...] += _dot(a_ref[...], b_ref[...])

    @pl.when(kk == pl.num_programs(2) - 1)
    def _():
        o_ref[...] = acc_ref[...]


def down_proj_residual(a, b, res, tm=1024, tn=512, tk=DENSE_TK):
    m, k = a.shape
    n = b.shape[1]
    return pl.pallas_call(
        _down_res_kernel,
        grid=(m // tm, n // tn, k // tk),
        in_specs=[pl.BlockSpec((tm, tk), lambda i, j, kk: (i, kk)),
                  pl.BlockSpec((tk, tn), lambda i, j, kk: (kk, j)),
                  pl.BlockSpec((tm, tn), lambda i, j, kk: (i, j))],
        out_specs=pl.BlockSpec((tm, tn), lambda i, j, kk: (i, j)),
        out_shape=jax.ShapeDtypeStruct((m, n), F32),
        scratch_shapes=[pltpu.VMEM((tm, tn), F32)],
        compiler_params=_cparams("parallel", "parallel", "arbitrary"),
        name="down_proj_residual",
    )(a, b, res)


def _ple_kernel(n_ref, wg_ref, p_ref, wp_ref, res_ref, o_ref):
    gate = jax.nn.sigmoid(_dot(n_ref[...], wg_ref[...]))
    emb = _dot(p_ref[...], wp_ref[...])
    o_ref[...] = res_ref[...] + gate * emb


def ple_residual(n, wg, p, wp, res, tm=1024, tn=512):
    m, k = n.shape
    nn = wg.shape[1]
    pd = p.shape[1]
    return pl.pallas_call(
        _ple_kernel,
        grid=(m // tm, nn // tn),
        in_specs=[pl.BlockSpec((tm, k), lambda i, j: (i, 0)),
                  pl.BlockSpec((k, tn), lambda i, j: (0, j)),
                  pl.BlockSpec((tm, pd), lambda i, j: (i, 0)),
                  pl.BlockSpec((pd, tn), lambda i, j: (0, j)),
                  pl.BlockSpec((tm, tn), lambda i, j: (i, j))],
        out_specs=pl.BlockSpec((tm, tn), lambda i, j: (i, j)),
        out_shape=jax.ShapeDtypeStruct((m, nn), F32),
        compiler_params=_cparams("parallel", "arbitrary"),
        name="ple_residual",
    )(n, wg, p, wp, res)


def _head_sum(x, ones_bd, split=False):
    outs = []
    for p in range(x.shape[1] // LANES):
        xp = x[:, p * LANES:(p + 1) * LANES]
        if split:
            hi, lo = _split2(xp)
            outs.append(_dot(hi, ones_bd) + _dot(lo, ones_bd))
        else:
            outs.append(_dot(xp.astype(BF16), ones_bd))
    return jnp.concatenate(outs, axis=1)


def _rwkv_kernel(z_ref, vf_ref, mu_ref, vec_ref, uw_ref, ua_ref, ug_ref, uv_ref,
                 y_ref, vout_ref, prev_ref, h_ref, *, has_v_res):
    tc = RWKV_CHUNK
    tr = z_ref.shape[0]
    chunks = range(tr // tc)
    c = pl.program_id(1)

    @pl.when(c == 0)
    def _():
        prev_ref[...] = jnp.zeros_like(prev_ref)
        h_ref[...] = jnp.zeros_like(h_ref)

    z = z_ref[...]
    row = lax.broadcasted_iota(jnp.int32, z.shape, 0)
    zp = jnp.where(row == 0, prev_ref[...], pltpu.roll(z, 1, 0))
    prev_ref[...] = z[tr - 1:tr, :]
    zs = z + (zp - z) * mu_ref[...]
    r = zs[:, 0:C_A]
    k = zs[:, C_A:2 * C_A]
    v = zs[:, 2 * C_A:3 * C_A]
    lr = zs[:, 3 * C_A:3 * C_A + LORA_PAD]

    w0 = vec_ref[0:1, :]
    a0 = vec_ref[1:2, :]
    v0 = vec_ref[2:3, :]
    k_k = vec_ref[3:4, :]
    k_a = vec_ref[4:5, :]
    r_k = vec_ref[5:6, :]
    lnx_g = vec_ref[6:7, :]
    lnx_b = vec_ref[7:8, :]

    lr_b = lr.astype(BF16)
    w_lin = _dot(jnp.tanh(lr).astype(BF16), uw_ref[...])
    a_lin = _dot(lr_b, ua_ref[...])
    g = _dot(jax.nn.sigmoid(lr).astype(BF16), ug_ref[...])
    w = -jax.nn.softplus(-(w0 + w_lin)) - 0.5
    ld = -jnp.exp(w)
    a_lr = jax.nn.sigmoid(a0 + a_lin)
    if has_v_res:
        v_lin = _dot(lr_b, uv_ref[...])
        v = v + (vf_ref[...] - v) * jax.nn.sigmoid(v0 + v_lin)
    vout_ref[...] = v

    li = lax.broadcasted_iota(jnp.int32, (LANES, LANES), 0)
    lj = lax.broadcasted_iota(jnp.int32, (LANES, LANES), 1)
    same_head = (li // N_A) == (lj // N_A)
    ones_bd = jnp.where(same_head, 1.0, 0.0).astype(BF16)

    kk = k * k_k
    kk = kk / jnp.maximum(jnp.sqrt(_head_sum(kk * kk, ones_bd, split=True)), L2_EPS)
    k = k * (1.0 + (a_lr - 1.0) * k_a)
    a_s = -kk
    b_s = kk * a_lr

    ti = lax.broadcasted_iota(jnp.int32, (tr, tr), 0)
    tj = lax.broadcasted_iota(jnp.int32, (tr, tr), 1)
    tri = jnp.where(((ti // tc) == (tj // tc)) & (ti >= tj), 1.0, 0.0).astype(BF16)
    ld_h, ld_m, ld_l = _split3(ld)
    cum = _dot(tri, ld_h) + _dot(tri, ld_m) + _dot(tri, ld_l)
    cum_last = [cum[(ci + 1) * tc - 1:(ci + 1) * tc, :] for ci in chunks]
    cum_end = jnp.concatenate([jnp.broadcast_to(x, (tc, C_A)) for x in cum_last], axis=0)
    p_inc = jnp.exp(cum)
    p_exc = jnp.exp(cum - ld)
    p_inv = jnp.exp(-cum)
    p_end = jnp.exp(cum_end - cum)
    decay_end = [jnp.exp(x) for x in cum_last]

    a_t = a_s * p_exc
    b_t = b_s * p_inv
    k_t = k * p_inv
    r_t = r * p_inc
    b_h = b_s * p_end
    k_h = k * p_end

    first_head = lax.broadcasted_iota(jnp.int32, (tc, LANES), 1) < N_A
    strict = same_head & ((li % N_A) > (lj % N_A))
    incl = same_head & ((li % N_A) >= (lj % N_A))
    eye = li == lj

    def stack(x):
        return jnp.concatenate([jnp.where(first_head, x, 0.0), jnp.where(first_head, 0.0, x)], axis=0)

    def dup(x):
        return jnp.concatenate([x, x], axis=0)

    n_pairs = C_A // LANES
    chains = [(ci, p) for ci in chunks for p in range(n_pairs)]
    pairs = range(len(chains))
    cut = [(slice(ci * tc, (ci + 1) * tc), slice(p * LANES, (p + 1) * LANES)) for ci, p in chains]
    n2 = 2 * tc
    a_st = [stack(a_t[rs, sl]) for rs, sl in cut]
    r_st = [stack(r_t[rs, sl]) for rs, sl in cut]
    v_st = [stack(v[rs, sl]).astype(BF16) for rs, sl in cut]
    bh_st = [stack(b_h[rs, sl]).astype(BF16) for rs, sl in cut]
    kh_st = [stack(k_h[rs, sl]).astype(BF16) for rs, sl in cut]
    lhs = [jnp.concatenate([a_st[p], r_st[p]], axis=0).astype(BF16) for p in pairs]
    rhs = [jnp.concatenate([dup(b_t[rs, sl]), dup(k_t[rs, sl])], axis=0).astype(BF16) for rs, sl in cut]
    x = [_dot(lhs[p], rhs[p], NT) for p in pairs]
    lk = [jnp.where(strict, x[p][0:n2, 0:n2], 0.0) for p in pairs]
    a_ak = [jnp.where(strict, x[p][0:n2, n2:2 * n2], 0.0).astype(BF16) for p in pairs]
    a_rb = [jnp.where(incl, x[p][n2:2 * n2, 0:n2], 0.0).astype(BF16) for p in pairs]
    a_rk = [jnp.where(incl, x[p][n2:2 * n2, n2:2 * n2], 0.0).astype(BF16) for p in pairs]
    zz = [jnp.concatenate([a_st[p], _dot(a_ak[p], v_st[p])], axis=1) for p in pairs]
    n_iter = int(math.log2(tc))
    for it in range(n_iter):
        lk_b = [lk[p].astype(BF16) for p in pairs]
        zz = [zz[p] + _dot(lk_b[p], zz[p].astype(BF16)) for p in pairs]
        if it < n_iter - 1:
            lk = [_dot(lk_b[p], lk_b[p]) for p in pairs]
    zz_b = [zz[p].astype(BF16) for p in pairs]
    qy = [_dot(a_rb[p], zz_b[p]) for p in pairs]
    y3 = [qy[p][:, LANES:] + _dot(a_rk[p], v_st[p]) for p in pairs]
    mg = [_dot(bh_st[p], zz_b[p], TN) for p in pairs]
    g_mat = [mg[p][:, LANES:] + _dot(kh_st[p], v_st[p], TN) for p in pairs]
    qm = [jnp.concatenate([r_st[p] + qy[p][:, 0:LANES],
                           jnp.where(eye, decay_end[chains[p][0]][:, cut[p][1]], 0.0) + mg[p][:, 0:LANES]],
                          axis=0).astype(BF16) for p in pairs]
    state = [h_ref[p] for p in range(n_pairs)]
    y_rows = []
    for ci in chunks:
        first = ci * n_pairs
        out = [_dot(qm[first + p], state[p].astype(BF16)) for p in range(n_pairs)]
        y_parts = []
        for p in range(n_pairs):
            y_st = out[p][0:n2, :] + y3[first + p]
            y_parts.append(y_st[0:tc, :] + y_st[tc:n2, :])
            state[p] = out[p][n2:, :] + g_mat[first + p]
        y_rows.append(jnp.concatenate(y_parts, axis=1))
    for p in range(n_pairs):
        h_ref[p] = state[p]
    y = jnp.concatenate(y_rows, axis=0)

    inv_n = 1.0 / N_A
    mean = _head_sum(y, ones_bd) * inv_n
    d = y - mean
    var = _head_sum(d * d, ones_bd) * inv_n
    yn = d * lax.rsqrt(var + GN_EPS) * lnx_g + lnx_b
    bonus = _head_sum(r * k * r_k, ones_bd) * v
    y_ref[...] = ((yn + bonus) * g).astype(y_ref.dtype)


def rwkv_mix(proj_a, v_first, mu, vec, uw, ua, ug, uv, batch, seq, has_v_res):
    tr = RWKV_CHUNK * RWKV_GROUP
    nc = seq // tr
    row_spec = lambda w: pl.BlockSpec((tr, w), lambda b, c: (b * nc + c, 0))
    full = lambda s: pl.BlockSpec(s, lambda b, c: tuple(0 for _ in s))
    m = batch * seq
    return pl.pallas_call(
        functools.partial(_rwkv_kernel, has_v_res=has_v_res),
        grid=(batch, nc),
        in_specs=[row_spec(N_A_COLS), row_spec(C_A), full((1, N_A_COLS)), full((8, C_A)),
                  full((LORA_PAD, C_A)), full((LORA_PAD, C_A)), full((LORA_PAD, C_A)), full((LORA_PAD, C_A))],
        out_specs=[row_spec(C_A), row_spec(C_A)],
        out_shape=[jax.ShapeDtypeStruct((m, C_A), BF16), jax.ShapeDtypeStruct((m, C_A), F32)],
        scratch_shapes=[pltpu.VMEM((1, N_A_COLS), F32), pltpu.VMEM((C_A // LANES, LANES, LANES), F32)],
        compiler_params=_cparams("parallel", "arbitrary"),
        name="rwkv_mix",
    )(proj_a, v_first, mu, vec, uw, ua, ug, uv)


def _gelu(x):
    return 0.5 * x * (1.0 + lax.erf(x * math.sqrt(0.5)))


def _gmlp_kernel(z_ref, lng_ref, lnb_ref, w_ref, bs_ref, og_ref, o_ref):
    u = _gelu(z_ref[:, 0:C_B])
    v = _gelu(z_ref[:, C_B:2 * C_B])
    mu = jnp.mean(v, axis=-1, keepdims=True)
    d = v - mu
    var = jnp.mean(d * d, axis=-1, keepdims=True)
    vn = d * lax.rsqrt(var + LN_EPS) * lng_ref[...] + lnb_ref[...]
    ti = lax.broadcasted_iota(jnp.int32, (CHUNK, CHUNK), 0)
    tj = lax.broadcasted_iota(jnp.int32, (CHUNK, CHUNK), 1)
    causal = ti >= tj
    dh = C_B // H_B
    for h in range(H_B):
        sl = slice(h * dh, (h + 1) * dh)
        w = jnp.where(causal, w_ref[h], 0.0).astype(BF16)
        s = _dot(w, vn[:, sl].astype(BF16)) + bs_ref[:, h:h + 1]
        y = u[:, sl] * s
        ms = jnp.mean(y * y, axis=-1, keepdims=True)
        o_ref[:, sl] = (y * lax.rsqrt(ms + RMS_EPS) * og_ref[:, sl]).astype(o_ref.dtype)


def gmlp_mix(proj_bd, ln_g, ln_b, w_s, b_s_t, out_g):
    m = proj_bd.shape[0]
    full = lambda s: pl.BlockSpec(s, lambda i: tuple(0 for _ in s))
    return pl.pallas_call(
        _gmlp_kernel,
        grid=(m // CHUNK,),
        in_specs=[pl.BlockSpec((CHUNK, 2 * C_B), lambda i: (i, 0)),
                  full((1, C_B)), full((1, C_B)), full((H_B, CHUNK, CHUNK)), full((CHUNK, H_B)), full((1, C_B))],
        out_specs=pl.BlockSpec((CHUNK, C_B), lambda i: (i, 0)),
        out_shape=jax.ShapeDtypeStruct((m, C_B), BF16),
        compiler_params=_cparams("parallel"),
        name="gmlp_mix",
    )(proj_bd, ln_g, ln_b, w_s, b_s_t, out_g)


def _sb_kernel(q_ref, k_ref, v_ref, g_ref, o_ref, *, tq, scale):
    qi = pl.program_id(2)
    heads = range(q_ref.shape[1] // D_C)
    hs = [slice(h * D_C, (h + 1) * D_C) for h in heads]
    q = [q_ref[:, s] for s in hs]
    row = lax.broadcasted_iota(jnp.int32, (tq, tq), 0)
    col = lax.broadcasted_iota(jnp.int32, (tq, tq), 1)
    below = col < row
    upper = jnp.where(row > col, 1.0, 0.0).astype(BF16)

    def block(j, carry, acc, keep):
        start = pl.multiple_of(j * tq, tq)
        z = [_dot(q[h], k_ref[pl.ds(start, tq), hs[h]], NT) * scale for h in heads]
        log_beta = [jnp.minimum(z[h], 0.0) - jnp.log(1.0 + jnp.exp(-jnp.abs(z[h]))) for h in heads]
        log_1mb = [log_beta[h] - z[h] for h in heads]
        if keep is not None:
            log_1mb = [jnp.where(keep, x, 0.0) for x in log_1mb]
        parts = [_split2(x) for x in log_1mb]
        after = [_dot(hi, upper) + _dot(lo, upper) for hi, lo in parts]
        att = [jnp.exp(log_beta[h] + after[h] + carry[h]) for h in heads]
        if keep is not None:
            att = [jnp.where(keep, x, 0.0) for x in att]
        acc = [acc[h] + _dot(att[h].astype(BF16), v_ref[pl.ds(start, tq), hs[h]]) for h in heads]
        carry = [carry[h] + after[h][:, 0:1] + log_1mb[h][:, 0:1] for h in heads]
        return carry, acc

    carry = [jnp.zeros((tq, 1), F32) for _ in heads]
    acc = [jnp.zeros((tq, D_C), F32) for _ in heads]
    carry, acc = block(qi, carry, acc, below)
    has_prev = jnp.broadcast_to(qi >= 1, (tq, tq))
    carry, acc = block(jnp.maximum(qi - 1, 0), carry, acc, has_prev)

    def more(state):
        top = functools.reduce(jnp.maximum, [jnp.max(c) for c in state[1]])
        return jnp.logical_and(state[0] >= 0, top > SB_CUTOFF)

    def body(state):
        carry_j, acc_j = block(state[0], list(state[1]), list(state[2]), None)
        return state[0] - 1, tuple(carry_j), tuple(acc_j)

    _, carry, acc = lax.while_loop(more, body, (qi - 2, tuple(carry), tuple(acc)))
    for h in heads:
        ms = jnp.mean(acc[h] * acc[h], axis=-1, keepdims=True)
        o_ref[:, hs[h]] = (acc[h] * lax.rsqrt(ms + RMS_EPS) * g_ref[:, hs[h]]).astype(o_ref.dtype)


def sb_attention(proj_cd, out_g, batch, seq, tq=256, heads=SB_HEADS):
    nq = seq // tq
    w = heads * D_C
    groups = H_C // heads
    return pl.pallas_call(
        functools.partial(_sb_kernel, tq=tq, scale=D_C ** -0.5),
        grid=(batch, groups, nq),
        in_specs=[pl.BlockSpec((None, tq, w), lambda b, h, i: (b, i, h)),
                  pl.BlockSpec((None, seq, w), lambda b, h, i: (b, 0, groups + h)),
                  pl.BlockSpec((None, seq, w), lambda b, h, i: (b, 0, 2 * groups + h)),
                  pl.BlockSpec((1, w), lambda b, h, i: (0, h))],
        out_specs=pl.BlockSpec((None, tq, w), lambda b, h, i: (b, i, h)),
        out_shape=jax.ShapeDtypeStruct((batch, seq, C_C), BF16),
        compiler_params=_cparams("parallel", "parallel", "arbitrary"),
        name="sb_attention",
    )(proj_cd, proj_cd, proj_cd, out_g)


def _rope_table_kernel(pos_ref, invf_ref, cos_ref, sin_ref):
    ang = pos_ref[...].astype(F32) * invf_ref[...]
    cos_ref[...] = jnp.cos(ang)
    sin_ref[...] = jnp.sin(ang)


def rope_tables(pos_col, invf, tm=512):
    m = pos_col.shape[0]
    spec = pl.BlockSpec((tm, LANES), lambda i: (i, 0))
    return pl.pallas_call(
        _rope_table_kernel,
        grid=(m // tm,),
        in_specs=[pl.BlockSpec((tm, 1), lambda i: (i, 0)), pl.BlockSpec((1, LANES), lambda i: (0, 0))],
        out_specs=[spec, spec],
        out_shape=[jax.ShapeDtypeStruct((m, LANES), F32)] * 2,
        compiler_params=_cparams("parallel"),
        name="rope_tables",
    )(pos_col, invf)


def _rope_kernel(x_ref, cos_ref, sin_ref, o_ref):
    x = x_ref[...]
    width = x.shape[1]
    reps = width // LANES
    c = jnp.concatenate([cos_ref[...]] * reps, axis=1)
    s = jnp.concatenate([sin_ref[...]] * reps, axis=1)
    lane = lax.broadcasted_iota(jnp.int32, x.shape, 1)
    first_half = (lane % DH_D) < (DH_D // 2)
    rot = jnp.where(first_half, -pltpu.roll(x, width - DH_D // 2, 1), pltpu.roll(x, DH_D // 2, 1))
    o_ref[...] = (x * c + rot * s).astype(o_ref.dtype)


def rope_qk(proj_bd, cos_t, sin_t, tm=256):
    m = proj_bd.shape[0]
    width = 2 * C_D
    return pl.pallas_call(
        _rope_kernel,
        grid=(m // tm,),
        in_specs=[pl.BlockSpec((tm, width), lambda i: (i, 1)),
                  pl.BlockSpec((tm, LANES), lambda i: (i, 0)),
                  pl.BlockSpec((tm, LANES), lambda i: (i, 0))],
        out_specs=pl.BlockSpec((tm, width), lambda i: (i, 0)),
        out_shape=jax.ShapeDtypeStruct((m, width), BF16),
        compiler_params=_cparams("parallel"),
        name="rope_qk",
    )(proj_bd, cos_t, sin_t)


def _diff_kernel(lam_ref, q_ref, k_ref, v_ref, g_ref, o_ref, *, tq, lam_init):
    qi = pl.program_id(2)
    q = (q_ref[...].astype(F32) * (DH_D ** -0.5)).astype(BF16)
    lane = lax.broadcasted_iota(jnp.int32, q.shape, 1)
    zero = jnp.zeros_like(q)
    qs = jnp.concatenate([jnp.where(lane < DH_D, q, zero), jnp.where(lane < DH_D, zero, q)], axis=0)
    row = lax.broadcasted_iota(jnp.int32, (2 * tq, tq), 0) % tq
    col = lax.broadcasted_iota(jnp.int32, (2 * tq, tq), 1)
    visible = col <= row

    def block(j, m, l, acc, diagonal):
        start = pl.multiple_of(j * tq, tq)
        kb = k_ref[pl.ds(start, tq), :]
        vb = v_ref[pl.ds(start, tq), :]
        s = _dot(qs, kb, NT)
        if diagonal:
            s = jnp.where(visible, s, -jnp.inf)
        m_new = jnp.maximum(m, jnp.max(s, axis=-1, keepdims=True))
        alpha = jnp.exp(m - m_new)
        p = jnp.exp(s - m_new)
        l = alpha * l + jnp.sum(p, axis=-1, keepdims=True)
        acc = alpha * acc + _dot(p.astype(BF16), vb)
        return m_new, l, acc

    m0 = jnp.full((2 * tq, 1), -jnp.inf, F32)
    l0 = jnp.zeros((2 * tq, 1), F32)
    acc0 = jnp.zeros((2 * tq, DV_D), F32)
    m, l, acc = block(qi, m0, l0, acc0, True)

    def grouped(first, group):
        def body(i, state):
            for t in range(group):
                state = block(first + i * group + t, state[0], state[1], state[2], False)
            return state
        return body

    n4 = qi // 4
    n2 = (qi - 4 * n4) // 2
    n1 = qi - 4 * n4 - 2 * n2
    state = lax.fori_loop(0, n4, grouped(0, 4), (m, l, acc))
    state = lax.fori_loop(0, n2, grouped(4 * n4, 2), state)
    m, l, acc = lax.fori_loop(0, n1, grouped(4 * n4 + 2 * n2, 1), state)
    lv = lam_ref[...]
    lam = (jnp.exp(jnp.sum(lv[0:1, :] * lv[1:2, :], axis=-1, keepdims=True))
           - jnp.exp(jnp.sum(lv[2:3, :] * lv[3:4, :], axis=-1, keepdims=True)) + lam_init)
    o = acc / l
    y = o[0:tq, :] - lam * o[tq:2 * tq, :]
    ms = jnp.mean(y * y, axis=-1, keepdims=True)
    o_ref[...] = (y * lax.rsqrt(ms + LN_EPS) * g_ref[...] * (1.0 - lam_init)).astype(o_ref.dtype)


def diff_attention(lam_p, qk_rot, proj_cd, subln, batch, seq, lam_init, tq=512):
    nq = seq // tq
    return pl.pallas_call(
        functools.partial(_diff_kernel, tq=tq, lam_init=lam_init),
        grid=(batch, H_D, nq),
        in_specs=[pl.BlockSpec((4, DH_D), lambda b, h, i: (0, 0)),
                  pl.BlockSpec((None, tq, DV_D), lambda b, h, i: (b, i, h)),
                  pl.BlockSpec((None, seq, DV_D), lambda b, h, i: (b, 0, H_D + h)),
                  pl.BlockSpec((None, seq, DV_D), lambda b, h, i: (b, 0, 3 * H_C + h)),
                  pl.BlockSpec((1, DV_D), lambda b, h, i: (0, 0))],
        out_specs=pl.BlockSpec((None, tq, DV_D), lambda b, h, i: (b, i, h)),
        out_shape=jax.ShapeDtypeStruct((batch, seq, C_D), BF16),
        compiler_params=_cparams("parallel", "parallel", "arbitrary"),
        name="diff_attention",
    )(lam_p, qk_rot, qk_rot, proj_cd, subln)


def _router_kernel(h_ref, g_ref, wr_ref, n_ref, comb_ref, idx_ref):
    x = h_ref[...]
    ms = jnp.mean(x * x, axis=-1, keepdims=True)
    n = x * lax.rsqrt(ms + RMS_EPS) * g_ref[...]
    n_ref[...] = n.astype(n_ref.dtype)
    n_hi, n_lo = _split2(n)
    w_hi, w_lo = _split2(wr_ref[...])
    logits = _dot(n_hi, w_hi) + _dot(n_lo, w_hi) + _dot(n_hi, w_lo)
    lane = lax.broadcasted_iota(jnp.int32, logits.shape, 1)
    neg = -jnp.inf
    logits = jnp.where(lane < N_EXPERTS, logits, neg)
    m1 = jnp.max(logits, axis=-1, keepdims=True)
    i1 = jnp.min(jnp.where(logits == m1, lane, LANES), axis=-1, keepdims=True)
    rest = jnp.where(lane == i1, neg, logits)
    m2 = jnp.max(rest, axis=-1, keepdims=True)
    i2 = jnp.min(jnp.where(rest == m2, lane, LANES), axis=-1, keepdims=True)
    e = jnp.exp(m2 - m1)
    g1 = 1.0 / (1.0 + e)
    g2 = e / (1.0 + e)
    comb_ref[...] = jnp.where(lane == i1, g1, 0.0) + jnp.where(lane == i2, g2, 0.0)
    idx_ref[...] = jnp.where(lane == 0, i1, jnp.where(lane == 1, i2, 0))


def moe_router(h, g, w_router_pad, tm=256):
    m, d = h.shape
    spec = pl.BlockSpec((tm, LANES), lambda i: (i, 0))
    return pl.pallas_call(
        _router_kernel,
        grid=(m // tm,),
        in_specs=[pl.BlockSpec((tm, d), lambda i: (i, 0)), pl.BlockSpec((1, d), lambda i: (0, 0)),
                  pl.BlockSpec((d, LANES), lambda i: (0, 0))],
        out_specs=[pl.BlockSpec((tm, d), lambda i: (i, 0)), spec, spec],
        out_shape=[jax.ShapeDtypeStruct((m, d), BF16), jax.ShapeDtypeStruct((m, LANES), F32),
                   jax.ShapeDtypeStruct((m, LANES), jnp.int32)],
        compiler_params=_cparams("parallel"),
        name="moe_router",
    )(h, g, w_router_pad)


def _tile_state(te_ref):
    i = pl.program_id(1)
    in_use = i < te_ref[pl.num_programs(1)]
    new_weights = jnp.logical_or(i == 0, te_ref[i] != te_ref[jnp.maximum(i - 1, 0)])
    return in_use, new_weights


def _gmm_up_kernel(te_ref, a_ref, wg_ref, wu_ref, o_ref, wg_bf, wu_bf):
    in_use, new_weights = _tile_state(te_ref)

    @pl.when(new_weights)
    def _():
        wg_bf[...] = wg_ref[...].astype(BF16)
        wu_bf[...] = wu_ref[...].astype(BF16)

    @pl.when(in_use)
    def _():
        a = a_ref[...]
        g = _dot(a, wg_bf[...])
        u = _dot(a, wu_bf[...])
        o_ref[...] = (g * jax.nn.sigmoid(g) * u).astype(o_ref.dtype)

    @pl.when(jnp.logical_not(in_use))
    def _():
        o_ref[...] = jnp.zeros_like(o_ref)


def _row_spec(tm, width, n_tiles):
    return pl.BlockSpec((tm, width), lambda j, i, te: (jnp.minimum(i, te[n_tiles] - 1), 0))


def gmm_swiglu_up(tile_table, xs, wg, wu, tm, tn):
    s, k = xs.shape
    n = wg.shape[2]
    n_tiles = s // tm
    w_spec = pl.BlockSpec((None, k, tn), lambda j, i, te: (te[i], 0, j))
    return pl.pallas_call(
        _gmm_up_kernel,
        grid_spec=pltpu.PrefetchScalarGridSpec(
            num_scalar_prefetch=1,
            grid=(n // tn, n_tiles),
            in_specs=[_row_spec(tm, k, n_tiles), w_spec, w_spec],
            out_specs=pl.BlockSpec((tm, tn), lambda j, i, te: (i, j)),
            scratch_shapes=[pltpu.VMEM((k, tn), BF16), pltpu.VMEM((k, tn), BF16)],
        ),
        out_shape=jax.ShapeDtypeStruct((s, n), BF16),
        compiler_params=_cparams("arbitrary", "arbitrary"),
        name="gmm_swiglu_up",
    )(tile_table, xs, wg, wu)


def _gmm_down_kernel(te_ref, a_ref, w_ref, gate_ref, o_ref):
    in_use = pl.program_id(0) < te_ref[pl.num_programs(0)]

    @pl.when(in_use)
    def _():
        o_ref[...] = _dot(a_ref[...], w_ref[...]) * gate_ref[...]

    @pl.when(jnp.logical_not(in_use))
    def _():
        o_ref[...] = jnp.zeros_like(o_ref)


def gmm_down(tile_table, hs, wd, slot_gate, tm, tn=1024):
    s, k = hs.shape
    n = wd.shape[2]
    n_tiles = s // tm
    return pl.pallas_call(
        _gmm_down_kernel,
        grid_spec=pltpu.PrefetchScalarGridSpec(
            num_scalar_prefetch=1,
            grid=(n_tiles, n // tn),
            in_specs=[pl.BlockSpec((tm, k), lambda i, j, te: (jnp.minimum(i, te[n_tiles] - 1), 0)),
                      pl.BlockSpec((None, k, tn), lambda i, j, te: (te[i], 0, jnp.where(i < te[n_tiles], j, 0))),
                      pl.BlockSpec((tm, 1), lambda i, j, te: (jnp.minimum(i, te[n_tiles] - 1), 0))],
            out_specs=pl.BlockSpec((tm, tn), lambda i, j, te: (i, j)),
        ),
        out_shape=jax.ShapeDtypeStruct((s, n), F32),
        compiler_params=_cparams("parallel", "arbitrary"),
        name="gmm_down",
    )(tile_table, hs, wd, slot_gate)


def moe_layer(h, ffn_g, w_router, wg, wu, wd, tm=MOE_TM):
    m, d = h.shape
    w_router_pad = jnp.pad(w_router.astype(F32), ((0, 0), (0, LANES - N_EXPERTS)))
    n_b, comb, idx = moe_router(h, ffn_g.reshape(1, d).astype(F32), w_router_pad)
    top_idx = idx[:, 0:2]
    gates = jnp.take_along_axis(comb[:, 0:N_EXPERTS], top_idx, axis=1)

    flat_e = top_idx.reshape(-1)
    n_pairs = 2 * m
    onehot = (flat_e[:, None] == jnp.arange(N_EXPERTS, dtype=jnp.int32)[None, :]).astype(jnp.int32)
    counts = jnp.sum(onehot, axis=0)
    rank = jnp.sum((jnp.cumsum(onehot, axis=0) - onehot) * onehot, axis=1)
    padded = ((counts + tm - 1) // tm) * tm
    pad_off = jnp.cumsum(padded) - padded
    raw_off = jnp.cumsum(counts) - counts
    dest = pad_off[flat_e] + rank
    n_tiles = n_pairs // tm + N_EXPERTS
    n_slots = n_tiles * tm
    order = jnp.argsort(flat_e, stable=True).astype(jnp.int32)
    tile_start = jnp.arange(n_tiles, dtype=jnp.int32) * tm
    pad_end = pad_off + padded
    tile_expert = jnp.minimum(jnp.sum((tile_start[:, None] >= pad_end[None, :]).astype(jnp.int32), axis=1),
                              N_EXPERTS - 1).astype(jnp.int32)
    slot = jnp.arange(n_slots, dtype=jnp.int32)
    slot_e = jnp.repeat(tile_expert, tm)
    local = slot - pad_off[slot_e]
    valid = (local < counts[slot_e]) & (slot < pad_end[N_EXPERTS - 1])
    src_pair = order[jnp.clip(raw_off[slot_e] + local, 0, n_pairs - 1)]
    slot_token = jnp.where(valid, src_pair // 2, 0)
    slot_gate = jnp.where(valid, gates.reshape(-1)[src_pair], 0.0).astype(F32).reshape(n_slots, 1)

    tiles_in_use = (pad_end[N_EXPERTS - 1] // tm).astype(jnp.int32).reshape(1)
    tile_table = jnp.concatenate([tile_expert, tiles_in_use])

    xs = jnp.take(n_b, slot_token, axis=0, mode="clip")
    hs = gmm_swiglu_up(tile_table, xs, wg, wu, tm, min(MOE_TN, wg.shape[2]))
    ys = gmm_down(tile_table, hs, cast_experts_bf16(wd), slot_gate, tm, min(1024, wd.shape[2]))
    dest2 = dest.reshape(m, 2)
    return h + jnp.take(ys, dest2[:, 0], axis=0) + jnp.take(ys, dest2[:, 1], axis=0)


def _rwkv_columns(w_in, layer):
    d = w_in.shape[0]
    o_w = C_A
    o_k = o_w + W_LORA
    o_v = o_k + C_A
    o_a = o_v + C_A
    o_g = o_a + A_LORA
    parts = [w_in[:, 0:C_A], w_in[:, o_k:o_k + C_A], w_in[:, o_v:o_v + C_A],
             w_in[:, o_w:o_w + W_LORA], w_in[:, o_a:o_a + A_LORA], w_in[:, o_g:o_g + G_LORA]]
    used = W_LORA + A_LORA + G_LORA
    if layer > 0:
        parts.append(w_in[:, N_IN0:N_IN0 + V_LORA])
        used += V_LORA
    parts.append(jnp.zeros((d, LORA_PAD - used), w_in.dtype))
    return jnp.concatenate(parts, axis=1)


def _rwkv_mu(mu, mu_v):
    o_w = C_A
    o_k = o_w + W_LORA
    o_v = o_k + C_A
    o_a = o_v + C_A
    o_g = o_a + A_LORA
    parts = [mu[0:C_A], mu[o_k:o_k + C_A], mu[o_v:o_v + C_A],
             mu[o_w:o_w + W_LORA], mu[o_a:o_a + A_LORA], mu[o_g:o_g + G_LORA]]
    used = W_LORA + A_LORA + G_LORA
    if mu_v is not None:
        parts.append(mu_v)
        used += V_LORA
    parts.append(jnp.zeros((LORA_PAD - used,), mu.dtype))
    return jnp.concatenate(parts).reshape(1, N_A_COLS).astype(F32)


def _pad_rows(w, offset):
    return jnp.pad(w, ((offset, LORA_PAD - offset - w.shape[0]), (0, 0))).astype(BF16)


def kernel(x, p, positions, attn_norm, w_in0, w_in_rest, w_out, rwkv_mu, rwkv_mu_v, rwkv_w0, rwkv_w_up, rwkv_a0, rwkv_a_up, rwkv_v0, rwkv_v_up, rwkv_g_up, rwkv_k_k, rwkv_k_a, rwkv_r_k, rwkv_lnx_g, rwkv_lnx_b, gm_ln_g, gm_ln_b, gm_w_s, gm_b_s, gm_out_g, sb_out_g, diff_lambda, diff_subln, ffn_norm, dense_w_gate, dense_w_up, dense_w_down, moe_router, moe_w_gate, moe_w_up, moe_w_down, ple_norm, ple_w_gate, ple_w_proj, final_norm):
    batch, seq, d = x.shape
    m = batch * seq
    h = x.reshape(m, d).astype(F32)

    inv_freq = 1.0 / (ROPE_THETA ** (jnp.arange(0, DH_D, 2, dtype=F32) / DH_D))
    invf = jnp.tile(inv_freq, LANES // (DH_D // 2)).reshape(1, LANES)
    cos_t, sin_t = rope_tables(positions.reshape(m, 1), invf)

    v_first = None
    for i in range(DEPTH):
        w_in = w_in0 if i == 0 else w_in_rest[i - 1]
        o_b = N_RWKV
        o_c = o_b + N_GMLP
        o_d = o_c + N_SB
        w_a = _rwkv_columns(w_in, i).astype(BF16)
        w_bd = jnp.concatenate([w_in[:, o_b:o_c], w_in[:, o_d:o_d + 2 * C_D]], axis=1).astype(BF16)
        w_cd = jnp.concatenate([w_in[:, o_c:o_d], w_in[:, o_d + 2 * C_D:o_d + 3 * C_D]], axis=1).astype(BF16)

        n = rmsnorm(h, attn_norm[i], BF16)
        proj_a = matmul(n, w_a, F32)
        proj_bd = matmul(n, w_bd, F32)
        proj_cd = matmul(n, w_cd, BF16)

        mu = _rwkv_mu(rwkv_mu[i], rwkv_mu_v[i - 1] if i > 0 else None)
        zeros_c = jnp.zeros((C_A,), F32)
        vec = jnp.stack([rwkv_w0[i], rwkv_a0[i], rwkv_v0[i - 1] if i > 0 else zeros_c, rwkv_k_k[i], rwkv_k_a[i],
                         rwkv_r_k[i].reshape(C_A), rwkv_lnx_g[i], rwkv_lnx_b[i]]).astype(F32)
        uw = _pad_rows(rwkv_w_up[i], 0)
        ua = _pad_rows(rwkv_a_up[i], W_LORA)
        ug = _pad_rows(rwkv_g_up[i], W_LORA + A_LORA)
        uv = _pad_rows(rwkv_v_up[i - 1], W_LORA + A_LORA + G_LORA) if i > 0 else jnp.zeros((LORA_PAD, C_A), BF16)
        vf_in = v_first if i > 0 else proj_a
        y_a, v_out = rwkv_mix(proj_a, vf_in, mu, vec, uw, ua, ug, uv, batch, seq, has_v_res=i > 0)
        if i == 0:
            v_first = v_out

        y_b = gmlp_mix(proj_bd, gm_ln_g[i].reshape(1, C_B), gm_ln_b[i].reshape(1, C_B), gm_w_s[i],
                       gm_b_s[i].T, gm_out_g[i].reshape(1, C_B))

        proj_cd3 = proj_cd.reshape(batch, seq, 4 * C_C)
        y_c = sb_attention(proj_cd3, sb_out_g[i].reshape(1, C_C), batch, seq).reshape(m, C_C)

        qk_rot = rope_qk(proj_bd, cos_t, sin_t).reshape(batch, seq, 2 * C_D)
        lam_init = 0.8 - 0.6 * math.exp(-0.3 * i)
        y_d = diff_attention(diff_lambda[i].astype(F32), qk_rot, proj_cd3, diff_subln[i].reshape(1, DV_D),
                             batch, seq, lam_init).reshape(m, C_D)

        h = mix_out_proj((y_a, y_b, y_c, y_d), w_out[i].astype(BF16), h)

        j = i // 2
        if i % 2 == 0:
            n = rmsnorm(h, ffn_norm[i], BF16)
            n_row_tiles = m // DENSE_TM
            one_group = jnp.concatenate([jnp.zeros((n_row_tiles,), jnp.int32), jnp.full((1,), n_row_tiles, jnp.int32)])
            hid = gmm_swiglu_up(one_group, n, dense_w_gate[j][None], dense_w_up[j][None], DENSE_TM, DENSE_TN)
            h = down_proj_residual(hid, dense_w_down[j].astype(BF16), h)
        else:
            h = moe_layer(h, ffn_norm[i], moe_router[j], moe_w_gate[j], moe_w_up[j], moe_w_down[j])

        n = rmsnorm(h, ple_norm[i], BF16)
        h = ple_residual(n, ple_w_gate[i].astype(BF16), p[i].reshape(m, P_DIM).astype(BF16),
                         ple_w_proj[i].astype(BF16), h)

    return rmsnorm(h, final_norm, x.dtype).reshape(batch, seq, d)
```

```python
import functools
import math

import jax
import jax.numpy as jnp
import numpy as np
from jax import lax
from jax.experimental import pallas as pl
from jax.experimental.pallas import tpu as pltpu

F32 = jnp.float32
BF16 = jnp.bfloat16

D_MODEL = 4096
DEPTH = 2
C_A = 1024
N_A = 64
W_LORA, A_LORA, V_LORA, G_LORA = 64, 64, 32, 160
C_B = 1024
H_B = 8
CHUNK = 128
C_C = 1024
H_C = 8
D_C = 128
C_D = 1024
H_D = 8
DV_D = 128
DH_D = 64
D_FF = 11008
N_EXPERTS = 8
D_FF_EXPERT = 5632
P_DIM = 256
ROPE_THETA = 10000.0
RMS_EPS = 1e-6
LN_EPS = 1e-5
GN_EPS = 64e-5
L2_EPS = 1e-12
N_RWKV = 3 * C_A + W_LORA + A_LORA + G_LORA
N_GMLP = 2 * C_B
N_SB = 3 * C_C
N_DIFF = 3 * C_D
N_IN0 = N_RWKV + N_GMLP + N_SB + N_DIFF

LANES = 128
VMEM_LIMIT = 56 * 1024 * 1024

LORA_PAD = 512
N_A_COLS = 3 * C_A + LORA_PAD
DENSE_TM, DENSE_TN = 1024, 256
DENSE_TK = 5504
MOE_TM, MOE_TN = 512, 512
RWKV_CHUNK = 64
RWKV_GROUP = 2
SB_CUTOFF = -110.0
SB_HEADS = 4

NT = (((1,), (1,)), ((), ()))
TN = (((0,), (0,)), ((), ()))


def _cparams(*sem):
    return pltpu.CompilerParams(dimension_semantics=sem, vmem_limit_bytes=VMEM_LIMIT)


def _dot(a, b, dims=None):
    if dims is None:
        return jnp.dot(a, b, preferred_element_type=F32)
    return lax.dot_general(a, b, dims, preferred_element_type=F32)


def _split2(x):
    hi = x.astype(BF16)
    lo = (x - hi.astype(F32)).astype(BF16)
    return hi, lo


def _split3(x):
    hi = x.astype(BF16)
    r = x - hi.astype(F32)
    mid = r.astype(BF16)
    lo = (r - mid.astype(F32)).astype(BF16)
    return hi, mid, lo


def _pick_tile(n, prefs):
    for t in prefs:
        if n % t == 0:
            return t
    raise ValueError(f"no tile for {n}")


def _rmsnorm_kernel(x_ref, g_ref, o_ref, *, eps):
    x = x_ref[...].astype(F32)
    ms = jnp.mean(x * x, axis=-1, keepdims=True)
    o_ref[...] = (x * lax.rsqrt(ms + eps) * g_ref[...]).astype(o_ref.dtype)


def rmsnorm(x, g, out_dtype, tm=512):
    m, d = x.shape
    return pl.pallas_call(
        functools.partial(_rmsnorm_kernel, eps=RMS_EPS),
        grid=(m // tm,),
        in_specs=[pl.BlockSpec((tm, d), lambda i: (i, 0)), pl.BlockSpec((1, d), lambda i: (0, 0))],
        out_specs=pl.BlockSpec((tm, d), lambda i: (i, 0)),
        out_shape=jax.ShapeDtypeStruct((m, d), out_dtype),
        compiler_params=_cparams("parallel"),
        name="rmsnorm",
    )(x, g.reshape(1, d).astype(F32))


def _cast_kernel(x_ref, o_ref):
    o_ref[...] = x_ref[...].astype(o_ref.dtype)


def cast_experts_bf16(w, rows=512):
    e, k, n = w.shape
    rows = min(rows, k)
    spec = pl.BlockSpec((None, rows, n), lambda i, j: (i, j, 0))
    return pl.pallas_call(
        _cast_kernel,
        grid=(e, k // rows),
        in_specs=[spec],
        out_specs=spec,
        out_shape=jax.ShapeDtypeStruct(w.shape, BF16),
        compiler_params=_cparams("parallel", "parallel"),
        name="cast_experts_bf16",
    )(w)


def _mm_kernel(a_ref, b_ref, o_ref):
    o_ref[...] = _dot(a_ref[...], b_ref[...]).astype(o_ref.dtype)


def matmul(a, b, out_dtype, tm=1024):
    m, k = a.shape
    n = b.shape[1]
    tn = _pick_tile(n, (1024, 512, 256, 128))
    return pl.pallas_call(
        _mm_kernel,
        grid=(m // tm, n // tn),
        in_specs=[pl.BlockSpec((tm, k), lambda i, j: (i, 0)), pl.BlockSpec((k, tn), lambda i, j: (0, j))],
        out_specs=pl.BlockSpec((tm, tn), lambda i, j: (i, j)),
        out_shape=jax.ShapeDtypeStruct((m, n), out_dtype),
        compiler_params=_cparams("parallel", "arbitrary"),
        name="matmul",
    )(a, b)


def _mix_out_kernel(ya_ref, yb_ref, yc_ref, yd_ref, w_ref, res_ref, o_ref):
    kq = ya_ref.shape[1]
    acc = res_ref[...]
    for idx, y_ref in enumerate((ya_ref, yb_ref, yc_ref, yd_ref)):
        acc = acc + _dot(y_ref[...], w_ref[idx * kq:(idx + 1) * kq, :])
    o_ref[...] = acc


def mix_out_proj(ys, w, res, tm=1024, tn=512):
    m, kq = ys[0].shape
    k, n = w.shape
    y_spec = pl.BlockSpec((tm, kq), lambda i, j: (i, 0))
    return pl.pallas_call(
        _mix_out_kernel,
        grid=(m // tm, n // tn),
        in_specs=[y_spec, y_spec, y_spec, y_spec,
                  pl.BlockSpec((k, tn), lambda i, j: (0, j)),
                  pl.BlockSpec((tm, tn), lambda i, j: (i, j))],
        out_specs=pl.BlockSpec((tm, tn), lambda i, j: (i, j)),
        out_shape=jax.ShapeDtypeStruct((m, n), F32),
        compiler_params=_cparams("parallel", "arbitrary"),
        name="mix_out_proj",
    )(*ys, w, res)


def _down_res_kernel(a_ref, b_ref, res_ref, o_ref, acc_ref):
    kk = pl.program_id(2)

    @pl.when(kk == 0)
    def _():
        acc_ref[...] = res_ref[...]

    acc_ref[...] += _dot(a_ref[...], b_ref[...])

    @pl.when(kk == pl.num_programs(2) - 1)
    def _():
        o_ref[...] = acc_ref[...]


def down_proj_residual(a, b, res, tm=1024, tn=512, tk=DENSE_TK):
    m, k = a.shape
    n = b.shape[1]
    return pl.pallas_call(
        _down_res_kernel,
        grid=(m // tm, n // tn, k // tk),
        in_specs=[pl.BlockSpec((tm, tk), lambda i, j, kk: (i, kk)),
                  pl.BlockSpec((tk, tn), lambda i, j, kk: (kk, j)),
                  pl.BlockSpec((tm, tn), lambda i, j, kk: (i, j))],
        out_specs=pl.BlockSpec((tm, tn), lambda i, j, kk: (i, j)),
        out_shape=jax.ShapeDtypeStruct((m, n), F32),
        scratch_shapes=[pltpu.VMEM((tm, tn), F32)],
        compiler_params=_cparams("parallel", "parallel", "arbitrary"),
        name="down_proj_residual",
    )(a, b, res)


def _ple_kernel(n_ref, wg_ref, p_ref, wp_ref, res_ref, o_ref):
    gate = jax.nn.sigmoid(_dot(n_ref[...], wg_ref[...]))
    emb = _dot(p_ref[...], wp_ref[...])
    o_ref[...] = res_ref[...] + gate * emb


def ple_residual(n, wg, p, wp, res, tm=1024, tn=512):
    m, k = n.shape
    nn = wg.shape[1]
    pd = p.shape[1]
    return pl.pallas_call(
        _ple_kernel,
        grid=(m // tm, nn // tn),
        in_specs=[pl.BlockSpec((tm, k), lambda i, j: (i, 0)),
                  pl.BlockSpec((k, tn), lambda i, j: (0, j)),
                  pl.BlockSpec((tm, pd), lambda i, j: (i, 0)),
                  pl.BlockSpec((pd, tn), lambda i, j: (0, j)),
                  pl.BlockSpec((tm, tn), lambda i, j: (i, j))],
        out_specs=pl.BlockSpec((tm, tn), lambda i, j: (i, j)),
        out_shape=jax.ShapeDtypeStruct((m, nn), F32),
        compiler_params=_cparams("parallel", "arbitrary"),
        name="ple_residual",
    )(n, wg, p, wp, res)


def _head_sum(x, ones_bd, split=False):
    outs = []
    for p in range(x.shape[1] // LANES):
        xp = x[:, p * LANES:(p + 1) * LANES]
        if split:
            hi, lo = _split2(xp)
            outs.append(_dot(hi, ones_bd) + _dot(lo, ones_bd))
        else:
            outs.append(_dot(xp.astype(BF16), ones_bd))
    return jnp.concatenate(outs, axis=1)


def _rwkv_kernel(z_ref, vf_ref, mu_ref, vec_ref, uw_ref, ua_ref, ug_ref, uv_ref,
                 y_ref, vout_ref, prev_ref, h_ref, *, has_v_res):
    tc = RWKV_CHUNK
    tr = z_ref.shape[0]
    chunks = range(tr // tc)
    c = pl.program_id(1)

    @pl.when(c == 0)
    def _():
        prev_ref[...] = jnp.zeros_like(prev_ref)
        h_ref[...] = jnp.zeros_like(h_ref)

    z = z_ref[...]
    row = lax.broadcasted_iota(jnp.int32, z.shape, 0)
    zp = jnp.where(row == 0, prev_ref[...], pltpu.roll(z, 1, 0))
    prev_ref[...] = z[tr - 1:tr, :]
    zs = z + (zp - z) * mu_ref[...]
    r = zs[:, 0:C_A]
    k = zs[:, C_A:2 * C_A]
    v = zs[:, 2 * C_A:3 * C_A]
    lr = zs[:, 3 * C_A:3 * C_A + LORA_PAD]

    w0 = vec_ref[0:1, :]
    a0 = vec_ref[1:2, :]
    v0 = vec_ref[2:3, :]
    k_k = vec_ref[3:4, :]
    k_a = vec_ref[4:5, :]
    r_k = vec_ref[5:6, :]
    lnx_g = vec_ref[6:7, :]
    lnx_b = vec_ref[7:8, :]

    lr_b = lr.astype(BF16)
    w_lin = _dot(jnp.tanh(lr).astype(BF16), uw_ref[...])
    a_lin = _dot(lr_b, ua_ref[...])
    g = _dot(jax.nn.sigmoid(lr).astype(BF16), ug_ref[...])
    w = -jax.nn.softplus(-(w0 + w_lin)) - 0.5
    ld = -jnp.exp(w)
    a_lr = jax.nn.sigmoid(a0 + a_lin)
    if has_v_res:
        v_lin = _dot(lr_b, uv_ref[...])
        v = v + (vf_ref[...] - v) * jax.nn.sigmoid(v0 + v_lin)
    vout_ref[...] = v

    li = lax.broadcasted_iota(jnp.int32, (LANES, LANES), 0)
    lj = lax.broadcasted_iota(jnp.int32, (LANES, LANES), 1)
    same_head = (li // N_A) == (lj // N_A)
    ones_bd = jnp.where(same_head, 1.0, 0.0).astype(BF16)

    kk = k * k_k
    kk = kk / jnp.maximum(jnp.sqrt(_head_sum(kk * kk, ones_bd, split=True)), L2_EPS)
    k = k * (1.0 + (a_lr - 1.0) * k_a)
    a_s = -kk
    b_s = kk * a_lr

    ti = lax.broadcasted_iota(jnp.int32, (tr, tr), 0)
    tj = lax.broadcasted_iota(jnp.int32, (tr, tr), 1)
    tri = jnp.where(((ti // tc) == (tj // tc)) & (ti >= tj), 1.0, 0.0).astype(BF16)
    ld_h, ld_m, ld_l = _split3(ld)
    cum = _dot(tri, ld_h) + _dot(tri, ld_m) + _dot(tri, ld_l)
    cum_last = [cum[(ci + 1) * tc - 1:(ci + 1) * tc, :] for ci in chunks]
    cum_end = jnp.concatenate([jnp.broadcast_to(x, (tc, C_A)) for x in cum_last], axis=0)
    p_inc = jnp.exp(cum)
    p_exc = jnp.exp(cum - ld)
    p_inv = jnp.exp(-cum)
    p_end = jnp.exp(cum_end - cum)
    decay_end = [jnp.exp(x) for x in cum_last]

    a_t = a_s * p_exc
    b_t = b_s * p_inv
    k_t = k * p_inv
    r_t = r * p_inc
    b_h = b_s * p_end
    k_h = k * p_end

    first_head = lax.broadcasted_iota(jnp.int32, (tc, LANES), 1) < N_A
    strict = same_head & ((li % N_A) > (lj % N_A))
    incl = same_head & ((li % N_A) >= (lj % N_A))
    eye = li == lj

    def stack(x):
        return jnp.concatenate([jnp.where(first_head, x, 0.0), jnp.where(first_head, 0.0, x)], axis=0)

    def dup(x):
        return jnp.concatenate([x, x], axis=0)

    n_pairs = C_A // LANES
    chains = [(ci, p) for ci in chunks for p in range(n_pairs)]
    pairs = range(len(chains))
    cut = [(slice(ci * tc, (ci + 1) * tc), slice(p * LANES, (p + 1) * LANES)) for ci, p in chains]
    n2 = 2 * tc
    a_st = [stack(a_t[rs, sl]) for rs, sl in cut]
    r_st = [stack(r_t[rs, sl]) for rs, sl in cut]
    v_st = [stack(v[rs, sl]).astype(BF16) for rs, sl in cut]
    bh_st = [stack(b_h[rs, sl]).astype(BF16) for rs, sl in cut]
    kh_st = [stack(k_h[rs, sl]).astype(BF16) for rs, sl in cut]
    lhs = [jnp.concatenate([a_st[p], r_st[p]], axis=0).astype(BF16) for p in pairs]
    rhs = [jnp.concatenate([dup(b_t[rs, sl]), dup(k_t[rs, sl])], axis=0).astype(BF16) for rs, sl in cut]
    x = [_dot(lhs[p], rhs[p], NT) for p in pairs]
    lk = [jnp.where(strict, x[p][0:n2, 0:n2], 0.0) for p in pairs]
    a_ak = [jnp.where(strict, x[p][0:n2, n2:2 * n2], 0.0).astype(BF16) for p in pairs]
    a_rb = [jnp.where(incl, x[p][n2:2 * n2, 0:n2], 0.0).astype(BF16) for p in pairs]
    a_rk = [jnp.where(incl, x[p][n2:2 * n2, n2:2 * n2], 0.0).astype(BF16) for p in pairs]
    zz = [jnp.concatenate([a_st[p], _dot(a_ak[p], v_st[p])], axis=1) for p in pairs]
    n_iter = int(math.log2(tc))
    for it in range(n_iter):
        lk_b = [lk[p].astype(BF16) for p in pairs]
        zz = [zz[p] + _dot(lk_b[p], zz[p].astype(BF16)) for p in pairs]
        if it < n_iter - 1:
            lk = [_dot(lk_b[p], lk_b[p]) for p in pairs]
    zz_b = [zz[p].astype(BF16) for p in pairs]
    qy = [_dot(a_rb[p], zz_b[p]) for p in pairs]
    y3 = [qy[p][:, LANES:] + _dot(a_rk[p], v_st[p]) for p in pairs]
    mg = [_dot(bh_st[p], zz_b[p], TN) for p in pairs]
    g_mat = [mg[p][:, LANES:] + _dot(kh_st[p], v_st[p], TN) for p in pairs]
    qm = [jnp.concatenate([r_st[p] + qy[p][:, 0:LANES],
                           jnp.where(eye, decay_end[chains[p][0]][:, cut[p][1]], 0.0) + mg[p][:, 0:LANES]],
                          axis=0).astype(BF16) for p in pairs]
    state = [h_ref[p] for p in range(n_pairs)]
    y_rows = []
    for ci in chunks:
        first = ci * n_pairs
        out = [_dot(qm[first + p], state[p].astype(BF16)) for p in range(n_pairs)]
        y_parts = []
        for p in range(n_pairs):
            y_st = out[p][0:n2, :] + y3[first + p]
            y_parts.append(y_st[0:tc, :] + y_st[tc:n2, :])
            state[p] = out[p][n2:, :] + g_mat[first + p]
        y_rows.append(jnp.concatenate(y_parts, axis=1))
    for p in range(n_pairs):
        h_ref[p] = state[p]
    y = jnp.concatenate(y_rows, axis=0)

    inv_n = 1.0 / N_A
    mean = _head_sum(y, ones_bd) * inv_n
    d = y - mean
    var = _head_sum(d * d, ones_bd) * inv_n
    yn = d * lax.rsqrt(var + GN_EPS) * lnx_g + lnx_b
    bonus = _head_sum(r * k * r_k, ones_bd) * v
    y_ref[...] = ((yn + bonus) * g).astype(y_ref.dtype)


def rwkv_mix(proj_a, v_first, mu, vec, uw, ua, ug, uv, batch, seq, has_v_res):
    tr = RWKV_CHUNK * RWKV_GROUP
    nc = seq // tr
    row_spec = lambda w: pl.BlockSpec((tr, w), lambda b, c: (b * nc + c, 0))
    full = lambda s: pl.BlockSpec(s, lambda b, c: tuple(0 for _ in s))
    m = batch * seq
    return pl.pallas_call(
        functools.partial(_rwkv_kernel, has_v_res=has_v_res),
        grid=(batch, nc),
        in_specs=[row_spec(N_A_COLS), row_spec(C_A), full((1, N_A_COLS)), full((8, C_A)),
                  full((LORA_PAD, C_A)), full((LORA_PAD, C_A)), full((LORA_PAD, C_A)), full((LORA_PAD, C_A))],
        out_specs=[row_spec(C_A), row_spec(C_A)],
        out_shape=[jax.ShapeDtypeStruct((m, C_A), BF16), jax.ShapeDtypeStruct((m, C_A), F32)],
        scratch_shapes=[pltpu.VMEM((1, N_A_COLS), F32), pltpu.VMEM((C_A // LANES, LANES, LANES), F32)],
        compiler_params=_cparams("parallel", "arbitrary"),
        name="rwkv_mix",
    )(proj_a, v_first, mu, vec, uw, ua, ug, uv)


def _gelu(x):
    return 0.5 * x * (1.0 + lax.erf(x * math.sqrt(0.5)))


def _gmlp_kernel(z_ref, lng_ref, lnb_ref, w_ref, bs_ref, og_ref, o_ref):
    u = _gelu(z_ref[:, 0:C_B])
    v = _gelu(z_ref[:, C_B:2 * C_B])
    mu = jnp.mean(v, axis=-1, keepdims=True)
    d = v - mu
    var = jnp.mean(d * d, axis=-1, keepdims=True)
    vn = d * lax.rsqrt(var + LN_EPS) * lng_ref[...] + lnb_ref[...]
    ti = lax.broadcasted_iota(jnp.int32, (CHUNK, CHUNK), 0)
    tj = lax.broadcasted_iota(jnp.int32, (CHUNK, CHUNK), 1)
    causal = ti >= tj
    dh = C_B // H_B
    for h in range(H_B):
        sl = slice(h * dh, (h + 1) * dh)
        w = jnp.where(causal, w_ref[h], 0.0).astype(BF16)
        s = _dot(w, vn[:, sl].astype(BF16)) + bs_ref[:, h:h + 1]
        y = u[:, sl] * s
        ms = jnp.mean(y * y, axis=-1, keepdims=True)
        o_ref[:, sl] = (y * lax.rsqrt(ms + RMS_EPS) * og_ref[:, sl]).astype(o_ref.dtype)


def gmlp_mix(proj_bd, ln_g, ln_b, w_s, b_s_t, out_g):
    m = proj_bd.shape[0]
    full = lambda s: pl.BlockSpec(s, lambda i: tuple(0 for _ in s))
    return pl.pallas_call(
        _gmlp_kernel,
        grid=(m // CHUNK,),
        in_specs=[pl.BlockSpec((CHUNK, 2 * C_B), lambda i: (i, 0)),
                  full((1, C_B)), full((1, C_B)), full((H_B, CHUNK, CHUNK)), full((CHUNK, H_B)), full((1, C_B))],
        out_specs=pl.BlockSpec((CHUNK, C_B), lambda i: (i, 0)),
        out_shape=jax.ShapeDtypeStruct((m, C_B), BF16),
        compiler_params=_cparams("parallel"),
        name="gmlp_mix",
    )(proj_bd, ln_g, ln_b, w_s, b_s_t, out_g)


def _sb_kernel(q_ref, k_ref, v_ref, g_ref, o_ref, *, tq, scale):
    qi = pl.program_id(2)
    heads = range(q_ref.shape[1] // D_C)
    hs = [slice(h * D_C, (h + 1) * D_C) for h in heads]
    q = [q_ref[:, s] for s in hs]
    row = lax.broadcasted_iota(jnp.int32, (tq, tq), 0)
    col = lax.broadcasted_iota(jnp.int32, (tq, tq), 1)
    below = col < row
    upper = jnp.where(row > col, 1.0, 0.0).astype(BF16)

    def block(j, carry, acc, keep):
        start = pl.multiple_of(j * tq, tq)
        z = [_dot(q[h], k_ref[pl.ds(start, tq), hs[h]], NT) * scale for h in heads]
        log_beta = [jnp.minimum(z[h], 0.0) - jnp.log(1.0 + jnp.exp(-jnp.abs(z[h]))) for h in heads]
        log_1mb = [log_beta[h] - z[h] for h in heads]
        if keep is not None:
            log_1mb = [jnp.where(keep, x, 0.0) for x in log_1mb]
        parts = [_split2(x) for x in log_1mb]
        after = [_dot(hi, upper) + _dot(lo, upper) for hi, lo in parts]
        att = [jnp.exp(log_beta[h] + after[h] + carry[h]) for h in heads]
        if keep is not None:
            att = [jnp.where(keep, x, 0.0) for x in att]
        acc = [acc[h] + _dot(att[h].astype(BF16), v_ref[pl.ds(start, tq), hs[h]]) for h in heads]
        carry = [carry[h] + after[h][:, 0:1] + log_1mb[h][:, 0:1] for h in heads]
        return carry, acc

    carry = [jnp.zeros((tq, 1), F32) for _ in heads]
    acc = [jnp.zeros((tq, D_C), F32) for _ in heads]
    carry, acc = block(qi, carry, acc, below)
    has_prev = jnp.broadcast_to(qi >= 1, (tq, tq))
    carry, acc = block(jnp.maximum(qi - 1, 0), carry, acc, has_prev)

    def more(state):
        top = functools.reduce(jnp.maximum, [jnp.max(c) for c in state[1]])
        return jnp.logical_and(state[0] >= 0, top > SB_CUTOFF)

    def body(state):
        carry_j, acc_j = block(state[0], list(state[1]), list(state[2]), None)
        return state[0] - 1, tuple(carry_j), tuple(acc_j)

    _, carry, acc = lax.while_loop(more, body, (qi - 2, tuple(carry), tuple(acc)))
    for h in heads:
        ms = jnp.mean(acc[h] * acc[h], axis=-1, keepdims=True)
        o_ref[:, hs[h]] = (acc[h] * lax.rsqrt(ms + RMS_EPS) * g_ref[:, hs[h]]).astype(o_ref.dtype)


def sb_attention(proj_cd, out_g, batch, seq, tq=256, heads=SB_HEADS):
    nq = seq // tq
    w = heads * D_C
    groups = H_C // heads
    return pl.pallas_call(
        functools.partial(_sb_kernel, tq=tq, scale=D_C ** -0.5),
        grid=(batch, groups, nq),
        in_specs=[pl.BlockSpec((None, tq, w), lambda b, h, i: (b, i, h)),
                  pl.BlockSpec((None, seq, w), lambda b, h, i: (b, 0, groups + h)),
                  pl.BlockSpec((None, seq, w), lambda b, h, i: (b, 0, 2 * groups + h)),
                  pl.BlockSpec((1, w), lambda b, h, i: (0, h))],
        out_specs=pl.BlockSpec((None, tq, w), lambda b, h, i: (b, i, h)),
        out_shape=jax.ShapeDtypeStruct((batch, seq, C_C), BF16),
        compiler_params=_cparams("parallel", "parallel", "arbitrary"),
        name="sb_attention",
    )(proj_cd, proj_cd, proj_cd, out_g)


def _rope_table_kernel(pos_ref, invf_ref, cos_ref, sin_ref):
    ang = pos_ref[...].astype(F32) * invf_ref[...]
    cos_ref[...] = jnp.cos(ang)
    sin_ref[...] = jnp.sin(ang)


def rope_tables(pos_col, invf, tm=512):
    m = pos_col.shape[0]
    spec = pl.BlockSpec((tm, LANES), lambda i: (i, 0))
    return pl.pallas_call(
        _rope_table_kernel,
        grid=(m // tm,),
        in_specs=[pl.BlockSpec((tm, 1), lambda i: (i, 0)), pl.BlockSpec((1, LANES), lambda i: (0, 0))],
        out_specs=[spec, spec],
        out_shape=[jax.ShapeDtypeStruct((m, LANES), F32)] * 2,
        compiler_params=_cparams("parallel"),
        name="rope_tables",
    )(pos_col, invf)


def _rope_kernel(x_ref, cos_ref, sin_ref, o_ref):
    x = x_ref[...]
    width = x.shape[1]
    reps = width // LANES
    c = jnp.concatenate([cos_ref[...]] * reps, axis=1)
    s = jnp.concatenate([sin_ref[...]] * reps, axis=1)
    lane = lax.broadcasted_iota(jnp.int32, x.shape, 1)
    first_half = (lane % DH_D) < (DH_D // 2)
    rot = jnp.where(first_half, -pltpu.roll(x, width - DH_D // 2, 1), pltpu.roll(x, DH_D // 2, 1))
    o_ref[...] = (x * c + rot * s).astype(o_ref.dtype)


def rope_qk(proj_bd, cos_t, sin_t, tm=256):
    m = proj_bd.shape[0]
    width = 2 * C_D
    return pl.pallas_call(
        _rope_kernel,
        grid=(m // tm,),
        in_specs=[pl.BlockSpec((tm, width), lambda i: (i, 1)),
                  pl.BlockSpec((tm, LANES), lambda i: (i, 0)),
                  pl.BlockSpec((tm, LANES), lambda i: (i, 0))],
        out_specs=pl.BlockSpec((tm, width), lambda i: (i, 0)),
        out_shape=jax.ShapeDtypeStruct((m, width), BF16),
        compiler_params=_cparams("parallel"),
        name="rope_qk",
    )(proj_bd, cos_t, sin_t)


def _diff_kernel(lam_ref, q_ref, k_ref, v_ref, g_ref, o_ref, *, tq, lam_init):
    qi = pl.program_id(2)
    q = (q_ref[...].astype(F32) * (DH_D ** -0.5)).astype(BF16)
    lane = lax.broadcasted_iota(jnp.int32, q.shape, 1)
    zero = jnp.zeros_like(q)
    qs = jnp.concatenate([jnp.where(lane < DH_D, q, zero), jnp.where(lane < DH_D, zero, q)], axis=0)
    row = lax.broadcasted_iota(jnp.int32, (2 * tq, tq), 0) % tq
    col = lax.broadcasted_iota(jnp.int32, (2 * tq, tq), 1)
    visible = col <= row

    def block(j, m, l, acc, diagonal):
        start = pl.multiple_of(j * tq, tq)
        kb = k_ref[pl.ds(start, tq), :]
        vb = v_ref[pl.ds(start, tq), :]
        s = _dot(qs, kb, NT)
        if diagonal:
            s = jnp.where(visible, s, -jnp.inf)
        m_new = jnp.maximum(m, jnp.max(s, axis=-1, keepdims=True))
        alpha = jnp.exp(m - m_new)
        p = jnp.exp(s - m_new)
        l = alpha * l + jnp.sum(p, axis=-1, keepdims=True)
        acc = alpha * acc + _dot(p.astype(BF16), vb)
        return m_new, l, acc

    m0 = jnp.full((2 * tq, 1), -jnp.inf, F32)
    l0 = jnp.zeros((2 * tq, 1), F32)
    acc0 = jnp.zeros((2 * tq, DV_D), F32)
    m, l, acc = block(qi, m0, l0, acc0, True)

    def grouped(first, group):
        def body(i, state):
            for t in range(group):
                state = block(first + i * group + t, state[0], state[1], state[2], False)
            return state
        return body

    n4 = qi // 4
    n2 = (qi - 4 * n4) // 2
    n1 = qi - 4 * n4 - 2 * n2
    state = lax.fori_loop(0, n4, grouped(0, 4), (m, l, acc))
    state = lax.fori_loop(0, n2, grouped(4 * n4, 2), state)
    m, l, acc = lax.fori_loop(0, n1, grouped(4 * n4 + 2 * n2, 1), state)
    lv = lam_ref[...]
    lam = (jnp.exp(jnp.sum(lv[0:1, :] * lv[1:2, :], axis=-1, keepdims=True))
           - jnp.exp(jnp.sum(lv[2:3, :] * lv[3:4, :], axis=-1, keepdims=True)) + lam_init)
    o = acc / l
    y = o[0:tq, :] - lam * o[tq:2 * tq, :]
    ms = jnp.mean(y * y, axis=-1, keepdims=True)
    o_ref[...] = (y * lax.rsqrt(ms + LN_EPS) * g_ref[...] * (1.0 - lam_init)).astype(o_ref.dtype)


def diff_attention(lam_p, qk_rot, proj_cd, subln, batch, seq, lam_init, tq=512):
    nq = seq // tq
    return pl.pallas_call(
        functools.partial(_diff_kernel, tq=tq, lam_init=lam_init),
        grid=(batch, H_D, nq),
        in_specs=[pl.BlockSpec((4, DH_D), lambda b, h, i: (0, 0)),
                  pl.BlockSpec((None, tq, DV_D), lambda b, h, i: (b, i, h)),
                  pl.BlockSpec((None, seq, DV_D), lambda b, h, i: (b, 0, H_D + h)),
                  pl.BlockSpec((None, seq, DV_D), lambda b, h, i: (b, 0, 3 * H_C + h)),
                  pl.BlockSpec((1, DV_D), lambda b, h, i: (0, 0))],
        out_specs=pl.BlockSpec((None, tq, DV_D), lambda b, h, i: (b, i, h)),
        out_shape=jax.ShapeDtypeStruct((batch, seq, C_D), BF16),
        compiler_params=_cparams("parallel", "parallel", "arbitrary"),
        name="diff_attention",
    )(lam_p, qk_rot, qk_rot, proj_cd, subln)


def _router_kernel(h_ref, g_ref, wr_ref, n_ref, comb_ref, idx_ref):
    x = h_ref[...]
    ms = jnp.mean(x * x, axis=-1, keepdims=True)
    n = x * lax.rsqrt(ms + RMS_EPS) * g_ref[...]
    n_ref[...] = n.astype(n_ref.dtype)
    n_hi, n_lo = _split2(n)
    w_hi, w_lo = _split2(wr_ref[...])
    logits = _dot(n_hi, w_hi) + _dot(n_lo, w_hi) + _dot(n_hi, w_lo)
    lane = lax.broadcasted_iota(jnp.int32, logits.shape, 1)
    neg = -jnp.inf
    logits = jnp.where(lane < N_EXPERTS, logits, neg)
    m1 = jnp.max(logits, axis=-1, keepdims=True)
    i1 = jnp.min(jnp.where(logits == m1, lane, LANES), axis=-1, keepdims=True)
    rest = jnp.where(lane == i1, neg, logits)
    m2 = jnp.max(rest, axis=-1, keepdims=True)
    i2 = jnp.min(jnp.where(rest == m2, lane, LANES), axis=-1, keepdims=True)
    e = jnp.exp(m2 - m1)
    g1 = 1.0 / (1.0 + e)
    g2 = e / (1.0 + e)
    comb_ref[...] = jnp.where(lane == i1, g1, 0.0) + jnp.where(lane == i2, g2, 0.0)
    idx_ref[...] = jnp.where(lane == 0, i1, jnp.where(lane == 1, i2, 0))


def moe_router(h, g, w_router_pad, tm=256):
    m, d = h.shape
    spec = pl.BlockSpec((tm, LANES), lambda i: (i, 0))
    return pl.pallas_call(
        _router_kernel,
        grid=(m // tm,),
        in_specs=[pl.BlockSpec((tm, d), lambda i: (i, 0)), pl.BlockSpec((1, d), lambda i: (0, 0)),
                  pl.BlockSpec((d, LANES), lambda i: (0, 0))],
        out_specs=[pl.BlockSpec((tm, d), lambda i: (i, 0)), spec, spec],
        out_shape=[jax.ShapeDtypeStruct((m, d), BF16), jax.ShapeDtypeStruct((m, LANES), F32),
                   jax.ShapeDtypeStruct((m, LANES), jnp.int32)],
        compiler_params=_cparams("parallel"),
        name="moe_router",
    )(h, g, w_router_pad)


def _tile_state(te_ref):
    i = pl.program_id(1)
    in_use = i < te_ref[pl.num_programs(1)]
    new_weights = jnp.logical_or(i == 0, te_ref[i] != te_ref[jnp.maximum(i - 1, 0)])
    return in_use, new_weights


def _gmm_up_kernel(te_ref, a_ref, wg_ref, wu_ref, o_ref, wg_bf, wu_bf):
    in_use, new_weights = _tile_state(te_ref)

    @pl.when(new_weights)
    def _():
        wg_bf[...] = wg_ref[...].astype(BF16)
        wu_bf[...] = wu_ref[...].astype(BF16)

    @pl.when(in_use)
    def _():
        a = a_ref[...]
        g = _dot(a, wg_bf[...])
        u = _dot(a, wu_bf[...])
        o_ref[...] = (g * jax.nn.sigmoid(g) * u).astype(o_ref.dtype)

    @pl.when(jnp.logical_not(in_use))
    def _():
        o_ref[...] = jnp.zeros_like(o_ref)


def _row_spec(tm, width, n_tiles):
    return pl.BlockSpec((tm, width), lambda j, i, te: (jnp.minimum(i, te[n_tiles] - 1), 0))


def gmm_swiglu_up(tile_table, xs, wg, wu, tm, tn):
    s, k = xs.shape
    n = wg.shape[2]
    n_tiles = s // tm
    w_spec = pl.BlockSpec((None, k, tn), lambda j, i, te: (te[i], 0, j))
    return pl.pallas_call(
        _gmm_up_kernel,
        grid_spec=pltpu.PrefetchScalarGridSpec(
            num_scalar_prefetch=1,
            grid=(n // tn, n_tiles),
            in_specs=[_row_spec(tm, k, n_tiles), w_spec, w_spec],
            out_specs=pl.BlockSpec((tm, tn), lambda j, i, te: (i, j)),
            scratch_shapes=[pltpu.VMEM((k, tn), BF16), pltpu.VMEM((k, tn), BF16)],
        ),
        out_shape=jax.ShapeDtypeStruct((s, n), BF16),
        compiler_params=_cparams("arbitrary", "arbitrary"),
        name="gmm_swiglu_up",
    )(tile_table, xs, wg, wu)


def _gmm_down_kernel(te_ref, a_ref, w_ref, gate_ref, o_ref):
    in_use = pl.program_id(0) < te_ref[pl.num_programs(0)]

    @pl.when(in_use)
    def _():
        o_ref[...] = _dot(a_ref[...], w_ref[...]) * gate_ref[...]

    @pl.when(jnp.logical_not(in_use))
    def _():
        o_ref[...] = jnp.zeros_like(o_ref)


def gmm_down(tile_table, hs, wd, slot_gate, tm, tn=1024):
    s, k = hs.shape
    n = wd.shape[2]
    n_tiles = s // tm
    return pl.pallas_call(
        _gmm_down_kernel,
        grid_spec=pltpu.PrefetchScalarGridSpec(
            num_scalar_prefetch=1,
            grid=(n_tiles, n // tn),
            in_specs=[pl.BlockSpec((tm, k), lambda i, j, te: (jnp.minimum(i, te[n_tiles] - 1), 0)),
                      pl.BlockSpec((None, k, tn), lambda i, j, te: (te[i], 0, jnp.where(i < te[n_tiles], j, 0))),
                      pl.BlockSpec((tm, 1), lambda i, j, te: (jnp.minimum(i, te[n_tiles] - 1), 0))],
            out_specs=pl.BlockSpec((tm, tn), lambda i, j, te: (i, j)),
        ),
        out_shape=jax.ShapeDtypeStruct((s, n), F32),
        compiler_params=_cparams("parallel", "arbitrary"),
        name="gmm_down",
    )(tile_table, hs, wd, slot_gate)


def moe_layer(h, ffn_g, w_router, wg, wu, wd, tm=MOE_TM):
    m, d = h.shape
    w_router_pad = jnp.pad(w_router.astype(F32), ((0, 0), (0, LANES - N_EXPERTS)))
    n_b, comb, idx = moe_router(h, ffn_g.reshape(1, d).astype(F32), w_router_pad)
    top_idx = idx[:, 0:2]
    gates = jnp.take_along_axis(comb[:, 0:N_EXPERTS], top_idx, axis=1)

    flat_e = top_idx.reshape(-1)
    n_pairs = 2 * m
    onehot = (flat_e[:, None] == jnp.arange(N_EXPERTS, dtype=jnp.int32)[None, :]).astype(jnp.int32)
    counts = jnp.sum(onehot, axis=0)
    rank = jnp.sum((jnp.cumsum(onehot, axis=0) - onehot) * onehot, axis=1)
    padded = ((counts + tm - 1) // tm) * tm
    pad_off = jnp.cumsum(padded) - padded
    raw_off = jnp.cumsum(counts) - counts
    dest = pad_off[flat_e] + rank
    n_tiles = n_pairs // tm + N_EXPERTS
    n_slots = n_tiles * tm
    order = jnp.argsort(flat_e, stable=True).astype(jnp.int32)
    tile_start = jnp.arange(n_tiles, dtype=jnp.int32) * tm
    pad_end = pad_off + padded
    tile_expert = jnp.minimum(jnp.sum((tile_start[:, None] >= pad_end[None, :]).astype(jnp.int32), axis=1),
                              N_EXPERTS - 1).astype(jnp.int32)
    slot = jnp.arange(n_slots, dtype=jnp.int32)
    slot_e = jnp.repeat(tile_expert, tm)
    local = slot - pad_off[slot_e]
    valid = (local < counts[slot_e]) & (slot < pad_end[N_EXPERTS - 1])
    src_pair = order[jnp.clip(raw_off[slot_e] + local, 0, n_pairs - 1)]
    slot_token = jnp.where(valid, src_pair // 2, 0)
    slot_gate = jnp.where(valid, gates.reshape(-1)[src_pair], 0.0).astype(F32).reshape(n_slots, 1)

    tiles_in_use = (pad_end[N_EXPERTS - 1] // tm).astype(jnp.int32).reshape(1)
    tile_table = jnp.concatenate([tile_expert, tiles_in_use])

    xs = jnp.take(n_b, slot_token, axis=0, mode="clip")
    hs = gmm_swiglu_up(tile_table, xs, wg, wu, tm, min(MOE_TN, wg.shape[2]))
    ys = gmm_down(tile_table, hs, cast_experts_bf16(wd), slot_gate, tm, min(1024, wd.shape[2]))
    dest2 = dest.reshape(m, 2)
    return h + jnp.take(ys, dest2[:, 0], axis=0) + jnp.take(ys, dest2[:, 1], axis=0)


def _lora_columns(w_in, layer):
    d = w_in.shape[0]
    o_w = C_A
    o_a = o_w + W_LORA + 2 * C_A
    o_g = o_a + A_LORA
    parts = [w_in[:, o_w:o_w + W_LORA], w_in[:, o_a:o_a + A_LORA], w_in[:, o_g:o_g + G_LORA]]
    used = W_LORA + A_LORA + G_LORA
    if layer > 0:
        parts.append(w_in[:, N_IN0:N_IN0 + V_LORA])
        used += V_LORA
    parts.append(jnp.zeros((d, LORA_PAD - used), w_in.dtype))
    return jnp.concatenate(parts, axis=1)


def _prep_w_in_kernel(w_ref, lora_ref, a_ref, bd_ref, cd_ref):
    o_k = C_A + W_LORA
    o_v = o_k + C_A
    o_b = N_RWKV
    o_c = o_b + N_GMLP
    o_d = o_c + N_SB

    def cast(lo, hi):
        return w_ref[:, lo:hi].astype(BF16)

    a_ref[:, 0:C_A] = cast(0, C_A)
    a_ref[:, C_A:2 * C_A] = cast(o_k, o_k + C_A)
    a_ref[:, 2 * C_A:3 * C_A] = cast(o_v, o_v + C_A)
    a_ref[:, 3 * C_A:] = lora_ref[...]
    bd_ref[:, 0:N_GMLP] = cast(o_b, o_c)
    bd_ref[:, N_GMLP:] = cast(o_d, o_d + 2 * C_D)
    cd_ref[:, 0:N_SB] = cast(o_c, o_d)
    cd_ref[:, N_SB:] = cast(o_d + 2 * C_D, o_d + 3 * C_D)


def prep_w_in(w_in, lora, rows=256):
    d, width = w_in.shape
    out_w = (N_A_COLS, N_GMLP + 2 * C_D, N_SB + C_D)
    return pl.pallas_call(
        _prep_w_in_kernel,
        grid=(d // rows,),
        in_specs=[pl.BlockSpec((rows, width), lambda i: (i, 0)), pl.BlockSpec((rows, LORA_PAD), lambda i: (i, 0))],
        out_specs=[pl.BlockSpec((rows, w), lambda i: (i, 0)) for w in out_w],
        out_shape=[jax.ShapeDtypeStruct((d, w), BF16) for w in out_w],
        compiler_params=_cparams("parallel"),
        name="prep_w_in",
    )(w_in, lora)


def _rwkv_mu(mu, mu_v):
    o_w = C_A
    o_k = o_w + W_LORA
    o_v = o_k + C_A
    o_a = o_v + C_A
    o_g = o_a + A_LORA
    parts = [mu[0:C_A], mu[o_k:o_k + C_A], mu[o_v:o_v + C_A],
             mu[o_w:o_w + W_LORA], mu[o_a:o_a + A_LORA], mu[o_g:o_g + G_LORA]]
    used = W_LORA + A_LORA + G_LORA
    if mu_v is not None:
        parts.append(mu_v)
        used += V_LORA
    parts.append(jnp.zeros((LORA_PAD - used,), mu.dtype))
    return jnp.concatenate(parts).reshape(1, N_A_COLS).astype(F32)


def _pad_rows(w, offset):
    return jnp.pad(w, ((offset, LORA_PAD - offset - w.shape[0]), (0, 0))).astype(BF16)


def kernel(x, p, positions, attn_norm, w_in0, w_in_rest, w_out, rwkv_mu, rwkv_mu_v, rwkv_w0, rwkv_w_up, rwkv_a0, rwkv_a_up, rwkv_v0, rwkv_v_up, rwkv_g_up, rwkv_k_k, rwkv_k_a, rwkv_r_k, rwkv_lnx_g, rwkv_lnx_b, gm_ln_g, gm_ln_b, gm_w_s, gm_b_s, gm_out_g, sb_out_g, diff_lambda, diff_subln, ffn_norm, dense_w_gate, dense_w_up, dense_w_down, moe_router, moe_w_gate, moe_w_up, moe_w_down, ple_norm, ple_w_gate, ple_w_proj, final_norm):
    batch, seq, d = x.shape
    m = batch * seq
    h = x.reshape(m, d).astype(F32)

    inv_freq = 1.0 / (ROPE_THETA ** (jnp.arange(0, DH_D, 2, dtype=F32) / DH_D))
    invf = jnp.tile(inv_freq, LANES // (DH_D // 2)).reshape(1, LANES)
    cos_t, sin_t = rope_tables(positions.reshape(m, 1), invf)

    v_first = None
    for i in range(DEPTH):
        w_in = w_in0 if i == 0 else w_in_rest[i - 1]
        w_a, w_bd, w_cd = prep_w_in(w_in, _lora_columns(w_in, i).astype(BF16))

        n = rmsnorm(h, attn_norm[i], BF16)
        proj_a = matmul(n, w_a, F32)
        proj_bd = matmul(n, w_bd, F32)
        proj_cd = matmul(n, w_cd, BF16)

        mu = _rwkv_mu(rwkv_mu[i], rwkv_mu_v[i - 1] if i > 0 else None)
        zeros_c = jnp.zeros((C_A,), F32)
        vec = jnp.stack([rwkv_w0[i], rwkv_a0[i], rwkv_v0[i - 1] if i > 0 else zeros_c, rwkv_k_k[i], rwkv_k_a[i],
                         rwkv_r_k[i].reshape(C_A), rwkv_lnx_g[i], rwkv_lnx_b[i]]).astype(F32)
        uw = _pad_rows(rwkv_w_up[i], 0)
        ua = _pad_rows(rwkv_a_up[i], W_LORA)
        ug = _pad_rows(rwkv_g_up[i], W_LORA + A_LORA)
        uv = _pad_rows(rwkv_v_up[i - 1], W_LORA + A_LORA + G_LORA) if i > 0 else jnp.zeros((LORA_PAD, C_A), BF16)
        vf_in = v_first if i > 0 else proj_a
        y_a, v_out = rwkv_mix(proj_a, vf_in, mu, vec, uw, ua, ug, uv, batch, seq, has_v_res=i > 0)
        if i == 0:
            v_first = v_out

        y_b = gmlp_mix(proj_bd, gm_ln_g[i].reshape(1, C_B), gm_ln_b[i].reshape(1, C_B), gm_w_s[i],
                       gm_b_s[i].T, gm_out_g[i].reshape(1, C_B))

        proj_cd3 = proj_cd.reshape(batch, seq, 4 * C_C)
        y_c = sb_attention(proj_cd3, sb_out_g[i].reshape(1, C_C), batch, seq).reshape(m, C_C)

        qk_rot = rope_qk(proj_bd, cos_t, sin_t).reshape(batch, seq, 2 * C_D)
        lam_init = 0.8 - 0.6 * math.exp(-0.3 * i)
        y_d = diff_attention(diff_lambda[i].astype(F32), qk_rot, proj_cd3, diff_subln[i].reshape(1, DV_D),
                             batch, seq, lam_init).reshape(m, C_D)

        h = mix_out_proj((y_a, y_b, y_c, y_d), w_out[i].astype(BF16), h)

        j = i // 2
        if i % 2 == 0:
            n = rmsnorm(h, ffn_norm[i], BF16)
            n_row_tiles = m // DENSE_TM
            one_group = jnp.concatenate([jnp.zeros((n_row_tiles,), jnp.int32), jnp.full((1,), n_row_tiles, jnp.int32)])
            hid = gmm_swiglu_up(one_group, n, dense_w_gate[j][None], dense_w_up[j][None], DENSE_TM, DENSE_TN)
            h = down_proj_residual(hid, dense_w_down[j].astype(BF16), h)
        else:
            h = moe_layer(h, ffn_norm[i], moe_router[j], moe_w_gate[j], moe_w_up[j], moe_w_down[j])

        n = rmsnorm(h, ple_norm[i], BF16)
        h = ple_residual(n, ple_w_gate[i].astype(BF16), p[i].reshape(m, P_DIM).astype(BF16),
                         ple_w_proj[i].astype(BF16), h)

    return rmsnorm(h, final_norm, x.dtype).reshape(batch, seq, d)
```

```python
import functools
import math

import jax
import jax.numpy as jnp
import numpy as np
from jax import lax
from jax.experimental import pallas as pl
from jax.experimental.pallas import tpu as pltpu

F32 = jnp.float32
BF16 = jnp.bfloat16

D_MODEL = 4096
DEPTH = 2
C_A = 1024
N_A = 64
W_LORA, A_LORA, V_LORA, G_LORA = 64, 64, 32, 160
C_B = 1024
H_B = 8
CHUNK = 128
C_C = 1024
H_C = 8
D_C = 128
C_D = 1024
H_D = 8
DV_D = 128
DH_D = 64
D_FF = 11008
N_EXPERTS = 8
D_FF_EXPERT = 5632
P_DIM = 256
ROPE_THETA = 10000.0
RMS_EPS = 1e-6
LN_EPS = 1e-5
GN_EPS = 64e-5
L2_EPS = 1e-12
N_RWKV = 3 * C_A + W_LORA + A_LORA + G_LORA
N_GMLP = 2 * C_B
N_SB = 3 * C_C
N_DIFF = 3 * C_D
N_IN0 = N_RWKV + N_GMLP + N_SB + N_DIFF

LANES = 128
VMEM_LIMIT = 56 * 1024 * 1024

LORA_PAD = 512
N_A_COLS = 3 * C_A + LORA_PAD
DENSE_TM, DENSE_TN = 1024, 256
DENSE_TK = 5504
MOE_TM, MOE_TN = 512, 512
RWKV_CHUNK = 64
RWKV_GROUP = 2
SB_CUTOFF = -110.0
SB_HEADS = 4

NT = (((1,), (1,)), ((), ()))
TN = (((0,), (0,)), ((), ()))


def _cparams(*sem):
    return pltpu.CompilerParams(dimension_semantics=sem, vmem_limit_bytes=VMEM_LIMIT)


def _dot(a, b, dims=None):
    if dims is None:
        return jnp.dot(a, b, preferred_element_type=F32)
    return lax.dot_general(a, b, dims, preferred_element_type=F32)


def _split2(x):
    hi = x.astype(BF16)
    lo = (x - hi.astype(F32)).astype(BF16)
    return hi, lo


def _split3(x):
    hi = x.astype(BF16)
    r = x - hi.astype(F32)
    mid = r.astype(BF16)
    lo = (r - mid.astype(F32)).astype(BF16)
    return hi, mid, lo


def _pick_tile(n, prefs):
    for t in prefs:
        if n % t == 0:
            return t
    raise ValueError(f"no tile for {n}")


def _rmsnorm_kernel(x_ref, g_ref, o_ref, *, eps):
    x = x_ref[...].astype(F32)
    ms = jnp.mean(x * x, axis=-1, keepdims=True)
    o_ref[...] = (x * lax.rsqrt(ms + eps) * g_ref[...]).astype(o_ref.dtype)


def rmsnorm(x, g, out_dtype, tm=512):
    m, d = x.shape
    return pl.pallas_call(
        functools.partial(_rmsnorm_kernel, eps=RMS_EPS),
        grid=(m // tm,),
        in_specs=[pl.BlockSpec((tm, d), lambda i: (i, 0)), pl.BlockSpec((1, d), lambda i: (0, 0))],
        out_specs=pl.BlockSpec((tm, d), lambda i: (i, 0)),
        out_shape=jax.ShapeDtypeStruct((m, d), out_dtype),
        compiler_params=_cparams("parallel"),
        name="rmsnorm",
    )(x, g.reshape(1, d).astype(F32))


def _cast_kernel(x_ref, o_ref):
    o_ref[...] = x_ref[...].astype(o_ref.dtype)


def cast_experts_bf16(w, rows=512):
    e, k, n = w.shape
    rows = min(rows, k)
    spec = pl.BlockSpec((None, rows, n), lambda i, j: (i, j, 0))
    return pl.pallas_call(
        _cast_kernel,
        grid=(e, k // rows),
        in_specs=[spec],
        out_specs=spec,
        out_shape=jax.ShapeDtypeStruct(w.shape, BF16),
        compiler_params=_cparams("parallel", "parallel"),
        name="cast_experts_bf16",
    )(w)


def _mm_kernel(a_ref, b_ref, o_ref):
    o_ref[...] = _dot(a_ref[...], b_ref[...]).astype(o_ref.dtype)


def matmul(a, b, out_dtype, tm=1024):
    m, k = a.shape
    n = b.shape[1]
    tn = _pick_tile(n, (1024, 512, 256, 128))
    return pl.pallas_call(
        _mm_kernel,
        grid=(m // tm, n // tn),
        in_specs=[pl.BlockSpec((tm, k), lambda i, j: (i, 0)), pl.BlockSpec((k, tn), lambda i, j: (0, j))],
        out_specs=pl.BlockSpec((tm, tn), lambda i, j: (i, j)),
        out_shape=jax.ShapeDtypeStruct((m, n), out_dtype),
        compiler_params=_cparams("parallel", "arbitrary"),
        name="matmul",
    )(a, b)


def _mix_out_kernel(ya_ref, yb_ref, yc_ref, yd_ref, w_ref, res_ref, o_ref):
    kq = ya_ref.shape[1]
    acc = res_ref[...]
    for idx, y_ref in enumerate((ya_ref, yb_ref, yc_ref, yd_ref)):
        acc = acc + _dot(y_ref[...], w_ref[idx * kq:(idx + 1) * kq, :])
    o_ref[...] = acc


def mix_out_proj(ys, w, res, tm=1024, tn=512):
    m, kq = ys[0].shape
    k, n = w.shape
    y_spec = pl.BlockSpec((tm, kq), lambda i, j: (i, 0))
    return pl.pallas_call(
        _mix_out_kernel,
        grid=(m // tm, n // tn),
        in_specs=[y_spec, y_spec, y_spec, y_spec,
                  pl.BlockSpec((k, tn), lambda i, j: (0, j)),
                  pl.BlockSpec((tm, tn), lambda i, j: (i, j))],
        out_specs=pl.BlockSpec((tm, tn), lambda i, j: (i, j)),
        out_shape=jax.ShapeDtypeStruct((m, n), F32),
        compiler_params=_cparams("parallel", "arbitrary"),
        name="mix_out_proj",
    )(*ys, w, res)


def _down_res_kernel(a_ref, b_ref, res_ref, o_ref, acc_ref):
    kk = pl.program_id(2)

    @pl.when(kk == 0)
    def _():
        acc_ref[...] = res_ref[...]

    acc_ref[...] += _dot(a_ref[...], b_ref[...])

    @pl.when(kk == pl.num_programs(2) - 1)
    def _():
        o_ref[...] = acc_ref[...]


def down_proj_residual(a, b, res, tm=1024, tn=512, tk=DENSE_TK):
    m, k = a.shape
    n = b.shape[1]
    return pl.pallas_call(
        _down_res_kernel,
        grid=(m // tm, n // tn, k // tk),
        in_specs=[pl.BlockSpec((tm, tk), lambda i, j, kk: (i, kk)),
                  pl.BlockSpec((tk, tn), lambda i, j, kk: (kk, j)),
                  pl.BlockSpec((tm, tn), lambda i, j, kk: (i, j))],
        out_specs=pl.BlockSpec((tm, tn), lambda i, j, kk: (i, j)),
        out_shape=jax.ShapeDtypeStruct((m, n), F32),
        scratch_shapes=[pltpu.VMEM((tm, tn), F32)],
        compiler_params=_cparams("parallel", "parallel", "arbitrary"),
        name="down_proj_residual",
    )(a, b, res)


def _ple_kernel(n_ref, wg_ref, p_ref, wp_ref, res_ref, o_ref):
    gate = jax.nn.sigmoid(_dot(n_ref[...], wg_ref[...]))
    emb = _dot(p_ref[...], wp_ref[...])
    o_ref[...] = res_ref[...] + gate * emb


def ple_residual(n, wg, p, wp, res, tm=1024, tn=512):
    m, k = n.shape
    nn = wg.shape[1]
    pd = p.shape[1]
    return pl.pallas_call(
        _ple_kernel,
        grid=(m // tm, nn // tn),
        in_specs=[pl.BlockSpec((tm, k), lambda i, j: (i, 0)),
                  pl.BlockSpec((k, tn), lambda i, j: (0, j)),
                  pl.BlockSpec((tm, pd), lambda i, j: (i, 0)),
                  pl.BlockSpec((pd, tn), lambda i, j: (0, j)),
                  pl.BlockSpec((tm, tn), lambda i, j: (i, j))],
        out_specs=pl.BlockSpec((tm, tn), lambda i, j: (i, j)),
        out_shape=jax.ShapeDtypeStruct((m, nn), F32),
        compiler_params=_cparams("parallel", "arbitrary"),
        name="ple_residual",
    )(n, wg, p, wp, res)


def _head_sum(x, ones_bd, split=False):
    outs = []
    for p in range(x.shape[1] // LANES):
        xp = x[:, p * LANES:(p + 1) * LANES]
        if split:
            hi, lo = _split2(xp)
            outs.append(_dot(hi, ones_bd) + _dot(lo, ones_bd))
        else:
            outs.append(_dot(xp.astype(BF16), ones_bd))
    return jnp.concatenate(outs, axis=1)


def _rwkv_kernel(z_ref, vf_ref, mu_ref, vec_ref, uw_ref, ua_ref, ug_ref, uv_ref,
                 y_ref, vout_ref, prev_ref, h_ref, *, has_v_res):
    tc = RWKV_CHUNK
    tr = z_ref.shape[0]
    chunks = range(tr // tc)
    c = pl.program_id(1)

    @pl.when(c == 0)
    def _():
        prev_ref[...] = jnp.zeros_like(prev_ref)
        h_ref[...] = jnp.zeros_like(h_ref)

    z = z_ref[...]
    row = lax.broadcasted_iota(jnp.int32, z.shape, 0)
    zp = jnp.where(row == 0, prev_ref[...], pltpu.roll(z, 1, 0))
    prev_ref[...] = z[tr - 1:tr, :]
    zs = z + (zp - z) * mu_ref[...]
    r = zs[:, 0:C_A]
    k = zs[:, C_A:2 * C_A]
    v = zs[:, 2 * C_A:3 * C_A]
    lr = zs[:, 3 * C_A:3 * C_A + LORA_PAD]

    w0 = vec_ref[0:1, :]
    a0 = vec_ref[1:2, :]
    v0 = vec_ref[2:3, :]
    k_k = vec_ref[3:4, :]
    k_a = vec_ref[4:5, :]
    r_k = vec_ref[5:6, :]
    lnx_g = vec_ref[6:7, :]
    lnx_b = vec_ref[7:8, :]

    lr_b = lr.astype(BF16)
    w_lin = _dot(jnp.tanh(lr).astype(BF16), uw_ref[...])
    a_lin = _dot(lr_b, ua_ref[...])
    g = _dot(jax.nn.sigmoid(lr).astype(BF16), ug_ref[...])
    w = -jax.nn.softplus(-(w0 + w_lin)) - 0.5
    ld = -jnp.exp(w)
    a_lr = jax.nn.sigmoid(a0 + a_lin)
    if has_v_res:
        v_lin = _dot(lr_b, uv_ref[...])
        v = v + (vf_ref[...] - v) * jax.nn.sigmoid(v0 + v_lin)
    vout_ref[...] = v

    li = lax.broadcasted_iota(jnp.int32, (LANES, LANES), 0)
    lj = lax.broadcasted_iota(jnp.int32, (LANES, LANES), 1)
    same_head = (li // N_A) == (lj // N_A)
    ones_bd = jnp.where(same_head, 1.0, 0.0).astype(BF16)

    kk = k * k_k
    kk = kk / jnp.maximum(jnp.sqrt(_head_sum(kk * kk, ones_bd, split=True)), L2_EPS)
    k = k * (1.0 + (a_lr - 1.0) * k_a)
    a_s = -kk
    b_s = kk * a_lr

    ti = lax.broadcasted_iota(jnp.int32, (tr, tr), 0)
    tj = lax.broadcasted_iota(jnp.int32, (tr, tr), 1)
    tri = jnp.where(((ti // tc) == (tj // tc)) & (ti >= tj), 1.0, 0.0).astype(BF16)
    ld_h, ld_m, ld_l = _split3(ld)
    cum = _dot(tri, ld_h) + _dot(tri, ld_m) + _dot(tri, ld_l)
    cum_last = [cum[(ci + 1) * tc - 1:(ci + 1) * tc, :] for ci in chunks]
    cum_end = jnp.concatenate([jnp.broadcast_to(x, (tc, C_A)) for x in cum_last], axis=0)
    p_inc = jnp.exp(cum)
    p_exc = jnp.exp(cum - ld)
    p_inv = jnp.exp(-cum)
    p_end = jnp.exp(cum_end - cum)
    decay_end = [jnp.exp(x) for x in cum_last]

    a_t = a_s * p_exc
    b_t = b_s * p_inv
    k_t = k * p_inv
    r_t = r * p_inc
    b_h = b_s * p_end
    k_h = k * p_end

    first_head = lax.broadcasted_iota(jnp.int32, (tc, LANES), 1) < N_A
    strict = same_head & ((li % N_A) > (lj % N_A))
    incl = same_head & ((li % N_A) >= (lj % N_A))
    eye = li == lj

    def stack(x):
        return jnp.concatenate([jnp.where(first_head, x, 0.0), jnp.where(first_head, 0.0, x)], axis=0)

    def dup(x):
        return jnp.concatenate([x, x], axis=0)

    n_pairs = C_A // LANES
    chains = [(ci, p) for ci in chunks for p in range(n_pairs)]
    pairs = range(len(chains))
    cut = [(slice(ci * tc, (ci + 1) * tc), slice(p * LANES, (p + 1) * LANES)) for ci, p in chains]
    n2 = 2 * tc
    a_st = [stack(a_t[rs, sl]) for rs, sl in cut]
    r_st = [stack(r_t[rs, sl]) for rs, sl in cut]
    v_st = [stack(v[rs, sl]).astype(BF16) for rs, sl in cut]
    bh_st = [stack(b_h[rs, sl]).astype(BF16) for rs, sl in cut]
    kh_st = [stack(k_h[rs, sl]).astype(BF16) for rs, sl in cut]
    lhs = [jnp.concatenate([a_st[p], r_st[p]], axis=0).astype(BF16) for p in pairs]
    rhs = [jnp.concatenate([dup(b_t[rs, sl]), dup(k_t[rs, sl])], axis=0).astype(BF16) for rs, sl in cut]
    x = [_dot(lhs[p], rhs[p], NT) for p in pairs]
    lk = [jnp.where(strict, x[p][0:n2, 0:n2], 0.0) for p in pairs]
    a_ak = [jnp.where(strict, x[p][0:n2, n2:2 * n2], 0.0).astype(BF16) for p in pairs]
    a_rb = [jnp.where(incl, x[p][n2:2 * n2, 0:n2], 0.0).astype(BF16) for p in pairs]
    a_rk = [jnp.where(incl, x[p][n2:2 * n2, n2:2 * n2], 0.0).astype(BF16) for p in pairs]
    zz = [jnp.concatenate([a_st[p], _dot(a_ak[p], v_st[p])], axis=1) for p in pairs]
    n_iter = int(math.log2(tc))
    for it in range(n_iter):
        lk_b = [lk[p].astype(BF16) for p in pairs]
        zz = [zz[p] + _dot(lk_b[p], zz[p].astype(BF16)) for p in pairs]
        if it < n_iter - 1:
            lk = [_dot(lk_b[p], lk_b[p]) for p in pairs]
    zz_b = [zz[p].astype(BF16) for p in pairs]
    qy = [_dot(a_rb[p], zz_b[p]) for p in pairs]
    y3 = [qy[p][:, LANES:] + _dot(a_rk[p], v_st[p]) for p in pairs]
    mg = [_dot(bh_st[p], zz_b[p], TN) for p in pairs]
    g_mat = [mg[p][:, LANES:] + _dot(kh_st[p], v_st[p], TN) for p in pairs]
    qm = [jnp.concatenate([r_st[p] + qy[p][:, 0:LANES],
                           jnp.where(eye, decay_end[chains[p][0]][:, cut[p][1]], 0.0) + mg[p][:, 0:LANES]],
                          axis=0).astype(BF16) for p in pairs]
    state = [h_ref[p] for p in range(n_pairs)]
    y_rows = []
    for ci in chunks:
        first = ci * n_pairs
        out = [_dot(qm[first + p], state[p].astype(BF16)) for p in range(n_pairs)]
        y_parts = []
        for p in range(n_pairs):
            y_st = out[p][0:n2, :] + y3[first + p]
            y_parts.append(y_st[0:tc, :] + y_st[tc:n2, :])
            state[p] = out[p][n2:, :] + g_mat[first + p]
        y_rows.append(jnp.concatenate(y_parts, axis=1))
    for p in range(n_pairs):
        h_ref[p] = state[p]
    y = jnp.concatenate(y_rows, axis=0)

    inv_n = 1.0 / N_A
    mean = _head_sum(y, ones_bd) * inv_n
    d = y - mean
    var = _head_sum(d * d, ones_bd) * inv_n
    yn = d * lax.rsqrt(var + GN_EPS) * lnx_g + lnx_b
    bonus = _head_sum(r * k * r_k, ones_bd) * v
    y_ref[...] = ((yn + bonus) * g).astype(y_ref.dtype)


def rwkv_mix(proj_a, v_first, mu, vec, uw, ua, ug, uv, batch, seq, has_v_res):
    tr = RWKV_CHUNK * RWKV_GROUP
    nc = seq // tr
    row_spec = lambda w: pl.BlockSpec((tr, w), lambda b, c: (b * nc + c, 0))
    full = lambda s: pl.BlockSpec(s, lambda b, c: tuple(0 for _ in s))
    m = batch * seq
    return pl.pallas_call(
        functools.partial(_rwkv_kernel, has_v_res=has_v_res),
        grid=(batch, nc),
        in_specs=[row_spec(N_A_COLS), row_spec(C_A), full((1, N_A_COLS)), full((8, C_A)),
                  full((LORA_PAD, C_A)), full((LORA_PAD, C_A)), full((LORA_PAD, C_A)), full((LORA_PAD, C_A))],
        out_specs=[row_spec(C_A), row_spec(C_A)],
        out_shape=[jax.ShapeDtypeStruct((m, C_A), BF16), jax.ShapeDtypeStruct((m, C_A), F32)],
        scratch_shapes=[pltpu.VMEM((1, N_A_COLS), F32), pltpu.VMEM((C_A // LANES, LANES, LANES), F32)],
        compiler_params=_cparams("parallel", "arbitrary"),
        name="rwkv_mix",
    )(proj_a, v_first, mu, vec, uw, ua, ug, uv)


def _gelu(x):
    return 0.5 * x * (1.0 + lax.erf(x * math.sqrt(0.5)))


def _gmlp_kernel(z_ref, lng_ref, lnb_ref, w_ref, bs_ref, og_ref, o_ref):
    u = _gelu(z_ref[:, 0:C_B])
    v = _gelu(z_ref[:, C_B:2 * C_B])
    mu = jnp.mean(v, axis=-1, keepdims=True)
    d = v - mu
    var = jnp.mean(d * d, axis=-1, keepdims=True)
    vn = d * lax.rsqrt(var + LN_EPS) * lng_ref[...] + lnb_ref[...]
    ti = lax.broadcasted_iota(jnp.int32, (CHUNK, CHUNK), 0)
    tj = lax.broadcasted_iota(jnp.int32, (CHUNK, CHUNK), 1)
    causal = ti >= tj
    dh = C_B // H_B
    for h in range(H_B):
        sl = slice(h * dh, (h + 1) * dh)
        w = jnp.where(causal, w_ref[h], 0.0).astype(BF16)
        s = _dot(w, vn[:, sl].astype(BF16)) + bs_ref[:, h:h + 1]
        y = u[:, sl] * s
        ms = jnp.mean(y * y, axis=-1, keepdims=True)
        o_ref[:, sl] = (y * lax.rsqrt(ms + RMS_EPS) * og_ref[:, sl]).astype(o_ref.dtype)


def gmlp_mix(proj_bd, ln_g, ln_b, w_s, b_s_t, out_g):
    m = proj_bd.shape[0]
    full = lambda s: pl.BlockSpec(s, lambda i: tuple(0 for _ in s))
    return pl.pallas_call(
        _gmlp_kernel,
        grid=(m // CHUNK,),
        in_specs=[pl.BlockSpec((CHUNK, 2 * C_B), lambda i: (i, 0)),
                  full((1, C_B)), full((1, C_B)), full((H_B, CHUNK, CHUNK)), full((CHUNK, H_B)), full((1, C_B))],
        out_specs=pl.BlockSpec((CHUNK, C_B), lambda i: (i, 0)),
        out_shape=jax.ShapeDtypeStruct((m, C_B), BF16),
        compiler_params=_cparams("parallel"),
        name="gmlp_mix",
    )(proj_bd, ln_g, ln_b, w_s, b_s_t, out_g)


def _sb_kernel(q_ref, k_ref, v_ref, g_ref, o_ref, *, tq, scale):
    qi = pl.program_id(2)
    heads = range(q_ref.shape[1] // D_C)
    hs = [slice(h * D_C, (h + 1) * D_C) for h in heads]
    q = [q_ref[:, s] for s in hs]
    row = lax.broadcasted_iota(jnp.int32, (tq, tq), 0)
    col = lax.broadcasted_iota(jnp.int32, (tq, tq), 1)
    below = col < row
    upper = jnp.where(row > col, 1.0, 0.0).astype(BF16)

    def block(j, carry, acc, keep):
        start = pl.multiple_of(j * tq, tq)
        z = [_dot(q[h], k_ref[pl.ds(start, tq), hs[h]], NT) * scale for h in heads]
        log_beta = [jnp.minimum(z[h], 0.0) - jnp.log(1.0 + jnp.exp(-jnp.abs(z[h]))) for h in heads]
        log_1mb = [log_beta[h] - z[h] for h in heads]
        if keep is not None:
            log_1mb = [jnp.where(keep, x, 0.0) for x in log_1mb]
        parts = [_split2(x) for x in log_1mb]
        after = [_dot(hi, upper) + _dot(lo, upper) for hi, lo in parts]
        att = [jnp.exp(log_beta[h] + after[h] + carry[h]) for h in heads]
        if keep is not None:
            att = [jnp.where(keep, x, 0.0) for x in att]
        acc = [acc[h] + _dot(att[h].astype(BF16), v_ref[pl.ds(start, tq), hs[h]]) for h in heads]
        carry = [carry[h] + after[h][:, 0:1] + log_1mb[h][:, 0:1] for h in heads]
        return carry, acc

    carry = [jnp.zeros((tq, 1), F32) for _ in heads]
    acc = [jnp.zeros((tq, D_C), F32) for _ in heads]
    carry, acc = block(qi, carry, acc, below)
    has_prev = jnp.broadcast_to(qi >= 1, (tq, tq))
    carry, acc = block(jnp.maximum(qi - 1, 0), carry, acc, has_prev)

    def more(state):
        top = functools.reduce(jnp.maximum, [jnp.max(c) for c in state[1]])
        return jnp.logical_and(state[0] >= 0, top > SB_CUTOFF)

    def body(state):
        carry_j, acc_j = block(state[0], list(state[1]), list(state[2]), None)
        return state[0] - 1, tuple(carry_j), tuple(acc_j)

    _, carry, acc = lax.while_loop(more, body, (qi - 2, tuple(carry), tuple(acc)))
    for h in heads:
        ms = jnp.mean(acc[h] * acc[h], axis=-1, keepdims=True)
        o_ref[:, hs[h]] = (acc[h] * lax.rsqrt(ms + RMS_EPS) * g_ref[:, hs[h]]).astype(o_ref.dtype)


def sb_attention(proj_cd, out_g, batch, seq, tq=256, heads=SB_HEADS):
    nq = seq // tq
    w = heads * D_C
    groups = H_C // heads
    return pl.pallas_call(
        functools.partial(_sb_kernel, tq=tq, scale=D_C ** -0.5),
        grid=(batch, groups, nq),
        in_specs=[pl.BlockSpec((None, tq, w), lambda b, h, i: (b, i, h)),
                  pl.BlockSpec((None, seq, w), lambda b, h, i: (b, 0, groups + h)),
                  pl.BlockSpec((None, seq, w), lambda b, h, i: (b, 0, 2 * groups + h)),
                  pl.BlockSpec((1, w), lambda b, h, i: (0, h))],
        out_specs=pl.BlockSpec((None, tq, w), lambda b, h, i: (b, i, h)),
        out_shape=jax.ShapeDtypeStruct((batch, seq, C_C), BF16),
        compiler_params=_cparams("parallel", "parallel", "arbitrary"),
        name="sb_attention",
    )(proj_cd, proj_cd, proj_cd, out_g)


def _rope_table_kernel(pos_ref, invf_ref, cos_ref, sin_ref):
    ang = pos_ref[...].astype(F32) * invf_ref[...]
    cos_ref[...] = jnp.cos(ang)
    sin_ref[...] = jnp.sin(ang)


def rope_tables(pos_col, invf, tm=512):
    m = pos_col.shape[0]
    spec = pl.BlockSpec((tm, LANES), lambda i: (i, 0))
    return pl.pallas_call(
        _rope_table_kernel,
        grid=(m // tm,),
        in_specs=[pl.BlockSpec((tm, 1), lambda i: (i, 0)), pl.BlockSpec((1, LANES), lambda i: (0, 0))],
        out_specs=[spec, spec],
        out_shape=[jax.ShapeDtypeStruct((m, LANES), F32)] * 2,
        compiler_params=_cparams("parallel"),
        name="rope_tables",
    )(pos_col, invf)


def _rope(x, c, s):
    lane = lax.broadcasted_iota(jnp.int32, x.shape, 1)
    first_half = (lane % DH_D) < (DH_D // 2)
    rot = jnp.where(first_half, -pltpu.roll(x, LANES - DH_D // 2, 1), pltpu.roll(x, DH_D // 2, 1))
    return x * c + rot * s


def _diff_kernel(lam_ref, q_ref, k_ref, v_ref, cos_ref, sin_ref, g_ref, o_ref, k_scr, *, tq, lam_init):
    qi = pl.program_id(2)

    @pl.when(qi == 0)
    def _():
        def rope_rows(t, carry):
            rows = pl.ds(pl.multiple_of(t * tq, tq), tq)
            k_scr[rows, :] = _rope(k_ref[rows, :], cos_ref[rows, :], sin_ref[rows, :]).astype(BF16)
            return carry

        lax.fori_loop(0, k_ref.shape[0] // tq, rope_rows, 0)

    q_rows = pl.ds(pl.multiple_of(qi * tq, tq), tq)
    q = (_rope(q_ref[...], cos_ref[q_rows, :], sin_ref[q_rows, :]) * (DH_D ** -0.5)).astype(BF16)
    lane = lax.broadcasted_iota(jnp.int32, q.shape, 1)
    zero = jnp.zeros_like(q)
    qs = jnp.concatenate([jnp.where(lane < DH_D, q, zero), jnp.where(lane < DH_D, zero, q)], axis=0)
    row = lax.broadcasted_iota(jnp.int32, (2 * tq, tq), 0) % tq
    col = lax.broadcasted_iota(jnp.int32, (2 * tq, tq), 1)
    visible = col <= row

    def block(j, m, l, acc, diagonal):
        start = pl.multiple_of(j * tq, tq)
        kb = k_scr[pl.ds(start, tq), :]
        vb = v_ref[pl.ds(start, tq), :]
        s = _dot(qs, kb, NT)
        if diagonal:
            s = jnp.where(visible, s, -jnp.inf)
        m_new = jnp.maximum(m, jnp.max(s, axis=-1, keepdims=True))
        alpha = jnp.exp(m - m_new)
        p = jnp.exp(s - m_new)
        l = alpha * l + jnp.sum(p, axis=-1, keepdims=True)
        acc = alpha * acc + _dot(p.astype(BF16), vb)
        return m_new, l, acc

    m0 = jnp.full((2 * tq, 1), -jnp.inf, F32)
    l0 = jnp.zeros((2 * tq, 1), F32)
    acc0 = jnp.zeros((2 * tq, DV_D), F32)
    m, l, acc = block(qi, m0, l0, acc0, True)

    def grouped(first, group):
        def body(i, state):
            for t in range(group):
                state = block(first + i * group + t, state[0], state[1], state[2], False)
            return state
        return body

    n4 = qi // 4
    n2 = (qi - 4 * n4) // 2
    n1 = qi - 4 * n4 - 2 * n2
    state = lax.fori_loop(0, n4, grouped(0, 4), (m, l, acc))
    state = lax.fori_loop(0, n2, grouped(4 * n4, 2), state)
    m, l, acc = lax.fori_loop(0, n1, grouped(4 * n4 + 2 * n2, 1), state)
    lv = lam_ref[...]
    lam = (jnp.exp(jnp.sum(lv[0:1, :] * lv[1:2, :], axis=-1, keepdims=True))
           - jnp.exp(jnp.sum(lv[2:3, :] * lv[3:4, :], axis=-1, keepdims=True)) + lam_init)
    o = acc / l
    y = o[0:tq, :] - lam * o[tq:2 * tq, :]
    ms = jnp.mean(y * y, axis=-1, keepdims=True)
    o_ref[...] = (y * lax.rsqrt(ms + LN_EPS) * g_ref[...] * (1.0 - lam_init)).astype(o_ref.dtype)


def diff_attention(lam_p, proj_bd, proj_cd, cos_t, sin_t, subln, batch, seq, lam_init, tq=512):
    nq = seq // tq
    q0 = N_GMLP // DV_D
    table = pl.BlockSpec((None, seq, LANES), lambda b, h, i: (b, 0, 0))
    return pl.pallas_call(
        functools.partial(_diff_kernel, tq=tq, lam_init=lam_init),
        grid=(batch, H_D, nq),
        in_specs=[pl.BlockSpec((4, DH_D), lambda b, h, i: (0, 0)),
                  pl.BlockSpec((None, tq, DV_D), lambda b, h, i: (b, i, q0 + h)),
                  pl.BlockSpec((None, seq, DV_D), lambda b, h, i: (b, 0, q0 + H_D + h)),
                  pl.BlockSpec((None, seq, DV_D), lambda b, h, i: (b, 0, 3 * H_C + h)),
                  table, table,
                  pl.BlockSpec((1, DV_D), lambda b, h, i: (0, 0))],
        out_specs=pl.BlockSpec((None, tq, DV_D), lambda b, h, i: (b, i, h)),
        out_shape=jax.ShapeDtypeStruct((batch, seq, C_D), BF16),
        scratch_shapes=[pltpu.VMEM((seq, DV_D), BF16)],
        compiler_params=_cparams("parallel", "parallel", "arbitrary"),
        name="diff_attention",
    )(lam_p, proj_bd, proj_bd, proj_cd, cos_t, sin_t, subln)


def _router_kernel(h_ref, g_ref, wr_ref, n_ref, comb_ref, idx_ref):
    x = h_ref[...]
    ms = jnp.mean(x * x, axis=-1, keepdims=True)
    n = x * lax.rsqrt(ms + RMS_EPS) * g_ref[...]
    n_ref[...] = n.astype(n_ref.dtype)
    n_hi, n_lo = _split2(n)
    w_hi, w_lo = _split2(wr_ref[...])
    logits = _dot(n_hi, w_hi) + _dot(n_lo, w_hi) + _dot(n_hi, w_lo)
    lane = lax.broadcasted_iota(jnp.int32, logits.shape, 1)
    neg = -jnp.inf
    logits = jnp.where(lane < N_EXPERTS, logits, neg)
    m1 = jnp.max(logits, axis=-1, keepdims=True)
    i1 = jnp.min(jnp.where(logits == m1, lane, LANES), axis=-1, keepdims=True)
    rest = jnp.where(lane == i1, neg, logits)
    m2 = jnp.max(rest, axis=-1, keepdims=True)
    i2 = jnp.min(jnp.where(rest == m2, lane, LANES), axis=-1, keepdims=True)
    e = jnp.exp(m2 - m1)
    g1 = 1.0 / (1.0 + e)
    g2 = e / (1.0 + e)
    comb_ref[...] = jnp.where(lane == i1, g1, 0.0) + jnp.where(lane == i2, g2, 0.0)
    idx_ref[...] = jnp.where(lane == 0, i1, jnp.where(lane == 1, i2, 0))


def moe_router(h, g, w_router_pad, tm=256):
    m, d = h.shape
    spec = pl.BlockSpec((tm, LANES), lambda i: (i, 0))
    return pl.pallas_call(
        _router_kernel,
        grid=(m // tm,),
        in_specs=[pl.BlockSpec((tm, d), lambda i: (i, 0)), pl.BlockSpec((1, d), lambda i: (0, 0)),
                  pl.BlockSpec((d, LANES), lambda i: (0, 0))],
        out_specs=[pl.BlockSpec((tm, d), lambda i: (i, 0)), spec, spec],
        out_shape=[jax.ShapeDtypeStruct((m, d), BF16), jax.ShapeDtypeStruct((m, LANES), F32),
                   jax.ShapeDtypeStruct((m, LANES), jnp.int32)],
        compiler_params=_cparams("parallel"),
        name="moe_router",
    )(h, g, w_router_pad)


def _tile_state(te_ref):
    i = pl.program_id(1)
    in_use = i < te_ref[pl.num_programs(1)]
    new_weights = jnp.logical_or(i == 0, te_ref[i] != te_ref[jnp.maximum(i - 1, 0)])
    return in_use, new_weights


def _gmm_up_kernel(te_ref, a_ref, wg_ref, wu_ref, o_ref, wg_bf, wu_bf):
    in_use, new_weights = _tile_state(te_ref)

    @pl.when(new_weights)
    def _():
        wg_bf[...] = wg_ref[...].astype(BF16)
        wu_bf[...] = wu_ref[...].astype(BF16)

    @pl.when(in_use)
    def _():
        a = a_ref[...]
        g = _dot(a, wg_bf[...])
        u = _dot(a, wu_bf[...])
        o_ref[...] = (g * jax.nn.sigmoid(g) * u).astype(o_ref.dtype)

    @pl.when(jnp.logical_not(in_use))
    def _():
        o_ref[...] = jnp.zeros_like(o_ref)


def _row_spec(tm, width, n_tiles):
    return pl.BlockSpec((tm, width), lambda j, i, te: (jnp.minimum(i, te[n_tiles] - 1), 0))


def gmm_swiglu_up(tile_table, xs, wg, wu, tm, tn):
    s, k = xs.shape
    n = wg.shape[2]
    n_tiles = s // tm
    w_spec = pl.BlockSpec((None, k, tn), lambda j, i, te: (te[i], 0, j))
    return pl.pallas_call(
        _gmm_up_kernel,
        grid_spec=pltpu.PrefetchScalarGridSpec(
            num_scalar_prefetch=1,
            grid=(n // tn, n_tiles),
            in_specs=[_row_spec(tm, k, n_tiles), w_spec, w_spec],
            out_specs=pl.BlockSpec((tm, tn), lambda j, i, te: (i, j)),
            scratch_shapes=[pltpu.VMEM((k, tn), BF16), pltpu.VMEM((k, tn), BF16)],
        ),
        out_shape=jax.ShapeDtypeStruct((s, n), BF16),
        compiler_params=_cparams("arbitrary", "arbitrary"),
        name="gmm_swiglu_up",
    )(tile_table, xs, wg, wu)


def _gmm_down_kernel(te_ref, a_ref, w_ref, gate_ref, o_ref):
    in_use = pl.program_id(0) < te_ref[pl.num_programs(0)]

    @pl.when(in_use)
    def _():
        o_ref[...] = _dot(a_ref[...], w_ref[...]) * gate_ref[...]

    @pl.when(jnp.logical_not(in_use))
    def _():
        o_ref[...] = jnp.zeros_like(o_ref)


def gmm_down(tile_table, hs, wd, slot_gate, tm, tn=1024):
    s, k = hs.shape
    n = wd.shape[2]
    n_tiles = s // tm
    return pl.pallas_call(
        _gmm_down_kernel,
        grid_spec=pltpu.PrefetchScalarGridSpec(
            num_scalar_prefetch=1,
            grid=(n_tiles, n // tn),
            in_specs=[pl.BlockSpec((tm, k), lambda i, j, te: (jnp.minimum(i, te[n_tiles] - 1), 0)),
                      pl.BlockSpec((None, k, tn), lambda i, j, te: (te[i], 0, jnp.where(i < te[n_tiles], j, 0))),
                      pl.BlockSpec((tm, 1), lambda i, j, te: (jnp.minimum(i, te[n_tiles] - 1), 0))],
            out_specs=pl.BlockSpec((tm, tn), lambda i, j, te: (i, j)),
        ),
        out_shape=jax.ShapeDtypeStruct((s, n), F32),
        compiler_params=_cparams("parallel", "arbitrary"),
        name="gmm_down",
    )(tile_table, hs, wd, slot_gate)


def moe_layer(h, ffn_g, w_router, wg, wu, wd, tm=MOE_TM):
    m, d = h.shape
    w_router_pad = jnp.pad(w_router.astype(F32), ((0, 0), (0, LANES - N_EXPERTS)))
    n_b, comb, idx = moe_router(h, ffn_g.reshape(1, d).astype(F32), w_router_pad)
    top_idx = idx[:, 0:2]
    gates = jnp.take_along_axis(comb[:, 0:N_EXPERTS], top_idx, axis=1)

    flat_e = top_idx.reshape(-1)
    n_pairs = 2 * m
    onehot = (flat_e[:, None] == jnp.arange(N_EXPERTS, dtype=jnp.int32)[None, :]).astype(jnp.int32)
    counts = jnp.sum(onehot, axis=0)
    rank = jnp.sum((jnp.cumsum(onehot, axis=0) - onehot) * onehot, axis=1)
    padded = ((counts + tm - 1) // tm) * tm
    pad_off = jnp.cumsum(padded) - padded
    raw_off = jnp.cumsum(counts) - counts
    dest = pad_off[flat_e] + rank
    n_tiles = n_pairs // tm + N_EXPERTS
    n_slots = n_tiles * tm
    order = jnp.argsort(flat_e, stable=True).astype(jnp.int32)
    tile_start = jnp.arange(n_tiles, dtype=jnp.int32) * tm
    pad_end = pad_off + padded
    tile_expert = jnp.minimum(jnp.sum((tile_start[:, None] >= pad_end[None, :]).astype(jnp.int32), axis=1),
                              N_EXPERTS - 1).astype(jnp.int32)
    slot = jnp.arange(n_slots, dtype=jnp.int32)
    slot_e = jnp.repeat(tile_expert, tm)
    local = slot - pad_off[slot_e]
    valid = (local < counts[slot_e]) & (slot < pad_end[N_EXPERTS - 1])
    src_pair = order[jnp.clip(raw_off[slot_e] + local, 0, n_pairs - 1)]
    slot_token = jnp.where(valid, src_pair // 2, 0)
    slot_gate = jnp.where(valid, gates.reshape(-1)[src_pair], 0.0).astype(F32).reshape(n_slots, 1)

    tiles_in_use = (pad_end[N_EXPERTS - 1] // tm).astype(jnp.int32).reshape(1)
    tile_table = jnp.concatenate([tile_expert, tiles_in_use])

    xs = jnp.take(n_b, slot_token, axis=0, mode="clip")
    hs = gmm_swiglu_up(tile_table, xs, wg, wu, tm, min(MOE_TN, wg.shape[2]))
    ys = gmm_down(tile_table, hs, cast_experts_bf16(wd), slot_gate, tm, min(1024, wd.shape[2]))
    dest2 = dest.reshape(m, 2)
    return h + jnp.take(ys, dest2[:, 0], axis=0) + jnp.take(ys, dest2[:, 1], axis=0)


def _rwkv_columns(w_in, layer):
    d = w_in.shape[0]
    o_w = C_A
    o_k = o_w + W_LORA
    o_v = o_k + C_A
    o_a = o_v + C_A
    o_g = o_a + A_LORA
    parts = [w_in[:, 0:C_A], w_in[:, o_k:o_k + C_A], w_in[:, o_v:o_v + C_A],
             w_in[:, o_w:o_w + W_LORA], w_in[:, o_a:o_a + A_LORA], w_in[:, o_g:o_g + G_LORA]]
    used = W_LORA + A_LORA + G_LORA
    if layer > 0:
        parts.append(w_in[:, N_IN0:N_IN0 + V_LORA])
        used += V_LORA
    parts.append(jnp.zeros((d, LORA_PAD - used), w_in.dtype))
    return jnp.concatenate(parts, axis=1)


def _rwkv_mu(mu, mu_v):
    o_w = C_A
    o_k = o_w + W_LORA
    o_v = o_k + C_A
    o_a = o_v + C_A
    o_g = o_a + A_LORA
    parts = [mu[0:C_A], mu[o_k:o_k + C_A], mu[o_v:o_v + C_A],
             mu[o_w:o_w + W_LORA], mu[o_a:o_a + A_LORA], mu[o_g:o_g + G_LORA]]
    used = W_LORA + A_LORA + G_LORA
    if mu_v is not None:
        parts.append(mu_v)
        used += V_LORA
    parts.append(jnp.zeros((LORA_PAD - used,), mu.dtype))
    return jnp.concatenate(parts).reshape(1, N_A_COLS).astype(F32)


def _pad_rows(w, offset):
    return jnp.pad(w, ((offset, LORA_PAD - offset - w.shape[0]), (0, 0))).astype(BF16)


def kernel(x, p, positions, attn_norm, w_in0, w_in_rest, w_out, rwkv_mu, rwkv_mu_v, rwkv_w0, rwkv_w_up, rwkv_a0, rwkv_a_up, rwkv_v0, rwkv_v_up, rwkv_g_up, rwkv_k_k, rwkv_k_a, rwkv_r_k, rwkv_lnx_g, rwkv_lnx_b, gm_ln_g, gm_ln_b, gm_w_s, gm_b_s, gm_out_g, sb_out_g, diff_lambda, diff_subln, ffn_norm, dense_w_gate, dense_w_up, dense_w_down, moe_router, moe_w_gate, moe_w_up, moe_w_down, ple_norm, ple_w_gate, ple_w_proj, final_norm):
    batch, seq, d = x.shape
    m = batch * seq
    h = x.reshape(m, d).astype(F32)

    inv_freq = 1.0 / (ROPE_THETA ** (jnp.arange(0, DH_D, 2, dtype=F32) / DH_D))
    invf = jnp.tile(inv_freq, LANES // (DH_D // 2)).reshape(1, LANES)
    cos_t, sin_t = rope_tables(positions.reshape(m, 1), invf)

    v_first = None
    for i in range(DEPTH):
        w_in = w_in0 if i == 0 else w_in_rest[i - 1]
        o_b = N_RWKV
        o_c = o_b + N_GMLP
        o_d = o_c + N_SB
        w_a = _rwkv_columns(w_in, i).astype(BF16)
        w_bd = jnp.concatenate([w_in[:, o_b:o_c], w_in[:, o_d:o_d + 2 * C_D]], axis=1).astype(BF16)
        w_cd = jnp.concatenate([w_in[:, o_c:o_d], w_in[:, o_d + 2 * C_D:o_d + 3 * C_D]], axis=1).astype(BF16)

        n = rmsnorm(h, attn_norm[i], BF16)
        proj_a = matmul(n, w_a, F32)
        proj_bd = matmul(n, w_bd, F32)
        proj_cd = matmul(n, w_cd, BF16)

        mu = _rwkv_mu(rwkv_mu[i], rwkv_mu_v[i - 1] if i > 0 else None)
        zeros_c = jnp.zeros((C_A,), F32)
        vec = jnp.stack([rwkv_w0[i], rwkv_a0[i], rwkv_v0[i - 1] if i > 0 else zeros_c, rwkv_k_k[i], rwkv_k_a[i],
                         rwkv_r_k[i].reshape(C_A), rwkv_lnx_g[i], rwkv_lnx_b[i]]).astype(F32)
        uw = _pad_rows(rwkv_w_up[i], 0)
        ua = _pad_rows(rwkv_a_up[i], W_LORA)
        ug = _pad_rows(rwkv_g_up[i], W_LORA + A_LORA)
        uv = _pad_rows(rwkv_v_up[i - 1], W_LORA + A_LORA + G_LORA) if i > 0 else jnp.zeros((LORA_PAD, C_A), BF16)
        vf_in = v_first if i > 0 else proj_a
        y_a, v_out = rwkv_mix(proj_a, vf_in, mu, vec, uw, ua, ug, uv, batch, seq, has_v_res=i > 0)
        if i == 0:
            v_first = v_out

        y_b = gmlp_mix(proj_bd, gm_ln_g[i].reshape(1, C_B), gm_ln_b[i].reshape(1, C_B), gm_w_s[i],
                       gm_b_s[i].T, gm_out_g[i].reshape(1, C_B))

        proj_cd3 = proj_cd.reshape(batch, seq, 4 * C_C)
        y_c = sb_attention(proj_cd3, sb_out_g[i].reshape(1, C_C), batch, seq).reshape(m, C_C)

        lam_init = 0.8 - 0.6 * math.exp(-0.3 * i)
        y_d = diff_attention(diff_lambda[i].astype(F32), proj_bd.reshape(batch, seq, N_GMLP + 2 * C_D), proj_cd3,
                             cos_t.reshape(batch, seq, LANES), sin_t.reshape(batch, seq, LANES),
                             diff_subln[i].reshape(1, DV_D), batch, seq, lam_init).reshape(m, C_D)

        h = mix_out_proj((y_a, y_b, y_c, y_d), w_out[i].astype(BF16), h)

        j = i // 2
        if i % 2 == 0:
            n = rmsnorm(h, ffn_norm[i], BF16)
            n_row_tiles = m // DENSE_TM
            one_group = jnp.concatenate([jnp.zeros((n_row_tiles,), jnp.int32), jnp.full((1,), n_row_tiles, jnp.int32)])
            hid = gmm_swiglu_up(one_group, n, dense_w_gate[j][None], dense_w_up[j][None], DENSE_TM, DENSE_TN)
            h = down_proj_residual(hid, dense_w_down[j].astype(BF16), h)
        else:
            h = moe_layer(h, ffn_norm[i], moe_router[j], moe_w_gate[j], moe_w_up[j], moe_w_down[j])

        n = rmsnorm(h, ple_norm[i], BF16)
        h = ple_residual(n, ple_w_gate[i].astype(BF16), p[i].reshape(m, P_DIM).astype(BF16),
                         ple_w_proj[i].astype(BF16), h)

    return rmsnorm(h, final_norm, x.dtype).reshape(batch, seq, d)
```

```python
import functools
import math

import jax
import jax.numpy as jnp
import numpy as np
from jax import lax
from jax.experimental import pallas as pl
from jax.experimental.pallas import tpu as pltpu

F32 = jnp.float32
BF16 = jnp.bfloat16

D_MODEL = 4096
DEPTH = 2
C_A = 1024
N_A = 64
W_LORA, A_LORA, V_LORA, G_LORA = 64, 64, 32, 160
C_B = 1024
H_B = 8
CHUNK = 128
C_C = 1024
H_C = 8
D_C = 128
C_D = 1024
H_D = 8
DV_D = 128
DH_D = 64
D_FF = 11008
N_EXPERTS = 8
D_FF_EXPERT = 5632
P_DIM = 256
ROPE_THETA = 10000.0
RMS_EPS = 1e-6
LN_EPS = 1e-5
GN_EPS = 64e-5
L2_EPS = 1e-12
N_RWKV = 3 * C_A + W_LORA + A_LORA + G_LORA
N_GMLP = 2 * C_B
N_SB = 3 * C_C
N_DIFF = 3 * C_D
N_IN0 = N_RWKV + N_GMLP + N_SB + N_DIFF

LANES = 128
VMEM_LIMIT = 56 * 1024 * 1024

LORA_PAD = 512
N_A_COLS = 3 * C_A + LORA_PAD
DENSE_TM, DENSE_TN = 512, 512
DENSE_TK = 5504
MOE_TM, MOE_TN = 512, 512
RWKV_CHUNK = 64
RWKV_GROUP = 2
SB_CUTOFF = -110.0
SB_HEADS = 4

NT = (((1,), (1,)), ((), ()))
TN = (((0,), (0,)), ((), ()))


def _cparams(*sem):
    return pltpu.CompilerParams(dimension_semantics=sem, vmem_limit_bytes=VMEM_LIMIT)


def _dot(a, b, dims=None):
    if dims is None:
        return jnp.dot(a, b, preferred_element_type=F32)
    return lax.dot_general(a, b, dims, preferred_element_type=F32)


def _split2(x):
    hi = x.astype(BF16)
    lo = (x - hi.astype(F32)).astype(BF16)
    return hi, lo


def _split3(x):
    hi = x.astype(BF16)
    r = x - hi.astype(F32)
    mid = r.astype(BF16)
    lo = (r - mid.astype(F32)).astype(BF16)
    return hi, mid, lo


def _pick_tile(n, prefs):
    for t in prefs:
        if n % t == 0:
            return t
    raise ValueError(f"no tile for {n}")


def _rmsnorm_kernel(x_ref, g_ref, o_ref, *, eps):
    x = x_ref[...].astype(F32)
    ms = jnp.mean(x * x, axis=-1, keepdims=True)
    o_ref[...] = (x * lax.rsqrt(ms + eps) * g_ref[...]).astype(o_ref.dtype)


def rmsnorm(x, g, out_dtype, tm=512):
    m, d = x.shape
    return pl.pallas_call(
        functools.partial(_rmsnorm_kernel, eps=RMS_EPS),
        grid=(m // tm,),
        in_specs=[pl.BlockSpec((tm, d), lambda i: (i, 0)), pl.BlockSpec((1, d), lambda i: (0, 0))],
        out_specs=pl.BlockSpec((tm, d), lambda i: (i, 0)),
        out_shape=jax.ShapeDtypeStruct((m, d), out_dtype),
        compiler_params=_cparams("parallel"),
        name="rmsnorm",
    )(x, g.reshape(1, d).astype(F32))


def _cast_kernel(x_ref, o_ref):
    o_ref[...] = x_ref[...].astype(o_ref.dtype)


def cast_experts_bf16(w, rows=512):
    e, k, n = w.shape
    rows = min(rows, k)
    spec = pl.BlockSpec((None, rows, n), lambda i, j: (i, j, 0))
    return pl.pallas_call(
        _cast_kernel,
        grid=(e, k // rows),
        in_specs=[spec],
        out_specs=spec,
        out_shape=jax.ShapeDtypeStruct(w.shape, BF16),
        compiler_params=_cparams("parallel", "parallel"),
        name="cast_experts_bf16",
    )(w)


def _mm_kernel(a_ref, b_ref, o_ref):
    o_ref[...] = _dot(a_ref[...], b_ref[...]).astype(o_ref.dtype)


def matmul(a, b, out_dtype, tm=1024):
    m, k = a.shape
    n = b.shape[1]
    tn = _pick_tile(n, (1024, 512, 256, 128))
    return pl.pallas_call(
        _mm_kernel,
        grid=(m // tm, n // tn),
        in_specs=[pl.BlockSpec((tm, k), lambda i, j: (i, 0)), pl.BlockSpec((k, tn), lambda i, j: (0, j))],
        out_specs=pl.BlockSpec((tm, tn), lambda i, j: (i, j)),
        out_shape=jax.ShapeDtypeStruct((m, n), out_dtype),
        compiler_params=_cparams("parallel", "arbitrary"),
        name="matmul",
    )(a, b)


def _mix_out_kernel(ya_ref, yb_ref, yc_ref, yd_ref, w_ref, res_ref, o_ref):
    kq = ya_ref.shape[1]
    acc = res_ref[...]
    for idx, y_ref in enumerate((ya_ref, yb_ref, yc_ref, yd_ref)):
        acc = acc + _dot(y_ref[...], w_ref[idx * kq:(idx + 1) * kq, :])
    o_ref[...] = acc


def mix_out_proj(ys, w, res, tm=1024, tn=512):
    m, kq = ys[0].shape
    k, n = w.shape
    y_spec = pl.BlockSpec((tm, kq), lambda i, j: (i, 0))
    return pl.pallas_call(
        _mix_out_kernel,
        grid=(m // tm, n // tn),
        in_specs=[y_spec, y_spec, y_spec, y_spec,
                  pl.BlockSpec((k, tn), lambda i, j: (0, j)),
                  pl.BlockSpec((tm, tn), lambda i, j: (i, j))],
        out_specs=pl.BlockSpec((tm, tn), lambda i, j: (i, j)),
        out_shape=jax.ShapeDtypeStruct((m, n), F32),
        compiler_params=_cparams("parallel", "arbitrary"),
        name="mix_out_proj",
    )(*ys, w, res)


def _down_res_kernel(a_ref, b_ref, res_ref, o_ref, acc_ref):
    kk = pl.program_id(2)

    @pl.when(kk == 0)
    def _():
        acc_ref[...] = res_ref[...]

    acc_ref[...] += _dot(a_ref[...], b_ref[...])

    @pl.when(kk == pl.num_programs(2) - 1)
    def _():
        o_ref[...] = acc_ref[...]


def down_proj_residual(a, b, res, tm=1024, tn=512, tk=DENSE_TK):
    m, k = a.shape
    n = b.shape[1]
    return pl.pallas_call(
        _down_res_kernel,
        grid=(m // tm, n // tn, k // tk),
        in_specs=[pl.BlockSpec((tm, tk), lambda i, j, kk: (i, kk)),
                  pl.BlockSpec((tk, tn), lambda i, j, kk: (kk, j)),
                  pl.BlockSpec((tm, tn), lambda i, j, kk: (i, j))],
        out_specs=pl.BlockSpec((tm, tn), lambda i, j, kk: (i, j)),
        out_shape=jax.ShapeDtypeStruct((m, n), F32),
        scratch_shapes=[pltpu.VMEM((tm, tn), F32)],
        compiler_params=_cparams("parallel", "parallel", "arbitrary"),
        name="down_proj_residual",
    )(a, b, res)


def _ple_kernel(n_ref, wg_ref, p_ref, wp_ref, res_ref, o_ref):
    gate = jax.nn.sigmoid(_dot(n_ref[...], wg_ref[...]))
    emb = _dot(p_ref[...], wp_ref[...])
    o_ref[...] = res_ref[...] + gate * emb


def ple_residual(n, wg, p, wp, res, tm=1024, tn=512):
    m, k = n.shape
    nn = wg.shape[1]
    pd = p.shape[1]
    return pl.pallas_call(
        _ple_kernel,
        grid=(m // tm, nn // tn),
        in_specs=[pl.BlockSpec((tm, k), lambda i, j: (i, 0)),
                  pl.BlockSpec((k, tn), lambda i, j: (0, j)),
                  pl.BlockSpec((tm, pd), lambda i, j: (i, 0)),
                  pl.BlockSpec((pd, tn), lambda i, j: (0, j)),
                  pl.BlockSpec((tm, tn), lambda i, j: (i, j))],
        out_specs=pl.BlockSpec((tm, tn), lambda i, j: (i, j)),
        out_shape=jax.ShapeDtypeStruct((m, nn), F32),
        compiler_params=_cparams("parallel", "arbitrary"),
        name="ple_residual",
    )(n, wg, p, wp, res)


def _head_sum(x, ones_bd, split=False):
    outs = []
    for p in range(x.shape[1] // LANES):
        xp = x[:, p * LANES:(p + 1) * LANES]
        if split:
            hi, lo = _split2(xp)
            outs.append(_dot(hi, ones_bd) + _dot(lo, ones_bd))
        else:
            outs.append(_dot(xp.astype(BF16), ones_bd))
    return jnp.concatenate(outs, axis=1)


def _rwkv_kernel(z_ref, vf_ref, mu_ref, vec_ref, uw_ref, ua_ref, ug_ref, uv_ref,
                 y_ref, vout_ref, prev_ref, h_ref, *, has_v_res):
    tc = RWKV_CHUNK
    tr = z_ref.shape[0]
    chunks = range(tr // tc)
    c = pl.program_id(1)

    @pl.when(c == 0)
    def _():
        prev_ref[...] = jnp.zeros_like(prev_ref)
        h_ref[...] = jnp.zeros_like(h_ref)

    z = z_ref[...]
    row = lax.broadcasted_iota(jnp.int32, z.shape, 0)
    zp = jnp.where(row == 0, prev_ref[...], pltpu.roll(z, 1, 0))
    prev_ref[...] = z[tr - 1:tr, :]
    zs = z + (zp - z) * mu_ref[...]
    r = zs[:, 0:C_A]
    k = zs[:, C_A:2 * C_A]
    v = zs[:, 2 * C_A:3 * C_A]
    lr = zs[:, 3 * C_A:3 * C_A + LORA_PAD]

    w0 = vec_ref[0:1, :]
    a0 = vec_ref[1:2, :]
    v0 = vec_ref[2:3, :]
    k_k = vec_ref[3:4, :]
    k_a = vec_ref[4:5, :]
    r_k = vec_ref[5:6, :]
    lnx_g = vec_ref[6:7, :]
    lnx_b = vec_ref[7:8, :]

    lr_b = lr.astype(BF16)
    w_lin = _dot(jnp.tanh(lr).astype(BF16), uw_ref[...])
    a_lin = _dot(lr_b, ua_ref[...])
    g = _dot(jax.nn.sigmoid(lr).astype(BF16), ug_ref[...])
    w = -jax.nn.softplus(-(w0 + w_lin)) - 0.5
    ld = -jnp.exp(w)
    a_lr = jax.nn.sigmoid(a0 + a_lin)
    if has_v_res:
        v_lin = _dot(lr_b, uv_ref[...])
        v = v + (vf_ref[...] - v) * jax.nn.sigmoid(v0 + v_lin)
    vout_ref[...] = v

    li = lax.broadcasted_iota(jnp.int32, (LANES, LANES), 0)
    lj = lax.broadcasted_iota(jnp.int32, (LANES, LANES), 1)
    same_head = (li // N_A) == (lj // N_A)
    ones_bd = jnp.where(same_head, 1.0, 0.0).astype(BF16)

    kk = k * k_k
    kk = kk / jnp.maximum(jnp.sqrt(_head_sum(kk * kk, ones_bd, split=True)), L2_EPS)
    k = k * (1.0 + (a_lr - 1.0) * k_a)
    a_s = -kk
    b_s = kk * a_lr

    ti = lax.broadcasted_iota(jnp.int32, (tr, tr), 0)
    tj = lax.broadcasted_iota(jnp.int32, (tr, tr), 1)
    tri = jnp.where(((ti // tc) == (tj // tc)) & (ti >= tj), 1.0, 0.0).astype(BF16)
    ld_h, ld_m, ld_l = _split3(ld)
    cum = _dot(tri, ld_h) + _dot(tri, ld_m) + _dot(tri, ld_l)
    cum_last = [cum[(ci + 1) * tc - 1:(ci + 1) * tc, :] for ci in chunks]
    cum_end = jnp.concatenate([jnp.broadcast_to(x, (tc, C_A)) for x in cum_last], axis=0)
    p_inc = jnp.exp(cum)
    p_exc = jnp.exp(cum - ld)
    p_inv = jnp.exp(-cum)
    p_end = jnp.exp(cum_end - cum)
    decay_end = [jnp.exp(x) for x in cum_last]

    a_t = a_s * p_exc
    b_t = b_s * p_inv
    k_t = k * p_inv
    r_t = r * p_inc
    b_h = b_s * p_end
    k_h = k * p_end

    first_head = lax.broadcasted_iota(jnp.int32, (tc, LANES), 1) < N_A
    strict = same_head & ((li % N_A) > (lj % N_A))
    incl = same_head & ((li % N_A) >= (lj % N_A))
    eye = li == lj

    def stack(x):
        return jnp.concatenate([jnp.where(first_head, x, 0.0), jnp.where(first_head, 0.0, x)], axis=0)

    def dup(x):
        return jnp.concatenate([x, x], axis=0)

    n_pairs = C_A // LANES
    chains = [(ci, p) for ci in chunks for p in range(n_pairs)]
    pairs = range(len(chains))
    cut = [(slice(ci * tc, (ci + 1) * tc), slice(p * LANES, (p + 1) * LANES)) for ci, p in chains]
    n2 = 2 * tc
    a_st = [stack(a_t[rs, sl]) for rs, sl in cut]
    r_st = [stack(r_t[rs, sl]) for rs, sl in cut]
    v_st = [stack(v[rs, sl]).astype(BF16) for rs, sl in cut]
    bh_st = [stack(b_h[rs, sl]).astype(BF16) for rs, sl in cut]
    kh_st = [stack(k_h[rs, sl]).astype(BF16) for rs, sl in cut]
    lhs = [jnp.concatenate([a_st[p], r_st[p]], axis=0).astype(BF16) for p in pairs]
    rhs = [jnp.concatenate([dup(b_t[rs, sl]), dup(k_t[rs, sl])], axis=0).astype(BF16) for rs, sl in cut]
    x = [_dot(lhs[p], rhs[p], NT) for p in pairs]
    lk = [jnp.where(strict, x[p][0:n2, 0:n2], 0.0) for p in pairs]
    a_ak = [jnp.where(strict, x[p][0:n2, n2:2 * n2], 0.0).astype(BF16) for p in pairs]
    a_rb = [jnp.where(incl, x[p][n2:2 * n2, 0:n2], 0.0).astype(BF16) for p in pairs]
    a_rk = [jnp.where(incl, x[p][n2:2 * n2, n2:2 * n2], 0.0).astype(BF16) for p in pairs]
    zz = [jnp.concatenate([a_st[p], _dot(a_ak[p], v_st[p])], axis=1) for p in pairs]
    n_iter = int(math.log2(tc))
    for it in range(n_iter):
        lk_b = [lk[p].astype(BF16) for p in pairs]
        zz = [zz[p] + _dot(lk_b[p], zz[p].astype(BF16)) for p in pairs]
        if it < n_iter - 1:
            lk = [_dot(lk_b[p], lk_b[p]) for p in pairs]
    zz_b = [zz[p].astype(BF16) for p in pairs]
    qy = [_dot(a_rb[p], zz_b[p]) for p in pairs]
    y3 = [qy[p][:, LANES:] + _dot(a_rk[p], v_st[p]) for p in pairs]
    mg = [_dot(bh_st[p], zz_b[p], TN) for p in pairs]
    g_mat = [mg[p][:, LANES:] + _dot(kh_st[p], v_st[p], TN) for p in pairs]
    qm = [jnp.concatenate([r_st[p] + qy[p][:, 0:LANES],
                           jnp.where(eye, decay_end[chains[p][0]][:, cut[p][1]], 0.0) + mg[p][:, 0:LANES]],
                          axis=0).astype(BF16) for p in pairs]
    state = [h_ref[p] for p in range(n_pairs)]
    y_rows = []
    for ci in chunks:
        first = ci * n_pairs
        out = [_dot(qm[first + p], state[p].astype(BF16)) for p in range(n_pairs)]
        y_parts = []
        for p in range(n_pairs):
            y_st = out[p][0:n2, :] + y3[first + p]
            y_parts.append(y_st[0:tc, :] + y_st[tc:n2, :])
            state[p] = out[p][n2:, :] + g_mat[first + p]
        y_rows.append(jnp.concatenate(y_parts, axis=1))
    for p in range(n_pairs):
        h_ref[p] = state[p]
    y = jnp.concatenate(y_rows, axis=0)

    inv_n = 1.0 / N_A
    mean = _head_sum(y, ones_bd) * inv_n
    d = y - mean
    var = _head_sum(d * d, ones_bd) * inv_n
    yn = d * lax.rsqrt(var + GN_EPS) * lnx_g + lnx_b
    bonus = _head_sum(r * k * r_k, ones_bd) * v
    y_ref[...] = ((yn + bonus) * g).astype(y_ref.dtype)


def rwkv_mix(proj_a, v_first, mu, vec, uw, ua, ug, uv, batch, seq, has_v_res):
    tr = RWKV_CHUNK * RWKV_GROUP
    nc = seq // tr
    row_spec = lambda w: pl.BlockSpec((tr, w), lambda b, c: (b * nc + c, 0))
    full = lambda s: pl.BlockSpec(s, lambda b, c: tuple(0 for _ in s))
    m = batch * seq
    return pl.pallas_call(
        functools.partial(_rwkv_kernel, has_v_res=has_v_res),
        grid=(batch, nc),
        in_specs=[row_spec(N_A_COLS), row_spec(C_A), full((1, N_A_COLS)), full((8, C_A)),
                  full((LORA_PAD, C_A)), full((LORA_PAD, C_A)), full((LORA_PAD, C_A)), full((LORA_PAD, C_A))],
        out_specs=[row_spec(C_A), row_spec(C_A)],
        out_shape=[jax.ShapeDtypeStruct((m, C_A), BF16), jax.ShapeDtypeStruct((m, C_A), F32)],
        scratch_shapes=[pltpu.VMEM((1, N_A_COLS), F32), pltpu.VMEM((C_A // LANES, LANES, LANES), F32)],
        compiler_params=_cparams("parallel", "arbitrary"),
        name="rwkv_mix",
    )(proj_a, v_first, mu, vec, uw, ua, ug, uv)


def _gelu(x):
    return 0.5 * x * (1.0 + lax.erf(x * math.sqrt(0.5)))


def _gmlp_kernel(z_ref, lng_ref, lnb_ref, w_ref, bs_ref, og_ref, o_ref):
    u = _gelu(z_ref[:, 0:C_B])
    v = _gelu(z_ref[:, C_B:2 * C_B])
    mu = jnp.mean(v, axis=-1, keepdims=True)
    d = v - mu
    var = jnp.mean(d * d, axis=-1, keepdims=True)
    vn = d * lax.rsqrt(var + LN_EPS) * lng_ref[...] + lnb_ref[...]
    ti = lax.broadcasted_iota(jnp.int32, (CHUNK, CHUNK), 0)
    tj = lax.broadcasted_iota(jnp.int32, (CHUNK, CHUNK), 1)
    causal = ti >= tj
    dh = C_B // H_B
    for h in range(H_B):
        sl = slice(h * dh, (h + 1) * dh)
        w = jnp.where(causal, w_ref[h], 0.0).astype(BF16)
        s = _dot(w, vn[:, sl].astype(BF16)) + bs_ref[:, h:h + 1]
        y = u[:, sl] * s
        ms = jnp.mean(y * y, axis=-1, keepdims=True)
        o_ref[:, sl] = (y * lax.rsqrt(ms + RMS_EPS) * og_ref[:, sl]).astype(o_ref.dtype)


def gmlp_mix(proj_bd, ln_g, ln_b, w_s, b_s_t, out_g):
    m = proj_bd.shape[0]
    full = lambda s: pl.BlockSpec(s, lambda i: tuple(0 for _ in s))
    return pl.pallas_call(
        _gmlp_kernel,
        grid=(m // CHUNK,),
        in_specs=[pl.BlockSpec((CHUNK, 2 * C_B), lambda i: (i, 0)),
                  full((1, C_B)), full((1, C_B)), full((H_B, CHUNK, CHUNK)), full((CHUNK, H_B)), full((1, C_B))],
        out_specs=pl.BlockSpec((CHUNK, C_B), lambda i: (i, 0)),
        out_shape=jax.ShapeDtypeStruct((m, C_B), BF16),
        compiler_params=_cparams("parallel"),
        name="gmlp_mix",
    )(proj_bd, ln_g, ln_b, w_s, b_s_t, out_g)


def _sb_kernel(q_ref, k_ref, v_ref, g_ref, o_ref, *, tq, scale):
    qi = pl.program_id(2)
    heads = range(q_ref.shape[1] // D_C)
    hs = [slice(h * D_C, (h + 1) * D_C) for h in heads]
    q = [q_ref[:, s] for s in hs]
    row = lax.broadcasted_iota(jnp.int32, (tq, tq), 0)
    col = lax.broadcasted_iota(jnp.int32, (tq, tq), 1)
    below = col < row
    upper = jnp.where(row > col, 1.0, 0.0).astype(BF16)

    def block(j, carry, acc, keep):
        start = pl.multiple_of(j * tq, tq)
        z = [_dot(q[h], k_ref[pl.ds(start, tq), hs[h]], NT) * scale for h in heads]
        log_beta = [jnp.minimum(z[h], 0.0) - jnp.log(1.0 + jnp.exp(-jnp.abs(z[h]))) for h in heads]
        log_1mb = [log_beta[h] - z[h] for h in heads]
        if keep is not None:
            log_1mb = [jnp.where(keep, x, 0.0) for x in log_1mb]
        parts = [_split2(x) for x in log_1mb]
        after = [_dot(hi, upper) + _dot(lo, upper) for hi, lo in parts]
        att = [jnp.exp(log_beta[h] + after[h] + carry[h]) for h in heads]
        if keep is not None:
            att = [jnp.where(keep, x, 0.0) for x in att]
        acc = [acc[h] + _dot(att[h].astype(BF16), v_ref[pl.ds(start, tq), hs[h]]) for h in heads]
        carry = [carry[h] + after[h][:, 0:1] + log_1mb[h][:, 0:1] for h in heads]
        return carry, acc

    carry = [jnp.zeros((tq, 1), F32) for _ in heads]
    acc = [jnp.zeros((tq, D_C), F32) for _ in heads]
    carry, acc = block(qi, carry, acc, below)
    has_prev = jnp.broadcast_to(qi >= 1, (tq, tq))
    carry, acc = block(jnp.maximum(qi - 1, 0), carry, acc, has_prev)

    def more(state):
        top = functools.reduce(jnp.maximum, [jnp.max(c) for c in state[1]])
        return jnp.logical_and(state[0] >= 0, top > SB_CUTOFF)

    def body(state):
        carry_j, acc_j = block(state[0], list(state[1]), list(state[2]), None)
        return state[0] - 1, tuple(carry_j), tuple(acc_j)

    _, carry, acc = lax.while_loop(more, body, (qi - 2, tuple(carry), tuple(acc)))
    for h in heads:
        ms = jnp.mean(acc[h] * acc[h], axis=-1, keepdims=True)
        o_ref[:, hs[h]] = (acc[h] * lax.rsqrt(ms + RMS_EPS) * g_ref[:, hs[h]]).astype(o_ref.dtype)


def sb_attention(proj_cd, out_g, batch, seq, tq=256, heads=SB_HEADS):
    nq = seq // tq
    w = heads * D_C
    groups = H_C // heads
    return pl.pallas_call(
        functools.partial(_sb_kernel, tq=tq, scale=D_C ** -0.5),
        grid=(batch, groups, nq),
        in_specs=[pl.BlockSpec((None, tq, w), lambda b, h, i: (b, i, h)),
                  pl.BlockSpec((None, seq, w), lambda b, h, i: (b, 0, groups + h)),
                  pl.BlockSpec((None, seq, w), lambda b, h, i: (b, 0, 2 * groups + h)),
                  pl.BlockSpec((1, w), lambda b, h, i: (0, h))],
        out_specs=pl.BlockSpec((None, tq, w), lambda b, h, i: (b, i, h)),
        out_shape=jax.ShapeDtypeStruct((batch, seq, C_C), BF16),
        compiler_params=_cparams("parallel", "parallel", "arbitrary"),
        name="sb_attention",
    )(proj_cd, proj_cd, proj_cd, out_g)


def _rope_table_kernel(pos_ref, invf_ref, cos_ref, sin_ref):
    ang = pos_ref[...].astype(F32) * invf_ref[...]
    cos_ref[...] = jnp.cos(ang)
    sin_ref[...] = jnp.sin(ang)


def rope_tables(pos_col, invf, tm=512):
    m = pos_col.shape[0]
    spec = pl.BlockSpec((tm, LANES), lambda i: (i, 0))
    return pl.pallas_call(
        _rope_table_kernel,
        grid=(m // tm,),
        in_specs=[pl.BlockSpec((tm, 1), lambda i: (i, 0)), pl.BlockSpec((1, LANES), lambda i: (0, 0))],
        out_specs=[spec, spec],
        out_shape=[jax.ShapeDtypeStruct((m, LANES), F32)] * 2,
        compiler_params=_cparams("parallel"),
        name="rope_tables",
    )(pos_col, invf)


def _rope(x, c, s):
    lane = lax.broadcasted_iota(jnp.int32, x.shape, 1)
    first_half = (lane % DH_D) < (DH_D // 2)
    rot = jnp.where(first_half, -pltpu.roll(x, LANES - DH_D // 2, 1), pltpu.roll(x, DH_D // 2, 1))
    return x * c + rot * s


def _diff_kernel(lam_ref, q_ref, k_ref, v_ref, cos_ref, sin_ref, g_ref, o_ref, k_scr, *, tq, lam_init):
    qi = pl.program_id(2)

    @pl.when(qi == 0)
    def _():
        def rope_rows(t, carry):
            rows = pl.ds(pl.multiple_of(t * tq, tq), tq)
            k_scr[rows, :] = _rope(k_ref[rows, :], cos_ref[rows, :], sin_ref[rows, :]).astype(BF16)
            return carry

        lax.fori_loop(0, k_ref.shape[0] // tq, rope_rows, 0)

    q_rows = pl.ds(pl.multiple_of(qi * tq, tq), tq)
    q = (_rope(q_ref[...], cos_ref[q_rows, :], sin_ref[q_rows, :]) * (DH_D ** -0.5)).astype(BF16)
    lane = lax.broadcasted_iota(jnp.int32, q.shape, 1)
    zero = jnp.zeros_like(q)
    qs = jnp.concatenate([jnp.where(lane < DH_D, q, zero), jnp.where(lane < DH_D, zero, q)], axis=0)
    row = lax.broadcasted_iota(jnp.int32, (2 * tq, tq), 0) % tq
    col = lax.broadcasted_iota(jnp.int32, (2 * tq, tq), 1)
    visible = col <= row

    def block(j, m, l, acc, diagonal):
        start = pl.multiple_of(j * tq, tq)
        kb = k_scr[pl.ds(start, tq), :]
        vb = v_ref[pl.ds(start, tq), :]
        s = _dot(qs, kb, NT)
        if diagonal:
            s = jnp.where(visible, s, -jnp.inf)
        m_new = jnp.maximum(m, jnp.max(s, axis=-1, keepdims=True))
        alpha = jnp.exp(m - m_new)
        p = jnp.exp(s - m_new)
        l = alpha * l + jnp.sum(p, axis=-1, keepdims=True)
        acc = alpha * acc + _dot(p.astype(BF16), vb)
        return m_new, l, acc

    m0 = jnp.full((2 * tq, 1), -jnp.inf, F32)
    l0 = jnp.zeros((2 * tq, 1), F32)
    acc0 = jnp.zeros((2 * tq, DV_D), F32)
    m, l, acc = block(qi, m0, l0, acc0, True)

    def grouped(first, group):
        def body(i, state):
            for t in range(group):
                state = block(first + i * group + t, state[0], state[1], state[2], False)
            return state
        return body

    n4 = qi // 4
    n2 = (qi - 4 * n4) // 2
    n1 = qi - 4 * n4 - 2 * n2
    state = lax.fori_loop(0, n4, grouped(0, 4), (m, l, acc))
    state = lax.fori_loop(0, n2, grouped(4 * n4, 2), state)
    m, l, acc = lax.fori_loop(0, n1, grouped(4 * n4 + 2 * n2, 1), state)
    lv = lam_ref[...]
    lam = (jnp.exp(jnp.sum(lv[0:1, :] * lv[1:2, :], axis=-1, keepdims=True))
           - jnp.exp(jnp.sum(lv[2:3, :] * lv[3:4, :], axis=-1, keepdims=True)) + lam_init)
    o = acc / l
    y = o[0:tq, :] - lam * o[tq:2 * tq, :]
    ms = jnp.mean(y * y, axis=-1, keepdims=True)
    o_ref[...] = (y * lax.rsqrt(ms + LN_EPS) * g_ref[...] * (1.0 - lam_init)).astype(o_ref.dtype)


def diff_attention(lam_p, proj_bd, proj_cd, cos_t, sin_t, subln, batch, seq, lam_init, tq=512):
    nq = seq // tq
    q0 = N_GMLP // DV_D
    table = pl.BlockSpec((None, seq, LANES), lambda b, h, i: (b, 0, 0))
    return pl.pallas_call(
        functools.partial(_diff_kernel, tq=tq, lam_init=lam_init),
        grid=(batch, H_D, nq),
        in_specs=[pl.BlockSpec((4, DH_D), lambda b, h, i: (0, 0)),
                  pl.BlockSpec((None, tq, DV_D), lambda b, h, i: (b, i, q0 + h)),
                  pl.BlockSpec((None, seq, DV_D), lambda b, h, i: (b, 0, q0 + H_D + h)),
                  pl.BlockSpec((None, seq, DV_D), lambda b, h, i: (b, 0, 3 * H_C + h)),
                  table, table,
                  pl.BlockSpec((1, DV_D), lambda b, h, i: (0, 0))],
        out_specs=pl.BlockSpec((None, tq, DV_D), lambda b, h, i: (b, i, h)),
        out_shape=jax.ShapeDtypeStruct((batch, seq, C_D), BF16),
        scratch_shapes=[pltpu.VMEM((seq, DV_D), BF16)],
        compiler_params=_cparams("parallel", "parallel", "arbitrary"),
        name="diff_attention",
    )(lam_p, proj_bd, proj_bd, proj_cd, cos_t, sin_t, subln)


def _router_kernel(h_ref, g_ref, wr_ref, n_ref, comb_ref, idx_ref):
    x = h_ref[...]
    ms = jnp.mean(x * x, axis=-1, keepdims=True)
    n = x * lax.rsqrt(ms + RMS_EPS) * g_ref[...]
    n_ref[...] = n.astype(n_ref.dtype)
    n_hi, n_lo = _split2(n)
    w_hi, w_lo = _split2(wr_ref[...])
    logits = _dot(n_hi, w_hi) + _dot(n_lo, w_hi) + _dot(n_hi, w_lo)
    lane = lax.broadcasted_iota(jnp.int32, logits.shape, 1)
    neg = -jnp.inf
    logits = jnp.where(lane < N_EXPERTS, logits, neg)
    m1 = jnp.max(logits, axis=-1, keepdims=True)
    i1 = jnp.min(jnp.where(logits == m1, lane, LANES), axis=-1, keepdims=True)
    rest = jnp.where(lane == i1, neg, logits)
    m2 = jnp.max(rest, axis=-1, keepdims=True)
    i2 = jnp.min(jnp.where(rest == m2, lane, LANES), axis=-1, keepdims=True)
    e = jnp.exp(m2 - m1)
    g1 = 1.0 / (1.0 + e)
    g2 = e / (1.0 + e)
    comb_ref[...] = jnp.where(lane == i1, g1, 0.0) + jnp.where(lane == i2, g2, 0.0)
    idx_ref[...] = jnp.where(lane == 0, i1, jnp.where(lane == 1, i2, 0))


def moe_router(h, g, w_router_pad, tm=256):
    m, d = h.shape
    spec = pl.BlockSpec((tm, LANES), lambda i: (i, 0))
    return pl.pallas_call(
        _router_kernel,
        grid=(m // tm,),
        in_specs=[pl.BlockSpec((tm, d), lambda i: (i, 0)), pl.BlockSpec((1, d), lambda i: (0, 0)),
                  pl.BlockSpec((d, LANES), lambda i: (0, 0))],
        out_specs=[pl.BlockSpec((tm, d), lambda i: (i, 0)), spec, spec],
        out_shape=[jax.ShapeDtypeStruct((m, d), BF16), jax.ShapeDtypeStruct((m, LANES), F32),
                   jax.ShapeDtypeStruct((m, LANES), jnp.int32)],
        compiler_params=_cparams("parallel"),
        name="moe_router",
    )(h, g, w_router_pad)


def _tile_state(te_ref):
    i = pl.program_id(1)
    in_use = i < te_ref[pl.num_programs(1)]
    new_weights = jnp.logical_or(i == 0, te_ref[i] != te_ref[jnp.maximum(i - 1, 0)])
    return in_use, new_weights


def _gmm_up_kernel(te_ref, a_ref, wg_ref, wu_ref, o_ref, wg_bf, wu_bf):
    in_use, new_weights = _tile_state(te_ref)

    @pl.when(new_weights)
    def _():
        wg_bf[...] = wg_ref[...].astype(BF16)
        wu_bf[...] = wu_ref[...].astype(BF16)

    @pl.when(in_use)
    def _():
        a = a_ref[...]
        g = _dot(a, wg_bf[...])
        u = _dot(a, wu_bf[...])
        o_ref[...] = (g * jax.nn.sigmoid(g) * u).astype(o_ref.dtype)

    @pl.when(jnp.logical_not(in_use))
    def _():
        o_ref[...] = jnp.zeros_like(o_ref)


def _row_spec(tm, width, n_tiles):
    return pl.BlockSpec((tm, width), lambda j, i, te: (jnp.minimum(i, te[n_tiles] - 1), 0))


def gmm_swiglu_up(tile_table, xs, wg, wu, tm, tn):
    s, k = xs.shape
    n = wg.shape[2]
    n_tiles = s // tm
    w_spec = pl.BlockSpec((None, k, tn), lambda j, i, te: (te[i], 0, j))
    return pl.pallas_call(
        _gmm_up_kernel,
        grid_spec=pltpu.PrefetchScalarGridSpec(
            num_scalar_prefetch=1,
            grid=(pl.cdiv(n, tn), n_tiles),
            in_specs=[_row_spec(tm, k, n_tiles), w_spec, w_spec],
            out_specs=pl.BlockSpec((tm, tn), lambda j, i, te: (i, j)),
            scratch_shapes=[pltpu.VMEM((k, tn), BF16), pltpu.VMEM((k, tn), BF16)],
        ),
        out_shape=jax.ShapeDtypeStruct((s, n), BF16),
        compiler_params=_cparams("arbitrary", "arbitrary"),
        name="gmm_swiglu_up",
    )(tile_table, xs, wg, wu)


def _gmm_down_kernel(te_ref, a_ref, w_ref, gate_ref, o_ref):
    in_use = pl.program_id(0) < te_ref[pl.num_programs(0)]

    @pl.when(in_use)
    def _():
        o_ref[...] = _dot(a_ref[...], w_ref[...]) * gate_ref[...]

    @pl.when(jnp.logical_not(in_use))
    def _():
        o_ref[...] = jnp.zeros_like(o_ref)


def gmm_down(tile_table, hs, wd, slot_gate, tm, tn=1024):
    s, k = hs.shape
    n = wd.shape[2]
    n_tiles = s // tm
    return pl.pallas_call(
        _gmm_down_kernel,
        grid_spec=pltpu.PrefetchScalarGridSpec(
            num_scalar_prefetch=1,
            grid=(n_tiles, n // tn),
            in_specs=[pl.BlockSpec((tm, k), lambda i, j, te: (jnp.minimum(i, te[n_tiles] - 1), 0)),
                      pl.BlockSpec((None, k, tn), lambda i, j, te: (te[i], 0, jnp.where(i < te[n_tiles], j, 0))),
                      pl.BlockSpec((tm, 1), lambda i, j, te: (jnp.minimum(i, te[n_tiles] - 1), 0))],
            out_specs=pl.BlockSpec((tm, tn), lambda i, j, te: (i, j)),
        ),
        out_shape=jax.ShapeDtypeStruct((s, n), F32),
        compiler_params=_cparams("parallel", "arbitrary"),
        name="gmm_down",
    )(tile_table, hs, wd, slot_gate)


def moe_layer(h, ffn_g, w_router, wg, wu, wd, tm=MOE_TM):
    m, d = h.shape
    w_router_pad = jnp.pad(w_router.astype(F32), ((0, 0), (0, LANES - N_EXPERTS)))
    n_b, comb, idx = moe_router(h, ffn_g.reshape(1, d).astype(F32), w_router_pad)
    top_idx = idx[:, 0:2]
    gates = jnp.take_along_axis(comb[:, 0:N_EXPERTS], top_idx, axis=1)

    flat_e = top_idx.reshape(-1)
    n_pairs = 2 * m
    onehot = (flat_e[:, None] == jnp.arange(N_EXPERTS, dtype=jnp.int32)[None, :]).astype(jnp.int32)
    counts = jnp.sum(onehot, axis=0)
    rank = jnp.sum((jnp.cumsum(onehot, axis=0) - onehot) * onehot, axis=1)
    padded = ((counts + tm - 1) // tm) * tm
    pad_off = jnp.cumsum(padded) - padded
    raw_off = jnp.cumsum(counts) - counts
    dest = pad_off[flat_e] + rank
    n_tiles = n_pairs // tm + N_EXPERTS
    n_slots = n_tiles * tm
    order = jnp.argsort(flat_e, stable=True).astype(jnp.int32)
    tile_start = jnp.arange(n_tiles, dtype=jnp.int32) * tm
    pad_end = pad_off + padded
    tile_expert = jnp.minimum(jnp.sum((tile_start[:, None] >= pad_end[None, :]).astype(jnp.int32), axis=1),
                              N_EXPERTS - 1).astype(jnp.int32)
    slot = jnp.arange(n_slots, dtype=jnp.int32)
    slot_e = jnp.repeat(tile_expert, tm)
    local = slot - pad_off[slot_e]
    valid = (local < counts[slot_e]) & (slot < pad_end[N_EXPERTS - 1])
    src_pair = order[jnp.clip(raw_off[slot_e] + local, 0, n_pairs - 1)]
    slot_token = jnp.where(valid, src_pair // 2, 0)
    slot_gate = jnp.where(valid, gates.reshape(-1)[src_pair], 0.0).astype(F32).reshape(n_slots, 1)

    tiles_in_use = (pad_end[N_EXPERTS - 1] // tm).astype(jnp.int32).reshape(1)
    tile_table = jnp.concatenate([tile_expert, tiles_in_use])

    xs = jnp.take(n_b, slot_token, axis=0, mode="clip")
    hs = gmm_swiglu_up(tile_table, xs, wg, wu, tm, min(MOE_TN, wg.shape[2]))
    ys = gmm_down(tile_table, hs, cast_experts_bf16(wd), slot_gate, tm, min(1024, wd.shape[2]))
    dest2 = dest.reshape(m, 2)
    return h + jnp.take(ys, dest2[:, 0], axis=0) + jnp.take(ys, dest2[:, 1], axis=0)


def _rwkv_columns(w_in, layer):
    d = w_in.shape[0]
    o_w = C_A
    o_k = o_w + W_LORA
    o_v = o_k + C_A
    o_a = o_v + C_A
    o_g = o_a + A_LORA
    parts = [w_in[:, 0:C_A], w_in[:, o_k:o_k + C_A], w_in[:, o_v:o_v + C_A],
             w_in[:, o_w:o_w + W_LORA], w_in[:, o_a:o_a + A_LORA], w_in[:, o_g:o_g + G_LORA]]
    used = W_LORA + A_LORA + G_LORA
    if layer > 0:
        parts.append(w_in[:, N_IN0:N_IN0 + V_LORA])
        used += V_LORA
    parts.append(jnp.zeros((d, LORA_PAD - used), w_in.dtype))
    return jnp.concatenate(parts, axis=1)


def _rwkv_mu(mu, mu_v):
    o_w = C_A
    o_k = o_w + W_LORA
    o_v = o_k + C_A
    o_a = o_v + C_A
    o_g = o_a + A_LORA
    parts = [mu[0:C_A], mu[o_k:o_k + C_A], mu[o_v:o_v + C_A],
             mu[o_w:o_w + W_LORA], mu[o_a:o_a + A_LORA], mu[o_g:o_g + G_LORA]]
    used = W_LORA + A_LORA + G_LORA
    if mu_v is not None:
        parts.append(mu_v)
        used += V_LORA
    parts.append(jnp.zeros((LORA_PAD - used,), mu.dtype))
    return jnp.concatenate(parts).reshape(1, N_A_COLS).astype(F32)


def _pad_rows(w, offset):
    return jnp.pad(w, ((offset, LORA_PAD - offset - w.shape[0]), (0, 0))).astype(BF16)


def kernel(x, p, positions, attn_norm, w_in0, w_in_rest, w_out, rwkv_mu, rwkv_mu_v, rwkv_w0, rwkv_w_up, rwkv_a0, rwkv_a_up, rwkv_v0, rwkv_v_up, rwkv_g_up, rwkv_k_k, rwkv_k_a, rwkv_r_k, rwkv_lnx_g, rwkv_lnx_b, gm_ln_g, gm_ln_b, gm_w_s, gm_b_s, gm_out_g, sb_out_g, diff_lambda, diff_subln, ffn_norm, dense_w_gate, dense_w_up, dense_w_down, moe_router, moe_w_gate, moe_w_up, moe_w_down, ple_norm, ple_w_gate, ple_w_proj, final_norm):
    batch, seq, d = x.shape
    m = batch * seq
    h = x.reshape(m, d).astype(F32)

    inv_freq = 1.0 / (ROPE_THETA ** (jnp.arange(0, DH_D, 2, dtype=F32) / DH_D))
    invf = jnp.tile(inv_freq, LANES // (DH_D // 2)).reshape(1, LANES)
    cos_t, sin_t = rope_tables(positions.reshape(m, 1), invf)

    v_first = None
    for i in range(DEPTH):
        w_in = w_in0 if i == 0 else w_in_rest[i - 1]
        o_b = N_RWKV
        o_c = o_b + N_GMLP
        o_d = o_c + N_SB
        w_a = _rwkv_columns(w_in, i).astype(BF16)
        w_bd = jnp.concatenate([w_in[:, o_b:o_c], w_in[:, o_d:o_d + 2 * C_D]], axis=1).astype(BF16)
        w_cd = jnp.concatenate([w_in[:, o_c:o_d], w_in[:, o_d + 2 * C_D:o_d + 3 * C_D]], axis=1).astype(BF16)

        n = rmsnorm(h, attn_norm[i], BF16)
        proj_a = matmul(n, w_a, F32)
        proj_bd = matmul(n, w_bd, F32)
        proj_cd = matmul(n, w_cd, BF16)

        mu = _rwkv_mu(rwkv_mu[i], rwkv_mu_v[i - 1] if i > 0 else None)
        zeros_c = jnp.zeros((C_A,), F32)
        vec = jnp.stack([rwkv_w0[i], rwkv_a0[i], rwkv_v0[i - 1] if i > 0 else zeros_c, rwkv_k_k[i], rwkv_k_a[i],
                         rwkv_r_k[i].reshape(C_A), rwkv_lnx_g[i], rwkv_lnx_b[i]]).astype(F32)
        uw = _pad_rows(rwkv_w_up[i], 0)
        ua = _pad_rows(rwkv_a_up[i], W_LORA)
        ug = _pad_rows(rwkv_g_up[i], W_LORA + A_LORA)
        uv = _pad_rows(rwkv_v_up[i - 1], W_LORA + A_LORA + G_LORA) if i > 0 else jnp.zeros((LORA_PAD, C_A), BF16)
        vf_in = v_first if i > 0 else proj_a
        y_a, v_out = rwkv_mix(proj_a, vf_in, mu, vec, uw, ua, ug, uv, batch, seq, has_v_res=i > 0)
        if i == 0:
            v_first = v_out

        y_b = gmlp_mix(proj_bd, gm_ln_g[i].reshape(1, C_B), gm_ln_b[i].reshape(1, C_B), gm_w_s[i],
                       gm_b_s[i].T, gm_out_g[i].reshape(1, C_B))

        proj_cd3 = proj_cd.reshape(batch, seq, 4 * C_C)
        y_c = sb_attention(proj_cd3, sb_out_g[i].reshape(1, C_C), batch, seq).reshape(m, C_C)

        lam_init = 0.8 - 0.6 * math.exp(-0.3 * i)
        y_d = diff_attention(diff_lambda[i].astype(F32), proj_bd.reshape(batch, seq, N_GMLP + 2 * C_D), proj_cd3,
                             cos_t.reshape(batch, seq, LANES), sin_t.reshape(batch, seq, LANES),
                             diff_subln[i].reshape(1, DV_D), batch, seq, lam_init).reshape(m, C_D)

        h = mix_out_proj((y_a, y_b, y_c, y_d), w_out[i].astype(BF16), h)

        j = i // 2
        if i % 2 == 0:
            n = rmsnorm(h, ffn_norm[i], BF16)
            n_row_tiles = m // DENSE_TM
            one_group = jnp.concatenate([jnp.zeros((n_row_tiles,), jnp.int32), jnp.full((1,), n_row_tiles, jnp.int32)])
            hid = gmm_swiglu_up(one_group, n, dense_w_gate[j][None], dense_w_up[j][None], DENSE_TM, DENSE_TN)
            h = down_proj_residual(hid, dense_w_down[j].astype(BF16), h)
        else:
            h = moe_layer(h, ffn_norm[i], moe_router[j], moe_w_gate[j], moe_w_up[j], moe_w_down[j])

        n = rmsnorm(h, ple_norm[i], BF16)
        h = ple_residual(n, ple_w_gate[i].astype(BF16), p[i].reshape(m, P_DIM).astype(BF16),
                         ple_w_proj[i].astype(BF16), h)

    return rmsnorm(h, final_norm, x.dtype).reshape(batch, seq, d)
```

```python
import functools
import math

import jax
import jax.numpy as jnp
import numpy as np
from jax import lax
from jax.experimental import pallas as pl
from jax.experimental.pallas import tpu as pltpu

F32 = jnp.float32
BF16 = jnp.bfloat16

D_MODEL = 4096
DEPTH = 2
C_A = 1024
N_A = 64
W_LORA, A_LORA, V_LORA, G_LORA = 64, 64, 32, 160
C_B = 1024
H_B = 8
CHUNK = 128
C_C = 1024
H_C = 8
D_C = 128
C_D = 1024
H_D = 8
DV_D = 128
DH_D = 64
D_FF = 11008
N_EXPERTS = 8
D_FF_EXPERT = 5632
P_DIM = 256
ROPE_THETA = 10000.0
RMS_EPS = 1e-6
LN_EPS = 1e-5
GN_EPS = 64e-5
L2_EPS = 1e-12
N_RWKV = 3 * C_A + W_LORA + A_LORA + G_LORA
N_GMLP = 2 * C_B
N_SB = 3 * C_C
N_DIFF = 3 * C_D
N_IN0 = N_RWKV + N_GMLP + N_SB + N_DIFF

LANES = 128
VMEM_LIMIT = 56 * 1024 * 1024

LORA_PAD = 512
N_A_COLS = 3 * C_A + LORA_PAD
DENSE_TM, DENSE_TN = 1024, 256
DENSE_TK = 5504
MOE_TM, MOE_TN = 512, 512
RWKV_CHUNK = 64
RWKV_GROUP = 4
SB_CUTOFF = -110.0
SB_HEADS = 4

NT = (((1,), (1,)), ((), ()))
TN = (((0,), (0,)), ((), ()))


def _cparams(*sem):
    return pltpu.CompilerParams(dimension_semantics=sem, vmem_limit_bytes=VMEM_LIMIT)


def _dot(a, b, dims=None):
    if dims is None:
        return jnp.dot(a, b, preferred_element_type=F32)
    return lax.dot_general(a, b, dims, preferred_element_type=F32)


def _split2(x):
    hi = x.astype(BF16)
    lo = (x - hi.astype(F32)).astype(BF16)
    return hi, lo


def _split3(x):
    hi = x.astype(BF16)
    r = x - hi.astype(F32)
    mid = r.astype(BF16)
    lo = (r - mid.astype(F32)).astype(BF16)
    return hi, mid, lo


def _pick_tile(n, prefs):
    for t in prefs:
        if n % t == 0:
            return t
    raise ValueError(f"no tile for {n}")


def _rmsnorm_kernel(x_ref, g_ref, o_ref, *, eps):
    x = x_ref[...].astype(F32)
    ms = jnp.mean(x * x, axis=-1, keepdims=True)
    o_ref[...] = (x * lax.rsqrt(ms + eps) * g_ref[...]).astype(o_ref.dtype)


def rmsnorm(x, g, out_dtype, tm=512):
    m, d = x.shape
    return pl.pallas_call(
        functools.partial(_rmsnorm_kernel, eps=RMS_EPS),
        grid=(m // tm,),
        in_specs=[pl.BlockSpec((tm, d), lambda i: (i, 0)), pl.BlockSpec((1, d), lambda i: (0, 0))],
        out_specs=pl.BlockSpec((tm, d), lambda i: (i, 0)),
        out_shape=jax.ShapeDtypeStruct((m, d), out_dtype),
        compiler_params=_cparams("parallel"),
        name="rmsnorm",
    )(x, g.reshape(1, d).astype(F32))


def _cast_kernel(x_ref, o_ref):
    o_ref[...] = x_ref[...].astype(o_ref.dtype)


def cast_experts_bf16(w, rows=512):
    e, k, n = w.shape
    rows = min(rows, k)
    spec = pl.BlockSpec((None, rows, n), lambda i, j: (i, j, 0))
    return pl.pallas_call(
        _cast_kernel,
        grid=(e, k // rows),
        in_specs=[spec],
        out_specs=spec,
        out_shape=jax.ShapeDtypeStruct(w.shape, BF16),
        compiler_params=_cparams("parallel", "parallel"),
        name="cast_experts_bf16",
    )(w)


def _mm_kernel(a_ref, b_ref, o_ref):
    o_ref[...] = _dot(a_ref[...], b_ref[...]).astype(o_ref.dtype)


def matmul(a, b, out_dtype, tm=1024):
    m, k = a.shape
    n = b.shape[1]
    tn = _pick_tile(n, (1024, 512, 256, 128))
    return pl.pallas_call(
        _mm_kernel,
        grid=(m // tm, n // tn),
        in_specs=[pl.BlockSpec((tm, k), lambda i, j: (i, 0)), pl.BlockSpec((k, tn), lambda i, j: (0, j))],
        out_specs=pl.BlockSpec((tm, tn), lambda i, j: (i, j)),
        out_shape=jax.ShapeDtypeStruct((m, n), out_dtype),
        compiler_params=_cparams("parallel", "arbitrary"),
        name="matmul",
    )(a, b)


def _mix_out_kernel(ya_ref, yb_ref, yc_ref, yd_ref, w_ref, res_ref, o_ref):
    kq = ya_ref.shape[1]
    acc = res_ref[...]
    for idx, y_ref in enumerate((ya_ref, yb_ref, yc_ref, yd_ref)):
        acc = acc + _dot(y_ref[...], w_ref[idx * kq:(idx + 1) * kq, :])
    o_ref[...] = acc


def mix_out_proj(ys, w, res, tm=1024, tn=512):
    m, kq = ys[0].shape
    k, n = w.shape
    y_spec = pl.BlockSpec((tm, kq), lambda i, j: (i, 0))
    return pl.pallas_call(
        _mix_out_kernel,
        grid=(m // tm, n // tn),
        in_specs=[y_spec, y_spec, y_spec, y_spec,
                  pl.BlockSpec((k, tn), lambda i, j: (0, j)),
                  pl.BlockSpec((tm, tn), lambda i, j: (i, j))],
        out_specs=pl.BlockSpec((tm, tn), lambda i, j: (i, j)),
        out_shape=jax.ShapeDtypeStruct((m, n), F32),
        compiler_params=_cparams("parallel", "arbitrary"),
        name="mix_out_proj",
    )(*ys, w, res)


def _down_res_kernel(a_ref, b_ref, res_ref, o_ref, acc_ref):
    kk = pl.program_id(2)

    @pl.when(kk == 0)
    def _():
        acc_ref[...] = res_ref[...]

    acc_ref[...] += _dot(a_ref[...], b_ref[...])

    @pl.when(kk == pl.num_programs(2) - 1)
    def _():
        o_ref[...] = acc_ref[...]


def down_proj_residual(a, b, res, tm=1024, tn=512, tk=DENSE_TK):
    m, k = a.shape
    n = b.shape[1]
    return pl.pallas_call(
        _down_res_kernel,
        grid=(m // tm, n // tn, k // tk),
        in_specs=[pl.BlockSpec((tm, tk), lambda i, j, kk: (i, kk)),
                  pl.BlockSpec((tk, tn), lambda i, j, kk: (kk, j)),
                  pl.BlockSpec((tm, tn), lambda i, j, kk: (i, j))],
        out_specs=pl.BlockSpec((tm, tn), lambda i, j, kk: (i, j)),
        out_shape=jax.ShapeDtypeStruct((m, n), F32),
        scratch_shapes=[pltpu.VMEM((tm, tn), F32)],
        compiler_params=_cparams("parallel", "parallel", "arbitrary"),
        name="down_proj_residual",
    )(a, b, res)


def _ple_kernel(n_ref, wg_ref, p_ref, wp_ref, res_ref, o_ref):
    gate = jax.nn.sigmoid(_dot(n_ref[...], wg_ref[...]))
    emb = _dot(p_ref[...], wp_ref[...])
    o_ref[...] = res_ref[...] + gate * emb


def ple_residual(n, wg, p, wp, res, tm=1024, tn=512):
    m, k = n.shape
    nn = wg.shape[1]
    pd = p.shape[1]
    return pl.pallas_call(
        _ple_kernel,
        grid=(m // tm, nn // tn),
        in_specs=[pl.BlockSpec((tm, k), lambda i, j: (i, 0)),
                  pl.BlockSpec((k, tn), lambda i, j: (0, j)),
                  pl.BlockSpec((tm, pd), lambda i, j: (i, 0)),
                  pl.BlockSpec((pd, tn), lambda i, j: (0, j)),
                  pl.BlockSpec((tm, tn), lambda i, j: (i, j))],
        out_specs=pl.BlockSpec((tm, tn), lambda i, j: (i, j)),
        out_shape=jax.ShapeDtypeStruct((m, nn), F32),
        compiler_params=_cparams("parallel", "arbitrary"),
        name="ple_residual",
    )(n, wg, p, wp, res)


def _head_sum(x, ones_bd, split=False):
    outs = []
    for p in range(x.shape[1] // LANES):
        xp = x[:, p * LANES:(p + 1) * LANES]
        if split:
            hi, lo = _split2(xp)
            outs.append(_dot(hi, ones_bd) + _dot(lo, ones_bd))
        else:
            outs.append(_dot(xp.astype(BF16), ones_bd))
    return jnp.concatenate(outs, axis=1)


def _rwkv_kernel(z_ref, vf_ref, mu_ref, vec_ref, uw_ref, ua_ref, ug_ref, uv_ref,
                 y_ref, vout_ref, prev_ref, h_ref, *, has_v_res):
    tc = RWKV_CHUNK
    tr = z_ref.shape[0]
    chunks = range(tr // tc)
    c = pl.program_id(1)

    @pl.when(c == 0)
    def _():
        prev_ref[...] = jnp.zeros_like(prev_ref)
        h_ref[...] = jnp.zeros_like(h_ref)

    z = z_ref[...]
    row = lax.broadcasted_iota(jnp.int32, z.shape, 0)
    zp = jnp.where(row == 0, prev_ref[...], pltpu.roll(z, 1, 0))
    prev_ref[...] = z[tr - 1:tr, :]
    zs = z + (zp - z) * mu_ref[...]
    r = zs[:, 0:C_A]
    k = zs[:, C_A:2 * C_A]
    v = zs[:, 2 * C_A:3 * C_A]
    lr = zs[:, 3 * C_A:3 * C_A + LORA_PAD]

    w0 = vec_ref[0:1, :]
    a0 = vec_ref[1:2, :]
    v0 = vec_ref[2:3, :]
    k_k = vec_ref[3:4, :]
    k_a = vec_ref[4:5, :]
    r_k = vec_ref[5:6, :]
    lnx_g = vec_ref[6:7, :]
    lnx_b = vec_ref[7:8, :]

    lr_b = lr.astype(BF16)
    w_lin = _dot(jnp.tanh(lr).astype(BF16), uw_ref[...])
    a_lin = _dot(lr_b, ua_ref[...])
    g = _dot(jax.nn.sigmoid(lr).astype(BF16), ug_ref[...])
    w = -jax.nn.softplus(-(w0 + w_lin)) - 0.5
    ld = -jnp.exp(w)
    a_lr = jax.nn.sigmoid(a0 + a_lin)
    if has_v_res:
        v_lin = _dot(lr_b, uv_ref[...])
        v = v + (vf_ref[...] - v) * jax.nn.sigmoid(v0 + v_lin)
    vout_ref[...] = v

    li = lax.broadcasted_iota(jnp.int32, (LANES, LANES), 0)
    lj = lax.broadcasted_iota(jnp.int32, (LANES, LANES), 1)
    same_head = (li // N_A) == (lj // N_A)
    ones_bd = jnp.where(same_head, 1.0, 0.0).astype(BF16)

    kk = k * k_k
    kk = kk / jnp.maximum(jnp.sqrt(_head_sum(kk * kk, ones_bd, split=True)), L2_EPS)
    k = k * (1.0 + (a_lr - 1.0) * k_a)
    a_s = -kk
    b_s = kk * a_lr

    ti = lax.broadcasted_iota(jnp.int32, (tr, tr), 0)
    tj = lax.broadcasted_iota(jnp.int32, (tr, tr), 1)
    tri = jnp.where(((ti // tc) == (tj // tc)) & (ti >= tj), 1.0, 0.0).astype(BF16)
    ld_h, ld_m, ld_l = _split3(ld)
    cum = _dot(tri, ld_h) + _dot(tri, ld_m) + _dot(tri, ld_l)
    cum_last = [cum[(ci + 1) * tc - 1:(ci + 1) * tc, :] for ci in chunks]
    cum_end = jnp.concatenate([jnp.broadcast_to(x, (tc, C_A)) for x in cum_last], axis=0)
    p_inc = jnp.exp(cum)
    p_exc = jnp.exp(cum - ld)
    p_inv = jnp.exp(-cum)
    p_end = jnp.exp(cum_end - cum)
    decay_end = [jnp.exp(x) for x in cum_last]

    a_t = a_s * p_exc
    b_t = b_s * p_inv
    k_t = k * p_inv
    r_t = r * p_inc
    b_h = b_s * p_end
    k_h = k * p_end

    first_head = lax.broadcasted_iota(jnp.int32, (tc, LANES), 1) < N_A
    strict = same_head & ((li % N_A) > (lj % N_A))
    incl = same_head & ((li % N_A) >= (lj % N_A))
    eye = li == lj

    def stack(x):
        return jnp.concatenate([jnp.where(first_head, x, 0.0), jnp.where(first_head, 0.0, x)], axis=0)

    def dup(x):
        return jnp.concatenate([x, x], axis=0)

    n_pairs = C_A // LANES
    chains = [(ci, p) for ci in chunks for p in range(n_pairs)]
    pairs = range(len(chains))
    cut = [(slice(ci * tc, (ci + 1) * tc), slice(p * LANES, (p + 1) * LANES)) for ci, p in chains]
    n2 = 2 * tc
    a_st = [stack(a_t[rs, sl]) for rs, sl in cut]
    r_st = [stack(r_t[rs, sl]) for rs, sl in cut]
    v_st = [stack(v[rs, sl]).astype(BF16) for rs, sl in cut]
    bh_st = [stack(b_h[rs, sl]).astype(BF16) for rs, sl in cut]
    kh_st = [stack(k_h[rs, sl]).astype(BF16) for rs, sl in cut]
    lhs = [jnp.concatenate([a_st[p], r_st[p]], axis=0).astype(BF16) for p in pairs]
    rhs = [jnp.concatenate([dup(b_t[rs, sl]), dup(k_t[rs, sl])], axis=0).astype(BF16) for rs, sl in cut]
    x = [_dot(lhs[p], rhs[p], NT) for p in pairs]
    lk = [jnp.where(strict, x[p][0:n2, 0:n2], 0.0) for p in pairs]
    a_ak = [jnp.where(strict, x[p][0:n2, n2:2 * n2], 0.0).astype(BF16) for p in pairs]
    a_rb = [jnp.where(incl, x[p][n2:2 * n2, 0:n2], 0.0).astype(BF16) for p in pairs]
    a_rk = [jnp.where(incl, x[p][n2:2 * n2, n2:2 * n2], 0.0).astype(BF16) for p in pairs]
    zz = [jnp.concatenate([a_st[p], _dot(a_ak[p], v_st[p])], axis=1) for p in pairs]
    n_iter = int(math.log2(tc))
    for it in range(n_iter):
        lk_b = [lk[p].astype(BF16) for p in pairs]
        zz = [zz[p] + _dot(lk_b[p], zz[p].astype(BF16)) for p in pairs]
        if it < n_iter - 1:
            lk = [_dot(lk_b[p], lk_b[p]) for p in pairs]
    zz_b = [zz[p].astype(BF16) for p in pairs]
    qy = [_dot(a_rb[p], zz_b[p]) for p in pairs]
    y3 = [qy[p][:, LANES:] + _dot(a_rk[p], v_st[p]) for p in pairs]
    mg = [_dot(bh_st[p], zz_b[p], TN) for p in pairs]
    g_mat = [mg[p][:, LANES:] + _dot(kh_st[p], v_st[p], TN) for p in pairs]
    qm = [jnp.concatenate([r_st[p] + qy[p][:, 0:LANES],
                           jnp.where(eye, decay_end[chains[p][0]][:, cut[p][1]], 0.0) + mg[p][:, 0:LANES]],
                          axis=0).astype(BF16) for p in pairs]
    state = [h_ref[p] for p in range(n_pairs)]
    y_rows = []
    for ci in chunks:
        first = ci * n_pairs
        out = [_dot(qm[first + p], state[p].astype(BF16)) for p in range(n_pairs)]
        y_parts = []
        for p in range(n_pairs):
            y_st = out[p][0:n2, :] + y3[first + p]
            y_parts.append(y_st[0:tc, :] + y_st[tc:n2, :])
            state[p] = out[p][n2:, :] + g_mat[first + p]
        y_rows.append(jnp.concatenate(y_parts, axis=1))
    for p in range(n_pairs):
        h_ref[p] = state[p]
    y = jnp.concatenate(y_rows, axis=0)

    inv_n = 1.0 / N_A
    mean = _head_sum(y, ones_bd) * inv_n
    d = y - mean
    var = _head_sum(d * d, ones_bd) * inv_n
    yn = d * lax.rsqrt(var + GN_EPS) * lnx_g + lnx_b
    bonus = _head_sum(r * k * r_k, ones_bd) * v
    y_ref[...] = ((yn + bonus) * g).astype(y_ref.dtype)


def rwkv_mix(proj_a, v_first, mu, vec, uw, ua, ug, uv, batch, seq, has_v_res):
    tr = RWKV_CHUNK * RWKV_GROUP
    nc = seq // tr
    row_spec = lambda w: pl.BlockSpec((tr, w), lambda b, c: (b * nc + c, 0))
    full = lambda s: pl.BlockSpec(s, lambda b, c: tuple(0 for _ in s))
    m = batch * seq
    return pl.pallas_call(
        functools.partial(_rwkv_kernel, has_v_res=has_v_res),
        grid=(batch, nc),
        in_specs=[row_spec(N_A_COLS), row_spec(C_A), full((1, N_A_COLS)), full((8, C_A)),
                  full((LORA_PAD, C_A)), full((LORA_PAD, C_A)), full((LORA_PAD, C_A)), full((LORA_PAD, C_A))],
        out_specs=[row_spec(C_A), row_spec(C_A)],
        out_shape=[jax.ShapeDtypeStruct((m, C_A), BF16), jax.ShapeDtypeStruct((m, C_A), F32)],
        scratch_shapes=[pltpu.VMEM((1, N_A_COLS), F32), pltpu.VMEM((C_A // LANES, LANES, LANES), F32)],
        compiler_params=_cparams("parallel", "arbitrary"),
        name="rwkv_mix",
    )(proj_a, v_first, mu, vec, uw, ua, ug, uv)


def _gelu(x):
    return 0.5 * x * (1.0 + lax.erf(x * math.sqrt(0.5)))


def _gmlp_kernel(z_ref, lng_ref, lnb_ref, w_ref, bs_ref, og_ref, o_ref):
    u = _gelu(z_ref[:, 0:C_B])
    v = _gelu(z_ref[:, C_B:2 * C_B])
    mu = jnp.mean(v, axis=-1, keepdims=True)
    d = v - mu
    var = jnp.mean(d * d, axis=-1, keepdims=True)
    vn = d * lax.rsqrt(var + LN_EPS) * lng_ref[...] + lnb_ref[...]
    ti = lax.broadcasted_iota(jnp.int32, (CHUNK, CHUNK), 0)
    tj = lax.broadcasted_iota(jnp.int32, (CHUNK, CHUNK), 1)
    causal = ti >= tj
    dh = C_B // H_B
    for h in range(H_B):
        sl = slice(h * dh, (h + 1) * dh)
        w = jnp.where(causal, w_ref[h], 0.0).astype(BF16)
        s = _dot(w, vn[:, sl].astype(BF16)) + bs_ref[:, h:h + 1]
        y = u[:, sl] * s
        ms = jnp.mean(y * y, axis=-1, keepdims=True)
        o_ref[:, sl] = (y * lax.rsqrt(ms + RMS_EPS) * og_ref[:, sl]).astype(o_ref.dtype)


def gmlp_mix(proj_bd, ln_g, ln_b, w_s, b_s_t, out_g):
    m = proj_bd.shape[0]
    full = lambda s: pl.BlockSpec(s, lambda i: tuple(0 for _ in s))
    return pl.pallas_call(
        _gmlp_kernel,
        grid=(m // CHUNK,),
        in_specs=[pl.BlockSpec((CHUNK, 2 * C_B), lambda i: (i, 0)),
                  full((1, C_B)), full((1, C_B)), full((H_B, CHUNK, CHUNK)), full((CHUNK, H_B)), full((1, C_B))],
        out_specs=pl.BlockSpec((CHUNK, C_B), lambda i: (i, 0)),
        out_shape=jax.ShapeDtypeStruct((m, C_B), BF16),
        compiler_params=_cparams("parallel"),
        name="gmlp_mix",
    )(proj_bd, ln_g, ln_b, w_s, b_s_t, out_g)


def _sb_kernel(q_ref, k_ref, v_ref, g_ref, o_ref, *, tq, scale):
    qi = pl.program_id(2)
    heads = range(q_ref.shape[1] // D_C)
    hs = [slice(h * D_C, (h + 1) * D_C) for h in heads]
    q = [q_ref[:, s] for s in hs]
    row = lax.broadcasted_iota(jnp.int32, (tq, tq), 0)
    col = lax.broadcasted_iota(jnp.int32, (tq, tq), 1)
    below = col < row
    upper = jnp.where(row > col, 1.0, 0.0).astype(BF16)

    def block(j, carry, acc, keep):
        start = pl.multiple_of(j * tq, tq)
        z = [_dot(q[h], k_ref[pl.ds(start, tq), hs[h]], NT) * scale for h in heads]
        log_beta = [jnp.minimum(z[h], 0.0) - jnp.log(1.0 + jnp.exp(-jnp.abs(z[h]))) for h in heads]
        log_1mb = [log_beta[h] - z[h] for h in heads]
        if keep is not None:
            log_1mb = [jnp.where(keep, x, 0.0) for x in log_1mb]
        parts = [_split2(x) for x in log_1mb]
        after = [_dot(hi, upper) + _dot(lo, upper) for hi, lo in parts]
        att = [jnp.exp(log_beta[h] + after[h] + carry[h]) for h in heads]
        if keep is not None:
            att = [jnp.where(keep, x, 0.0) for x in att]
        acc = [acc[h] + _dot(att[h].astype(BF16), v_ref[pl.ds(start, tq), hs[h]]) for h in heads]
        carry = [carry[h] + after[h][:, 0:1] + log_1mb[h][:, 0:1] for h in heads]
        return carry, acc

    carry = [jnp.zeros((tq, 1), F32) for _ in heads]
    acc = [jnp.zeros((tq, D_C), F32) for _ in heads]
    carry, acc = block(qi, carry, acc, below)
    has_prev = jnp.broadcast_to(qi >= 1, (tq, tq))
    carry, acc = block(jnp.maximum(qi - 1, 0), carry, acc, has_prev)

    def more(state):
        top = functools.reduce(jnp.maximum, [jnp.max(c) for c in state[1]])
        return jnp.logical_and(state[0] >= 0, top > SB_CUTOFF)

    def body(state):
        carry_j, acc_j = block(state[0], list(state[1]), list(state[2]), None)
        return state[0] - 1, tuple(carry_j), tuple(acc_j)

    _, carry, acc = lax.while_loop(more, body, (qi - 2, tuple(carry), tuple(acc)))
    for h in heads:
        ms = jnp.mean(acc[h] * acc[h], axis=-1, keepdims=True)
        o_ref[:, hs[h]] = (acc[h] * lax.rsqrt(ms + RMS_EPS) * g_ref[:, hs[h]]).astype(o_ref.dtype)


def sb_attention(proj_cd, out_g, batch, seq, tq=256, heads=SB_HEADS):
    nq = seq // tq
    w = heads * D_C
    groups = H_C // heads
    return pl.pallas_call(
        functools.partial(_sb_kernel, tq=tq, scale=D_C ** -0.5),
        grid=(batch, groups, nq),
        in_specs=[pl.BlockSpec((None, tq, w), lambda b, h, i: (b, i, h)),
                  pl.BlockSpec((None, seq, w), lambda b, h, i: (b, 0, groups + h)),
                  pl.BlockSpec((None, seq, w), lambda b, h, i: (b, 0, 2 * groups + h)),
                  pl.BlockSpec((1, w), lambda b, h, i: (0, h))],
        out_specs=pl.BlockSpec((None, tq, w), lambda b, h, i: (b, i, h)),
        out_shape=jax.ShapeDtypeStruct((batch, seq, C_C), BF16),
        compiler_params=_cparams("parallel", "parallel", "arbitrary"),
        name="sb_attention",
    )(proj_cd, proj_cd, proj_cd, out_g)


def _rope_table_kernel(pos_ref, invf_ref, cos_ref, sin_ref):
    ang = pos_ref[...].astype(F32) * invf_ref[...]
    cos_ref[...] = jnp.cos(ang)
    sin_ref[...] = jnp.sin(ang)


def rope_tables(pos_col, invf, tm=512):
    m = pos_col.shape[0]
    spec = pl.BlockSpec((tm, LANES), lambda i: (i, 0))
    return pl.pallas_call(
        _rope_table_kernel,
        grid=(m // tm,),
        in_specs=[pl.BlockSpec((tm, 1), lambda i: (i, 0)), pl.BlockSpec((1, LANES), lambda i: (0, 0))],
        out_specs=[spec, spec],
        out_shape=[jax.ShapeDtypeStruct((m, LANES), F32)] * 2,
        compiler_params=_cparams("parallel"),
        name="rope_tables",
    )(pos_col, invf)


def _rope(x, c, s):
    lane = lax.broadcasted_iota(jnp.int32, x.shape, 1)
    first_half = (lane % DH_D) < (DH_D // 2)
    rot = jnp.where(first_half, -pltpu.roll(x, LANES - DH_D // 2, 1), pltpu.roll(x, DH_D // 2, 1))
    return x * c + rot * s


def _diff_kernel(lam_ref, q_ref, k_ref, v_ref, cos_ref, sin_ref, g_ref, o_ref, k_scr, *, tq, lam_init):
    qi = pl.program_id(2)

    @pl.when(qi == 0)
    def _():
        def rope_rows(t, carry):
            rows = pl.ds(pl.multiple_of(t * tq, tq), tq)
            k_scr[rows, :] = _rope(k_ref[rows, :], cos_ref[rows, :], sin_ref[rows, :]).astype(BF16)
            return carry

        lax.fori_loop(0, k_ref.shape[0] // tq, rope_rows, 0)

    q_rows = pl.ds(pl.multiple_of(qi * tq, tq), tq)
    q = (_rope(q_ref[...], cos_ref[q_rows, :], sin_ref[q_rows, :]) * (DH_D ** -0.5)).astype(BF16)
    lane = lax.broadcasted_iota(jnp.int32, q.shape, 1)
    zero = jnp.zeros_like(q)
    qs = jnp.concatenate([jnp.where(lane < DH_D, q, zero), jnp.where(lane < DH_D, zero, q)], axis=0)
    row = lax.broadcasted_iota(jnp.int32, (2 * tq, tq), 0) % tq
    col = lax.broadcasted_iota(jnp.int32, (2 * tq, tq), 1)
    visible = col <= row

    def block(j, m, l, acc, diagonal):
        start = pl.multiple_of(j * tq, tq)
        kb = k_scr[pl.ds(start, tq), :]
        vb = v_ref[pl.ds(start, tq), :]
        s = _dot(qs, kb, NT)
        if diagonal:
            s = jnp.where(visible, s, -jnp.inf)
        m_new = jnp.maximum(m, jnp.max(s, axis=-1, keepdims=True))
        alpha = jnp.exp(m - m_new)
        p = jnp.exp(s - m_new)
        l = alpha * l + jnp.sum(p, axis=-1, keepdims=True)
        acc = alpha * acc + _dot(p.astype(BF16), vb)
        return m_new, l, acc

    m0 = jnp.full((2 * tq, 1), -jnp.inf, F32)
    l0 = jnp.zeros((2 * tq, 1), F32)
    acc0 = jnp.zeros((2 * tq, DV_D), F32)
    m, l, acc = block(qi, m0, l0, acc0, True)

    def grouped(first, group):
        def body(i, state):
            for t in range(group):
                state = block(first + i * group + t, state[0], state[1], state[2], False)
            return state
        return body

    n4 = qi // 4
    n2 = (qi - 4 * n4) // 2
    n1 = qi - 4 * n4 - 2 * n2
    state = lax.fori_loop(0, n4, grouped(0, 4), (m, l, acc))
    state = lax.fori_loop(0, n2, grouped(4 * n4, 2), state)
    m, l, acc = lax.fori_loop(0, n1, grouped(4 * n4 + 2 * n2, 1), state)
    lv = lam_ref[...]
    lam = (jnp.exp(jnp.sum(lv[0:1, :] * lv[1:2, :], axis=-1, keepdims=True))
           - jnp.exp(jnp.sum(lv[2:3, :] * lv[3:4, :], axis=-1, keepdims=True)) + lam_init)
    o = acc / l
    y = o[0:tq, :] - lam * o[tq:2 * tq, :]
    ms = jnp.mean(y * y, axis=-1, keepdims=True)
    o_ref[...] = (y * lax.rsqrt(ms + LN_EPS) * g_ref[...] * (1.0 - lam_init)).astype(o_ref.dtype)


def diff_attention(lam_p, proj_bd, proj_cd, cos_t, sin_t, subln, batch, seq, lam_init, tq=512):
    nq = seq // tq
    q0 = N_GMLP // DV_D
    table = pl.BlockSpec((None, seq, LANES), lambda b, h, i: (b, 0, 0))
    return pl.pallas_call(
        functools.partial(_diff_kernel, tq=tq, lam_init=lam_init),
        grid=(batch, H_D, nq),
        in_specs=[pl.BlockSpec((4, DH_D), lambda b, h, i: (0, 0)),
                  pl.BlockSpec((None, tq, DV_D), lambda b, h, i: (b, i, q0 + h)),
                  pl.BlockSpec((None, seq, DV_D), lambda b, h, i: (b, 0, q0 + H_D + h)),
                  pl.BlockSpec((None, seq, DV_D), lambda b, h, i: (b, 0, 3 * H_C + h)),
                  table, table,
                  pl.BlockSpec((1, DV_D), lambda b, h, i: (0, 0))],
        out_specs=pl.BlockSpec((None, tq, DV_D), lambda b, h, i: (b, i, h)),
        out_shape=jax.ShapeDtypeStruct((batch, seq, C_D), BF16),
        scratch_shapes=[pltpu.VMEM((seq, DV_D), BF16)],
        compiler_params=_cparams("parallel", "parallel", "arbitrary"),
        name="diff_attention",
    )(lam_p, proj_bd, proj_bd, proj_cd, cos_t, sin_t, subln)


def _router_kernel(h_ref, g_ref, wr_ref, n_ref, comb_ref, idx_ref):
    x = h_ref[...]
    ms = jnp.mean(x * x, axis=-1, keepdims=True)
    n = x * lax.rsqrt(ms + RMS_EPS) * g_ref[...]
    n_ref[...] = n.astype(n_ref.dtype)
    n_hi, n_lo = _split2(n)
    w_hi, w_lo = _split2(wr_ref[...])
    logits = _dot(n_hi, w_hi) + _dot(n_lo, w_hi) + _dot(n_hi, w_lo)
    lane = lax.broadcasted_iota(jnp.int32, logits.shape, 1)
    neg = -jnp.inf
    logits = jnp.where(lane < N_EXPERTS, logits, neg)
    m1 = jnp.max(logits, axis=-1, keepdims=True)
    i1 = jnp.min(jnp.where(logits == m1, lane, LANES), axis=-1, keepdims=True)
    rest = jnp.where(lane == i1, neg, logits)
    m2 = jnp.max(rest, axis=-1, keepdims=True)
    i2 = jnp.min(jnp.where(rest == m2, lane, LANES), axis=-1, keepdims=True)
    e = jnp.exp(m2 - m1)
    g1 = 1.0 / (1.0 + e)
    g2 = e / (1.0 + e)
    comb_ref[...] = jnp.where(lane == i1, g1, 0.0) + jnp.where(lane == i2, g2, 0.0)
    idx_ref[...] = jnp.where(lane == 0, i1, jnp.where(lane == 1, i2, 0))


def moe_router(h, g, w_router_pad, tm=256):
    m, d = h.shape
    spec = pl.BlockSpec((tm, LANES), lambda i: (i, 0))
    return pl.pallas_call(
        _router_kernel,
        grid=(m // tm,),
        in_specs=[pl.BlockSpec((tm, d), lambda i: (i, 0)), pl.BlockSpec((1, d), lambda i: (0, 0)),
                  pl.BlockSpec((d, LANES), lambda i: (0, 0))],
        out_specs=[pl.BlockSpec((tm, d), lambda i: (i, 0)), spec, spec],
        out_shape=[jax.ShapeDtypeStruct((m, d), BF16), jax.ShapeDtypeStruct((m, LANES), F32),
                   jax.ShapeDtypeStruct((m, LANES), jnp.int32)],
        compiler_params=_cparams("parallel"),
        name="moe_router",
    )(h, g, w_router_pad)


def _tile_state(te_ref):
    i = pl.program_id(1)
    in_use = i < te_ref[pl.num_programs(1)]
    new_weights = jnp.logical_or(i == 0, te_ref[i] != te_ref[jnp.maximum(i - 1, 0)])
    return in_use, new_weights


def _gmm_up_kernel(te_ref, a_ref, wg_ref, wu_ref, o_ref, wg_bf, wu_bf):
    in_use, new_weights = _tile_state(te_ref)

    @pl.when(new_weights)
    def _():
        wg_bf[...] = wg_ref[...].astype(BF16)
        wu_bf[...] = wu_ref[...].astype(BF16)

    @pl.when(in_use)
    def _():
        a = a_ref[...]
        g = _dot(a, wg_bf[...])
        u = _dot(a, wu_bf[...])
        o_ref[...] = (g * jax.nn.sigmoid(g) * u).astype(o_ref.dtype)

    @pl.when(jnp.logical_not(in_use))
    def _():
        o_ref[...] = jnp.zeros_like(o_ref)


def _row_spec(tm, width, n_tiles):
    return pl.BlockSpec((tm, width), lambda j, i, te: (jnp.minimum(i, te[n_tiles] - 1), 0))


def gmm_swiglu_up(tile_table, xs, wg, wu, tm, tn):
    s, k = xs.shape
    n = wg.shape[2]
    n_tiles = s // tm
    w_spec = pl.BlockSpec((None, k, tn), lambda j, i, te: (te[i], 0, j))
    return pl.pallas_call(
        _gmm_up_kernel,
        grid_spec=pltpu.PrefetchScalarGridSpec(
            num_scalar_prefetch=1,
            grid=(n // tn, n_tiles),
            in_specs=[_row_spec(tm, k, n_tiles), w_spec, w_spec],
            out_specs=pl.BlockSpec((tm, tn), lambda j, i, te: (i, j)),
            scratch_shapes=[pltpu.VMEM((k, tn), BF16), pltpu.VMEM((k, tn), BF16)],
        ),
        out_shape=jax.ShapeDtypeStruct((s, n), BF16),
        compiler_params=_cparams("arbitrary", "arbitrary"),
        name="gmm_swiglu_up",
    )(tile_table, xs, wg, wu)


def _gmm_down_kernel(te_ref, a_ref, w_ref, gate_ref, o_ref):
    in_use = pl.program_id(0) < te_ref[pl.num_programs(0)]

    @pl.when(in_use)
    def _():
        o_ref[...] = _dot(a_ref[...], w_ref[...]) * gate_ref[...]

    @pl.when(jnp.logical_not(in_use))
    def _():
        o_ref[...] = jnp.zeros_like(o_ref)


def gmm_down(tile_table, hs, wd, slot_gate, tm, tn=1024):
    s, k = hs.shape
    n = wd.shape[2]
    n_tiles = s // tm
    return pl.pallas_call(
        _gmm_down_kernel,
        grid_spec=pltpu.PrefetchScalarGridSpec(
            num_scalar_prefetch=1,
            grid=(n_tiles, n // tn),
            in_specs=[pl.BlockSpec((tm, k), lambda i, j, te: (jnp.minimum(i, te[n_tiles] - 1), 0)),
                      pl.BlockSpec((None, k, tn), lambda i, j, te: (te[i], 0, jnp.where(i < te[n_tiles], j, 0))),
                      pl.BlockSpec((tm, 1), lambda i, j, te: (jnp.minimum(i, te[n_tiles] - 1), 0))],
            out_specs=pl.BlockSpec((tm, tn), lambda i, j, te: (i, j)),
        ),
        out_shape=jax.ShapeDtypeStruct((s, n), F32),
        compiler_params=_cparams("parallel", "arbitrary"),
        name="gmm_down",
    )(tile_table, hs, wd, slot_gate)


def moe_layer(h, ffn_g, w_router, wg, wu, wd, tm=MOE_TM):
    m, d = h.shape
    w_router_pad = jnp.pad(w_router.astype(F32), ((0, 0), (0, LANES - N_EXPERTS)))
    n_b, comb, idx = moe_router(h, ffn_g.reshape(1, d).astype(F32), w_router_pad)
    top_idx = idx[:, 0:2]
    gates = jnp.take_along_axis(comb[:, 0:N_EXPERTS], top_idx, axis=1)

    flat_e = top_idx.reshape(-1)
    n_pairs = 2 * m
    onehot = (flat_e[:, None] == jnp.arange(N_EXPERTS, dtype=jnp.int32)[None, :]).astype(jnp.int32)
    counts = jnp.sum(onehot, axis=0)
    rank = jnp.sum((jnp.cumsum(onehot, axis=0) - onehot) * onehot, axis=1)
    padded = ((counts + tm - 1) // tm) * tm
    pad_off = jnp.cumsum(padded) - padded
    raw_off = jnp.cumsum(counts) - counts
    dest = pad_off[flat_e] + rank
    n_tiles = n_pairs // tm + N_EXPERTS
    n_slots = n_tiles * tm
    order = jnp.argsort(flat_e, stable=True).astype(jnp.int32)
    tile_start = jnp.arange(n_tiles, dtype=jnp.int32) * tm
    pad_end = pad_off + padded
    tile_expert = jnp.minimum(jnp.sum((tile_start[:, None] >= pad_end[None, :]).astype(jnp.int32), axis=1),
                              N_EXPERTS - 1).astype(jnp.int32)
    slot = jnp.arange(n_slots, dtype=jnp.int32)
    slot_e = jnp.repeat(tile_expert, tm)
    local = slot - pad_off[slot_e]
    valid = (local < counts[slot_e]) & (slot < pad_end[N_EXPERTS - 1])
    src_pair = order[jnp.clip(raw_off[slot_e] + local, 0, n_pairs - 1)]
    slot_token = jnp.where(valid, src_pair // 2, 0)
    slot_gate = jnp.where(valid, gates.reshape(-1)[src_pair], 0.0).astype(F32).reshape(n_slots, 1)

    tiles_in_use = (pad_end[N_EXPERTS - 1] // tm).astype(jnp.int32).reshape(1)
    tile_table = jnp.concatenate([tile_expert, tiles_in_use])

    xs = jnp.take(n_b, slot_token, axis=0, mode="clip")
    hs = gmm_swiglu_up(tile_table, xs, wg, wu, tm, min(MOE_TN, wg.shape[2]))
    ys = gmm_down(tile_table, hs, cast_experts_bf16(wd), slot_gate, tm, min(1024, wd.shape[2]))
    dest2 = dest.reshape(m, 2)
    return h + jnp.take(ys, dest2[:, 0], axis=0) + jnp.take(ys, dest2[:, 1], axis=0)


def _rwkv_columns(w_in, layer):
    d = w_in.shape[0]
    o_w = C_A
    o_k = o_w + W_LORA
    o_v = o_k + C_A
    o_a = o_v + C_A
    o_g = o_a + A_LORA
    parts = [w_in[:, 0:C_A], w_in[:, o_k:o_k + C_A], w_in[:, o_v:o_v + C_A],
             w_in[:, o_w:o_w + W_LORA], w_in[:, o_a:o_a + A_LORA], w_in[:, o_g:o_g + G_LORA]]
    used = W_LORA + A_LORA + G_LORA
    if layer > 0:
        parts.append(w_in[:, N_IN0:N_IN0 + V_LORA])
        used += V_LORA
    parts.append(jnp.zeros((d, LORA_PAD - used), w_in.dtype))
    return jnp.concatenate(parts, axis=1)


def _rwkv_mu(mu, mu_v):
    o_w = C_A
    o_k = o_w + W_LORA
    o_v = o_k + C_A
    o_a = o_v + C_A
    o_g = o_a + A_LORA
    parts = [mu[0:C_A], mu[o_k:o_k + C_A], mu[o_v:o_v + C_A],
             mu[o_w:o_w + W_LORA], mu[o_a:o_a + A_LORA], mu[o_g:o_g + G_LORA]]
    used = W_LORA + A_LORA + G_LORA
    if mu_v is not None:
        parts.append(mu_v)
        used += V_LORA
    parts.append(jnp.zeros((LORA_PAD - used,), mu.dtype))
    return jnp.concatenate(parts).reshape(1, N_A_COLS).astype(F32)


def _pad_rows(w, offset):
    return jnp.pad(w, ((offset, LORA_PAD - offset - w.shape[0]), (0, 0))).astype(BF16)


def kernel(x, p, positions, attn_norm, w_in0, w_in_rest, w_out, rwkv_mu, rwkv_mu_v, rwkv_w0, rwkv_w_up, rwkv_a0, rwkv_a_up, rwkv_v0, rwkv_v_up, rwkv_g_up, rwkv_k_k, rwkv_k_a, rwkv_r_k, rwkv_lnx_g, rwkv_lnx_b, gm_ln_g, gm_ln_b, gm_w_s, gm_b_s, gm_out_g, sb_out_g, diff_lambda, diff_subln, ffn_norm, dense_w_gate, dense_w_up, dense_w_down, moe_router, moe_w_gate, moe_w_up, moe_w_down, ple_norm, ple_w_gate, ple_w_proj, final_norm):
    batch, seq, d = x.shape
    m = batch * seq
    h = x.reshape(m, d).astype(F32)

    inv_freq = 1.0 / (ROPE_THETA ** (jnp.arange(0, DH_D, 2, dtype=F32) / DH_D))
    invf = jnp.tile(inv_freq, LANES // (DH_D // 2)).reshape(1, LANES)
    cos_t, sin_t = rope_tables(positions.reshape(m, 1), invf)

    v_first = None
    for i in range(DEPTH):
        w_in = w_in0 if i == 0 else w_in_rest[i - 1]
        o_b = N_RWKV
        o_c = o_b + N_GMLP
        o_d = o_c + N_SB
        w_a = _rwkv_columns(w_in, i).astype(BF16)
        w_bd = jnp.concatenate([w_in[:, o_b:o_c], w_in[:, o_d:o_d + 2 * C_D]], axis=1).astype(BF16)
        w_cd = jnp.concatenate([w_in[:, o_c:o_d], w_in[:, o_d + 2 * C_D:o_d + 3 * C_D]], axis=1).astype(BF16)

        n = rmsnorm(h, attn_norm[i], BF16)
        proj_a = matmul(n, w_a, F32)
        proj_bd = matmul(n, w_bd, F32)
        proj_cd = matmul(n, w_cd, BF16)

        mu = _rwkv_mu(rwkv_mu[i], rwkv_mu_v[i - 1] if i > 0 else None)
        zeros_c = jnp.zeros((C_A,), F32)
        vec = jnp.stack([rwkv_w0[i], rwkv_a0[i], rwkv_v0[i - 1] if i > 0 else zeros_c, rwkv_k_k[i], rwkv_k_a[i],
                         rwkv_r_k[i].reshape(C_A), rwkv_lnx_g[i], rwkv_lnx_b[i]]).astype(F32)
        uw = _pad_rows(rwkv_w_up[i], 0)
        ua = _pad_rows(rwkv_a_up[i], W_LORA)
        ug = _pad_rows(rwkv_g_up[i], W_LORA + A_LORA)
        uv = _pad_rows(rwkv_v_up[i - 1], W_LORA + A_LORA + G_LORA) if i > 0 else jnp.zeros((LORA_PAD, C_A), BF16)
        vf_in = v_first if i > 0 else proj_a
        y_a, v_out = rwkv_mix(proj_a, vf_in, mu, vec, uw, ua, ug, uv, batch, seq, has_v_res=i > 0)
        if i == 0:
            v_first = v_out

        y_b = gmlp_mix(proj_bd, gm_ln_g[i].reshape(1, C_B), gm_ln_b[i].reshape(1, C_B), gm_w_s[i],
                       gm_b_s[i].T, gm_out_g[i].reshape(1, C_B))

        proj_cd3 = proj_cd.reshape(batch, seq, 4 * C_C)
        y_c = sb_attention(proj_cd3, sb_out_g[i].reshape(1, C_C), batch, seq).reshape(m, C_C)

        lam_init = 0.8 - 0.6 * math.exp(-0.3 * i)
        y_d = diff_attention(diff_lambda[i].astype(F32), proj_bd.reshape(batch, seq, N_GMLP + 2 * C_D), proj_cd3,
                             cos_t.reshape(batch, seq, LANES), sin_t.reshape(batch, seq, LANES),
                             diff_subln[i].reshape(1, DV_D), batch, seq, lam_init).reshape(m, C_D)

        h = mix_out_proj((y_a, y_b, y_c, y_d), w_out[i].astype(BF16), h)

        j = i // 2
        if i % 2 == 0:
            n = rmsnorm(h, ffn_norm[i], BF16)
            n_row_tiles = m // DENSE_TM
            one_group = jnp.concatenate([jnp.zeros((n_row_tiles,), jnp.int32), jnp.full((1,), n_row_tiles, jnp.int32)])
            hid = gmm_swiglu_up(one_group, n, dense_w_gate[j][None], dense_w_up[j][None], DENSE_TM, DENSE_TN)
            h = down_proj_residual(hid, dense_w_down[j].astype(BF16), h)
        else:
            h = moe_layer(h, ffn_norm[i], moe_router[j], moe_w_gate[j], moe_w_up[j], moe_w_down[j])

        n = rmsnorm(h, ple_norm[i], BF16)
        h = ple_residual(n, ple_w_gate[i].astype(BF16), p[i].reshape(m, P_DIM).astype(BF16),
                         ple_w_proj[i].astype(BF16), h)

    return rmsnorm(h, final_norm, x.dtype).reshape(batch, seq, d)
```
